```python
import jax, jax.numpy as jnp
from jax import lax
import numpy as np

D_MODEL = 1024
BATCH = 16
SEQ = 2048
DEPTH = 2

N_MEM = 256
POOL_WINDOWS = (2, 4, 8, 16)
N_POOL_GROUPS = 4
POOL_WIDTH = D_MODEL // 2
POOL_GROUP = POOL_WIDTH // N_POOL_GROUPS
CONV_WIDTH = D_MODEL // 2
CONV_K = 3
N_XHEADS = 4
XHEAD_DIM = D_MODEL // N_XHEADS
D_FF = 2816
N_EXPERTS = 8
TOP_K = 2
D_FF_EXPERT = 3584
EPS = 1e-6
N_DENSE = (DEPTH + 1) // 2
N_MOE = DEPTH // 2
IN_WIDTH = POOL_WIDTH + 3 * CONV_WIDTH + 2 * D_MODEL

kernel_name = "hybrid_pool_shortconv_xattn_moe"


def rmsnorm(x, g):
    xf = x.astype(jnp.float32)
    y = xf * lax.rsqrt(jnp.mean(xf * xf, axis=-1, keepdims=True) + EPS)
    return (y * g.astype(jnp.float32)).astype(x.dtype)


def pool_mixer(u, w_group, scale):
    s = u.shape[1]
    uf = u.astype(jnp.float32)
    pos1 = jnp.arange(1, s + 1, dtype=jnp.float32)[None, :, None]
    outs = []
    for g, w in enumerate(POOL_WINDOWS):
        ug = uf[..., g * POOL_GROUP:(g + 1) * POOL_GROUP]
        cs = jnp.cumsum(ug, axis=1)
        lag = jnp.pad(cs, ((0, 0), (w, 0), (0, 0)))[:, :s]
        mean = (cs - lag) / jnp.minimum(pos1, float(w))
        outs.append(mean - ug)
    p = jnp.stack(outs, axis=2).astype(u.dtype)
    y = jnp.einsum('bsgc,gcd->bsgd', p, w_group)
    return y.reshape(u.shape) * scale


def short_conv(z, conv_w):
    s = z.shape[1]
    zp = jnp.pad(z, ((0, 0), (CONV_K - 1, 0), (0, 0)))
    y = conv_w[0] * zp[:, 0:s]
    for k in range(1, CONV_K):
        y = y + conv_w[k] * zp[:, k:k + s]
    return y


def cross_attention(h, mem_n, wq, wk, wv, wo):
    b, s, _ = h.shape
    m = mem_n.shape[1]
    q = (h @ wq).reshape(b, s, N_XHEADS, XHEAD_DIM)
    k = (mem_n @ wk).reshape(b, m, N_XHEADS, XHEAD_DIM)
    v = (mem_n @ wv).reshape(b, m, N_XHEADS, XHEAD_DIM)
    sc = jnp.einsum('bshd,bmhd->bhsm', q, k).astype(jnp.float32) * (XHEAD_DIM ** -0.5)
    p = jax.nn.softmax(sc, axis=-1).astype(v.dtype)
    o = jnp.einsum('bhsm,bmhd->bshd', p, v).reshape(b, s, D_MODEL)
    return o @ wo


def swiglu(h, wg, wu, wd):
    return (jax.nn.silu(h @ wg) * (h @ wu)) @ wd


def moe_ffn(h, w_router, wg, wu, wd):
    b, s, d = h.shape
    hf = h.reshape(b * s, d)
    logits = (hf @ w_router).astype(jnp.float32)
    topv, topi = lax.top_k(logits, TOP_K)
    wts = jax.nn.softmax(topv, axis=-1)
    gate = jnp.sum(jax.nn.one_hot(topi, N_EXPERTS, dtype=jnp.float32) * wts[..., None], axis=1)
    out = jnp.zeros_like(hf)
    for e in range(N_EXPERTS):
        out = out + gate[:, e:e + 1].astype(hf.dtype) * swiglu(hf, wg[e], wu[e], wd[e])
    return out.reshape(b, s, d)


def setup_inputs(seed: int = 0) -> dict:
    key = jax.random.key(seed)
    ks = jax.random.split(key, 32)
    f32 = jnp.float32

    def nrm(k, shape, fan_in):
        return jax.random.normal(k, shape, f32) * (fan_in ** -0.5)

    def gain(k, shape):
        return 1.0 + 0.05 * jax.random.normal(k, shape, f32)

    return {
        "x": jax.random.normal(ks[0], (BATCH, SEQ, D_MODEL), f32),
        "mem": jax.random.normal(ks[1], (BATCH, N_MEM, D_MODEL), f32),
        "g_mix": gain(ks[2], (DEPTH, D_MODEL)),
        "w_in": nrm(ks[3], (DEPTH, D_MODEL, IN_WIDTH), D_MODEL),
        "w_pool_group": nrm(ks[4], (DEPTH, N_POOL_GROUPS, POOL_GROUP, POOL_GROUP), POOL_GROUP),
        "pool_scale": gain(ks[5], (DEPTH, POOL_WIDTH)),
        "w_pool_out": nrm(ks[6], (DEPTH, POOL_WIDTH, D_MODEL), POOL_WIDTH),
        "conv_w": nrm(ks[7], (DEPTH, CONV_K, CONV_WIDTH), CONV_K),
        "w_conv_out": nrm(ks[8], (DEPTH, CONV_WIDTH, D_MODEL), CONV_WIDTH),
        "w_mix_out": nrm(ks[9], (DEPTH, D_MODEL, D_MODEL), D_MODEL),
        "g_xattn": gain(ks[10], (DEPTH, D_MODEL)),
        "g_mem": gain(ks[11], (DEPTH, D_MODEL)),
        "w_xq": nrm(ks[12], (DEPTH, D_MODEL, D_MODEL), D_MODEL),
        "w_xk": nrm(ks[13], (DEPTH, D_MODEL, D_MODEL), D_MODEL),
        "w_xv": nrm(ks[14], (DEPTH, D_MODEL, D_MODEL), D_MODEL),
        "w_xo": nrm(ks[15], (DEPTH, D_MODEL, D_MODEL), D_MODEL),
        "g_ffn": gain(ks[16], (DEPTH, D_MODEL)),
        "w_ff_gate": nrm(ks[17], (N_DENSE, D_MODEL, D_FF), D_MODEL),
        "w_ff_up": nrm(ks[18], (N_DENSE, D_MODEL, D_FF), D_MODEL),
        "w_ff_down": nrm(ks[19], (N_DENSE, D_FF, D_MODEL), D_FF),
        "w_router": nrm(ks[20], (N_MOE, D_MODEL, N_EXPERTS), D_MODEL),
        "w_e_gate": nrm(ks[21], (N_MOE, N_EXPERTS, D_MODEL, D_FF_EXPERT), D_MODEL),
        "w_e_up": nrm(ks[22], (N_MOE, N_EXPERTS, D_MODEL, D_FF_EXPERT), D_MODEL),
        "w_e_down": nrm(ks[23], (N_MOE, N_EXPERTS, D_FF_EXPERT, D_MODEL), D_FF_EXPERT),
        "g_final": gain(ks[24], (D_MODEL,)),
    }


def reference(x, mem, g_mix, w_in, w_pool_group, pool_scale, w_pool_out, conv_w, w_conv_out, w_mix_out,
              g_xattn, g_mem, w_xq, w_xk, w_xv, w_xo, g_ffn, w_ff_gate, w_ff_up, w_ff_down,
              w_router, w_e_gate, w_e_up, w_e_down, g_final):
    o_c = POOL_WIDTH
    o_b = o_c + CONV_WIDTH
    o_u = o_b + CONV_WIDTH
    o_g = o_u + CONV_WIDTH
    for l in range(DEPTH):
        h = rmsnorm(x, g_mix[l])
        z = h @ w_in[l]
        u_pool = z[..., :o_c]
        c_gate = z[..., o_c:o_b]
        b_gate = z[..., o_b:o_u]
        u_conv = z[..., o_u:o_g]
        gates = jax.nn.sigmoid(z[..., o_g:].astype(jnp.float32)).astype(x.dtype)
        y_pool = pool_mixer(u_pool, w_pool_group[l], pool_scale[l]) @ w_pool_out[l]
        y_conv = (b_gate * short_conv(c_gate * u_conv, conv_w[l])) @ w_conv_out[l]
        merged = gates[..., :D_MODEL] * y_pool + gates[..., D_MODEL:] * y_conv
        x = x + merged @ w_mix_out[l]
        h = rmsnorm(x, g_xattn[l])
        mem_n = rmsnorm(mem, g_mem[l])
        x = x + cross_attention(h, mem_n, w_xq[l], w_xk[l], w_xv[l], w_xo[l])
        h = rmsnorm(x, g_ffn[l])
        if l % 2 == 0:
            i = l // 2
            x = x + swiglu(h, w_ff_gate[i], w_ff_up[i], w_ff_down[i])
        else:
            i = l // 2
            x = x + moe_ffn(h, w_router[i], w_e_gate[i], w_e_up[i], w_e_down[i])
    return rmsnorm(x, g_final)
```

```python
import functools

import jax
import jax.numpy as jnp
from jax import lax
from jax.experimental import pallas as pl
from jax.experimental.pallas import tpu as pltpu

F32 = jnp.float32
BF16 = jnp.bfloat16

EPS = 1e-6
POOL_WINDOWS = (2, 4, 8, 16)
CONV_K = 3
N_XHEADS = 4
TOP_K = 2

LANES = 128
SUBLANES = 8
VMEM_LIMIT_BYTES = 56 * 1024 * 1024

POOL_HALO = 16
CONV_HALO = 8

ROW_TILE = 512
EXPERT_ROW_TILE = 1024
EXPERT_FF_TILE = 512
COMBINE_ROW_TILE = 256


def _dot(a, b):
    return jnp.dot(a, b, preferred_element_type=F32)


def _rms(xf, g):
    ms = jnp.mean(xf * xf, axis=-1, keepdims=True)
    return (xf * lax.rsqrt(ms + EPS)) * g


def _resident(shape):
    nd = len(shape)
    return pl.BlockSpec(shape, lambda *_: (0,) * nd, pipeline_mode=pl.Buffered(1))


def _params(*sem):
    return pltpu.CompilerParams(dimension_semantics=sem, vmem_limit_bytes=VMEM_LIMIT_BYTES)


def _mixer_body(x_ref, g_ref, win_ref, wgrp_ref, ps_ref, wpo_ref, cw_ref, wco_ref, wmo_ref, o_ref,
                pool_buf, conv_buf, *, tm, seq):
    n_groups, _, gw = wgrp_ref.shape
    pw = n_groups * gw
    cwid = cw_ref.shape[1]
    d = x_ref.shape[1]
    o_c, o_b, o_u, o_g = pw, pw + cwid, pw + 2 * cwid, pw + 3 * cwid

    pos0 = lax.rem(pl.program_id(0) * tm, seq)

    @pl.when(pos0 == 0)
    def _():
        pool_buf[:, 0:POOL_HALO, :] = jnp.zeros((n_groups, POOL_HALO, gw), F32)
        conv_buf[:, 0:CONV_HALO, :] = jnp.zeros((conv_buf.shape[0], CONV_HALO, LANES), F32)

    @pl.when(pos0 != 0)
    def _():
        pool_buf[:, 0:POOL_HALO, :] = pool_buf[:, tm:tm + POOL_HALO, :]
        conv_buf[:, 0:CONV_HALO, :] = conv_buf[:, tm:tm + CONV_HALO, :]

    x = x_ref[...]
    h = _rms(x, g_ref[...]).astype(BF16)

    u_pool = _dot(h, win_ref[:, 0:o_c])
    pos1 = lax.broadcasted_iota(jnp.int32, (tm, gw), 0) + (pos0 + 1)
    ys = []
    for g, w in enumerate(POOL_WINDOWS):
        ug = u_pool[:, g * gw:(g + 1) * gw]
        pool_buf[g, POOL_HALO:POOL_HALO + tm, :] = ug
        acc = ug
        for k in range(1, w):
            acc = acc + pool_buf[g, POOL_HALO - k:POOL_HALO - k + tm, :]
        mean = acc / jnp.minimum(pos1, w).astype(F32)
        ys.append(_dot((mean - ug).astype(BF16), wgrp_ref[g]))
    y = jnp.concatenate(ys, axis=1) * ps_ref[...]
    y_pool = _dot(y.astype(BF16), wpo_ref[...])

    c_gate = _dot(h, win_ref[:, o_c:o_b])
    u_conv = _dot(h, win_ref[:, o_u:o_g])
    zc = c_gate * u_conv
    ycs = []
    for j in range(cwid // LANES):
        sl = slice(j * LANES, (j + 1) * LANES)
        zj = zc[:, sl]
        conv_buf[j, CONV_HALO:CONV_HALO + tm, :] = zj
        acc = cw_ref[0:1, sl] * conv_buf[j, CONV_HALO - (CONV_K - 1):CONV_HALO - (CONV_K - 1) + tm, :]
        for k in range(1, CONV_K - 1):
            off = CONV_HALO - (CONV_K - 1) + k
            acc = acc + cw_ref[k:k + 1, sl] * conv_buf[j, off:off + tm, :]
        ycs.append(acc + cw_ref[CONV_K - 1:CONV_K, sl] * zj)
    b_gate = _dot(h, win_ref[:, o_b:o_u])
    y_conv = _dot((b_gate * jnp.concatenate(ycs, axis=1)).astype(BF16), wco_ref[...])

    merged = jax.nn.sigmoid(_dot(h, win_ref[:, o_g:o_g + d])) * y_pool
    merged = merged + jax.nn.sigmoid(_dot(h, win_ref[:, o_g + d:o_g + 2 * d])) * y_conv
    o_ref[...] = x + _dot(merged.astype(BF16), wmo_ref[...])


def _mixer(x, g, win, wgrp, ps, wpo, cw, wco, wmo, *, seq, tm):
    n, d = x.shape
    n_groups, _, gw = wgrp.shape
    assert seq % tm == 0 and n % tm == 0 and gw == LANES and cw.shape[1] % LANES == 0
    assert len(POOL_WINDOWS) == n_groups and max(POOL_WINDOWS) - 1 <= POOL_HALO and cw.shape[0] == CONV_K
    tile = pl.BlockSpec((tm, d), lambda i: (i, 0))
    return pl.pallas_call(
        functools.partial(_mixer_body, tm=tm, seq=seq),
        grid=(n // tm,),
        in_specs=[tile, _resident(g.shape), _resident(win.shape), _resident(wgrp.shape), _resident(ps.shape),
                  _resident(wpo.shape), _resident(cw.shape), _resident(wco.shape), _resident(wmo.shape)],
        out_specs=tile,
        out_shape=jax.ShapeDtypeStruct((n, d), F32),
        scratch_shapes=[pltpu.VMEM((n_groups, tm + POOL_HALO, gw), F32),
                        pltpu.VMEM((cw.shape[1] // LANES, tm + CONV_HALO, LANES), F32)],
        compiler_params=_params("arbitrary"),
        name="mixer",
    )(x, g, win, wgrp, ps, wpo, cw, wco, wmo)


def _memkv_body(mem_ref, g_ref, wk_ref, wv_ref, kt_ref, v_ref):
    mn = _rms(mem_ref[...], g_ref[...]).astype(BF16)
    kt_ref[...] = _dot(mn, wk_ref[...]).T.astype(BF16)
    v_ref[...] = _dot(mn, wv_ref[...]).astype(BF16)


def _memkv(mem, g_mem, wk, wv):
    b, m, d = mem.shape
    depth = wk.shape[0]
    per_layer = lambda l, i: (l, 0, 0)
    return pl.pallas_call(
        _memkv_body,
        grid=(depth, b),
        in_specs=[pl.BlockSpec((None, m, d), lambda l, i: (i, 0, 0)),
                  pl.BlockSpec((None, 1, d), per_layer),
                  pl.BlockSpec((None, d, d), per_layer),
                  pl.BlockSpec((None, d, d), per_layer)],
        out_specs=[pl.BlockSpec((None, None, d, m), lambda l, i: (l, i, 0, 0)),
                   pl.BlockSpec((None, None, m, d), lambda l, i: (l, i, 0, 0))],
        out_shape=[jax.ShapeDtypeStruct((depth, b, d, m), BF16), jax.ShapeDtypeStruct((depth, b, m, d), BF16)],
        compiler_params=_params("arbitrary", "arbitrary"),
        name="memkv",
    )(mem, g_mem, wk, wv)


def _xattn_body(x_ref, g_ref, wq_ref, kt_ref, v_ref, wo_ref, o_ref):
    x = x_ref[...]
    h = _rms(x, g_ref[...]).astype(BF16)
    q = _dot(h, wq_ref[...])
    dh = q.shape[1] // N_XHEADS
    heads = []
    for hd in range(N_XHEADS):
        sl = slice(hd * dh, (hd + 1) * dh)
        s = _dot(q[:, sl].astype(BF16), kt_ref[sl, :]) * (dh ** -0.5)
        e = jnp.exp(s - jnp.max(s, axis=-1, keepdims=True))
        p = e / jnp.sum(e, axis=-1, keepdims=True)
        heads.append(_dot(p.astype(BF16), v_ref[:, sl]))
    o = jnp.concatenate(heads, axis=1).astype(BF16)
    o_ref[...] = x + _dot(o, wo_ref[...])


def _xattn(x, g, wq, kt, v, wo, *, layer, seq, tm):
    n, d = x.shape
    m = v.shape[2]
    assert seq % tm == 0 and d % N_XHEADS == 0
    tiles_per_seq = seq // tm
    tile = pl.BlockSpec((tm, d), lambda i: (i, 0))
    return pl.pallas_call(
        _xattn_body,
        grid=(n // tm,),
        in_specs=[tile, _resident(g.shape), _resident(wq.shape),
                  pl.BlockSpec((None, None, d, m), lambda i: (layer, i // tiles_per_seq, 0, 0)),
                  pl.BlockSpec((None, None, m, d), lambda i: (layer, i // tiles_per_seq, 0, 0)),
                  _resident(wo.shape)],
        out_specs=tile,
        out_shape=jax.ShapeDtypeStruct((n, d), F32),
        compiler_params=_params("arbitrary"),
        name="xattn",
    )(x, g, wq, kt, v, wo)


def _swiglu(h, wg, wu, wd):
    g = _dot(h, wg)
    return _dot((g * jax.nn.sigmoid(g) * _dot(h, wu)).astype(BF16), wd)


def _ffn_body(x_ref, g_ref, wg_ref, wu_ref, wd_ref, o_ref):
    x = x_ref[...]
    h = _rms(x, g_ref[...]).astype(BF16)
    o_ref[...] = x + _swiglu(h, wg_ref[...], wu_ref[...], wd_ref[...])


def _ffn(x, g, wg, wu, wd, *, tm):
    n, d = x.shape
    tile = pl.BlockSpec((tm, d), lambda i: (i, 0))
    return pl.pallas_call(
        _ffn_body,
        grid=(n // tm,),
        in_specs=[tile, _resident(g.shape), _resident(wg.shape), _resident(wu.shape), _resident(wd.shape)],
        out_specs=tile,
        out_shape=jax.ShapeDtypeStruct((n, d), F32),
        compiler_params=_params("arbitrary"),
        name="ffn",
    )(x, g, wg, wu, wd)


def _router_body(x_ref, g_ref, wrt_ref, h_ref, idx_ref, wt_ref):
    tm = x_ref.shape[0]
    n_exp = wrt_ref.shape[0]
    h = _rms(x_ref[...], g_ref[...])
    h_ref[...] = h
    logits = [jnp.sum(h * wrt_ref[e:e + 1, :], axis=-1, keepdims=True) for e in range(n_exp)]
    m1, i1 = logits[0], jnp.zeros((tm, 1), jnp.int32)
    for e in range(1, n_exp):
        better = logits[e] > m1
        m1 = jnp.where(better, logits[e], m1)
        i1 = jnp.where(better, e, i1)
    m2, i2 = jnp.full((tm, 1), -jnp.inf, F32), jnp.zeros((tm, 1), jnp.int32)
    for e in range(n_exp):
        better = jnp.logical_and(i1 != e, logits[e] > m2)
        m2 = jnp.where(better, logits[e], m2)
        i2 = jnp.where(better, e, i2)
    e2 = jnp.exp(m2 - m1)
    den = 1.0 + e2
    lane = lax.broadcasted_iota(jnp.int32, (tm, LANES), 1)
    idx_ref[...] = jnp.where(lane == 0, i1, jnp.where(lane == 1, i2, 0))
    wt_ref[...] = jnp.where(lane == 0, 1.0 / den, jnp.where(lane == 1, e2 / den, 0.0))


def _router(x, g, wrt, *, tm):
    n, d = x.shape
    tile = pl.BlockSpec((tm, d), lambda i: (i, 0))
    lanes = pl.BlockSpec((tm, LANES), lambda i: (i, 0))
    return pl.pallas_call(
        _router_body,
        grid=(n // tm,),
        in_specs=[tile, _resident(g.shape), _resident(wrt.shape)],
        out_specs=[tile, lanes, lanes],
        out_shape=[jax.ShapeDtypeStruct((n, d), F32), jax.ShapeDtypeStruct((n, LANES), jnp.int32),
                   jax.ShapeDtypeStruct((n, LANES), F32)],
        compiler_params=_params("arbitrary"),
        name="router",
    )(x, g, wrt)


def _row_copy(src_hbm, dst_vmem, sem, src_row, dst_row):
    return pltpu.make_async_copy(src_hbm.at[pl.ds(src_row, 1)], dst_vmem.at[pl.ds(dst_row, 1)], sem)


def _expert_body(te_ref, tv_ref, tok_ref, h_hbm, wg_ref, wu_ref, wd_ref, y_ref, hrows, hb, sem):
    i, f = pl.program_id(0), pl.program_id(1)
    tmb = hrows.shape[0]
    valid = tv_ref[i] != 0

    @pl.when(jnp.logical_and(valid, f == 0))
    def _():
        def start(r, c):
            _row_copy(h_hbm, hrows, sem, tok_ref[0, 0, r], r).start()
            return c
        lax.fori_loop(0, tmb, start, 0)

        def wait(r, c):
            _row_copy(h_hbm, hrows, sem, 0, r).wait()
            return c
        lax.fori_loop(0, tmb, wait, 0)
        hb[...] = hrows[...].astype(BF16)

    @pl.when(jnp.logical_and(jnp.logical_not(valid), f == 0))
    def _():
        y_ref[...] = jnp.zeros(y_ref.shape, F32)

    @pl.when(valid)
    def _():
        contrib = _swiglu(hb[...], wg_ref[...], wu_ref[...], wd_ref[...])

        @pl.when(f == 0)
        def _():
            y_ref[...] = contrib

        @pl.when(f != 0)
        def _():
            y_ref[...] += contrib


def _experts(tile_expert, tile_valid, tok_of_slot, h, wg, wu, wd, *, tmb, fc):
    n_tiles = tile_expert.shape[0]
    _, d, ff = wg.shape
    assert ff % fc == 0
    return pl.pallas_call(
        _expert_body,
        grid_spec=pltpu.PrefetchScalarGridSpec(
            num_scalar_prefetch=2,
            grid=(n_tiles, ff // fc),
            in_specs=[pl.BlockSpec((1, 1, tmb), lambda i, f, te, tv: (i, 0, 0), memory_space=pltpu.SMEM),
                      pl.BlockSpec(memory_space=pl.ANY),
                      pl.BlockSpec((None, d, fc), lambda i, f, te, tv: (te[i], 0, f)),
                      pl.BlockSpec((None, d, fc), lambda i, f, te, tv: (te[i], 0, f)),
                      pl.BlockSpec((None, fc, d), lambda i, f, te, tv: (te[i], f, 0))],
            out_specs=pl.BlockSpec((tmb, d), lambda i, f, te, tv: (i, 0)),
            scratch_shapes=[pltpu.VMEM((tmb, d), F32), pltpu.VMEM((tmb, d), BF16), pltpu.SemaphoreType.DMA(())],
        ),
        out_shape=jax.ShapeDtypeStruct((n_tiles * tmb, d), F32),
        compiler_params=_params("arbitrary", "arbitrary"),
        name="experts",
    )(tile_expert, tile_valid, tok_of_slot.reshape(n_tiles, 1, tmb), h, wg, wu, wd)


def _combine_body(pos_ref, x_ref, wt_ref, g_ref, y_hbm, o_ref, yrows, sem, *, final_norm):
    tm = x_ref.shape[0]

    def start(r, c):
        for k in range(TOP_K):
            _row_copy(y_hbm, yrows.at[k], sem, pos_ref[0, 0, TOP_K * r + k], r).start()
        return c
    lax.fori_loop(0, tm, start, 0)

    def wait(r, c):
        for k in range(TOP_K):
            _row_copy(y_hbm, yrows.at[k], sem, 0, r).wait()
        return c
    lax.fori_loop(0, tm, wait, 0)

    moe = wt_ref[:, 0:1] * yrows[0]
    for k in range(1, TOP_K):
        moe = moe + wt_ref[:, k:k + 1] * yrows[k]
    out = x_ref[...] + moe
    o_ref[...] = _rms(out, g_ref[...]) if final_norm else out


def _combine(pos, x, wt, g, y, *, tm, final_norm):
    n, d = x.shape
    tile = pl.BlockSpec((tm, d), lambda i: (i, 0))
    return pl.pallas_call(
        functools.partial(_combine_body, final_norm=final_norm),
        grid=(n // tm,),
        in_specs=[pl.BlockSpec((1, 1, TOP_K * tm), lambda i: (i, 0, 0), memory_space=pltpu.SMEM),
                  tile, pl.BlockSpec((tm, LANES), lambda i: (i, 0)), _resident(g.shape),
                  pl.BlockSpec(memory_space=pl.ANY)],
        out_specs=tile,
        out_shape=jax.ShapeDtypeStruct((n, d), F32),
        scratch_shapes=[pltpu.VMEM((TOP_K, tm, d), F32), pltpu.SemaphoreType.DMA(())],
        compiler_params=_params("arbitrary"),
        name="combine",
    )(pos.reshape(n // tm, 1, TOP_K * tm), x, wt, g, y)


def _routing_tables(idx, n_exp, tmb):
    n = idx.shape[0]
    e_flat = idx[:, :TOP_K].reshape(-1)
    onehot = (e_flat[:, None] == jnp.arange(n_exp, dtype=jnp.int32)[None, :]).astype(jnp.int32)
    csum = jnp.cumsum(onehot, axis=0)
    rank = jnp.sum(onehot * (csum - 1), axis=1)
    counts = csum[-1]
    padded = ((counts + tmb - 1) // tmb) * tmb
    ends = jnp.cumsum(padded)
    pos = (ends - padded)[e_flat] + rank
    n_tiles = (n * TOP_K) // tmb + n_exp
    tok_of_slot = jnp.zeros((n_tiles * tmb,), jnp.int32).at[pos].set(jnp.arange(n * TOP_K, dtype=jnp.int32) // TOP_K)
    tile_start = jnp.arange(n_tiles, dtype=jnp.int32) * tmb
    tile_expert = jnp.minimum(jnp.searchsorted(ends, tile_start, side="right"), n_exp - 1).astype(jnp.int32)
    tile_valid = (tile_start < ends[-1]).astype(jnp.int32)
    return pos.astype(jnp.int32), tok_of_slot, tile_expert, tile_valid


def _moe(x, g, wrt, wg, wu, wd, g_out, *, final_norm):
    n_exp = wrt.shape[0]
    h, idx, wt = _router(x, g, wrt, tm=ROW_TILE)
    pos, tok_of_slot, tile_expert, tile_valid = _routing_tables(idx, n_exp, EXPERT_ROW_TILE)
    y = _experts(tile_expert, tile_valid, tok_of_slot, h, wg, wu, wd, tmb=EXPERT_ROW_TILE, fc=EXPERT_FF_TILE)
    return _combine(pos, x, wt, g_out, y, tm=COMBINE_ROW_TILE, final_norm=final_norm)


def _final_norm_body(x_ref, g_ref, o_ref):
    o_ref[...] = _rms(x_ref[...], g_ref[...])


def _final_norm(x, g, *, tm):
    n, d = x.shape
    tile = pl.BlockSpec((tm, d), lambda i: (i, 0))
    return pl.pallas_call(
        _final_norm_body, grid=(n // tm,), in_specs=[tile, _resident(g.shape)], out_specs=tile,
        out_shape=jax.ShapeDtypeStruct((n, d), F32), compiler_params=_params("arbitrary"), name="final_norm",
    )(x, g)


def kernel(x, mem, g_mix, w_in, w_pool_group, pool_scale, w_pool_out, conv_w, w_conv_out, w_mix_out, g_xattn, g_mem, w_xq, w_xk, w_xv, w_xo, g_ffn, w_ff_gate, w_ff_up, w_ff_down, w_router, w_e_gate, w_e_up, w_e_down, g_final):
    b, s, d = x.shape
    depth = g_mix.shape[0]
    bf = lambda w: w.astype(BF16)
    row = lambda v: v.reshape(1, -1)

    kt, v = _memkv(mem, g_mem.reshape(depth, 1, d), bf(w_xk), bf(w_xv))
    xf = x.reshape(b * s, d)
    for l in range(depth):
        xf = _mixer(xf, row(g_mix[l]), bf(w_in[l]), bf(w_pool_group[l]), row(pool_scale[l]), bf(w_pool_out[l]),
                    conv_w[l], bf(w_conv_out[l]), bf(w_mix_out[l]), seq=s, tm=ROW_TILE)
        xf = _xattn(xf, row(g_xattn[l]), bf(w_xq[l]), kt, v, bf(w_xo[l]), layer=l, seq=s, tm=ROW_TILE)
        last = l == depth - 1
        i = l // 2
        if l % 2 == 0:
            xf = _ffn(xf, row(g_ffn[l]), bf(w_ff_gate[i]), bf(w_ff_up[i]), bf(w_ff_down[i]), tm=ROW_TILE)
            if last:
                xf = _final_norm(xf, row(g_final), tm=ROW_TILE)
        else:
            xf = _moe(xf, row(g_ffn[l]), w_router[i].T, bf(w_e_gate[i]), bf(w_e_up[i]), bf(w_e_down[i]),
                      row(g_final), final_norm=last)
    return xf.reshape(b, s, d)
```

```python
import functools

import jax
import jax.numpy as jnp
from jax import lax
from jax.experimental import pallas as pl
from jax.experimental.pallas import tpu as pltpu

F32 = jnp.float32
BF16 = jnp.bfloat16

EPS = 1e-6
POOL_WINDOWS = (2, 4, 8, 16)
CONV_K = 3
N_XHEADS = 4
TOP_K = 2

LANES = 128
SUBLANES = 8
VMEM_LIMIT_BYTES = 56 * 1024 * 1024

POOL_HALO = 16
CONV_HALO = 8

ROW_TILE = 512
EXPERT_ROW_TILE = 1024
EXPERT_FF_TILE = 512
COMBINE_ROW_TILE = 256


def _dot(a, b):
    return jnp.dot(a, b, preferred_element_type=F32)


def _rms(xf, g):
    ms = jnp.mean(xf * xf, axis=-1, keepdims=True)
    return (xf * lax.rsqrt(ms + EPS)) * g


def _resident(shape):
    nd = len(shape)
    return pl.BlockSpec(shape, lambda *_: (0,) * nd, pipeline_mode=pl.Buffered(1))


def _params(*sem):
    return pltpu.CompilerParams(dimension_semantics=sem, vmem_limit_bytes=VMEM_LIMIT_BYTES)


def _mixer_body(x_ref, g_ref, win_ref, wgrp_ref, ps_ref, wpo_ref, cw_ref, wco_ref, wmo_ref, o_ref,
                pool_buf, conv_buf, *, tm, seq):
    n_groups, _, gw = wgrp_ref.shape
    pw = n_groups * gw
    cwid = cw_ref.shape[1]
    d = x_ref.shape[1]
    o_c, o_b, o_u, o_g = pw, pw + cwid, pw + 2 * cwid, pw + 3 * cwid

    pos0 = lax.rem(pl.program_id(0) * tm, seq)

    @pl.when(pos0 == 0)
    def _():
        pool_buf[:, 0:POOL_HALO, :] = jnp.zeros((n_groups, POOL_HALO, gw), F32)
        conv_buf[:, 0:CONV_HALO, :] = jnp.zeros((conv_buf.shape[0], CONV_HALO, LANES), F32)

    @pl.when(pos0 != 0)
    def _():
        pool_buf[:, 0:POOL_HALO, :] = pool_buf[:, tm:tm + POOL_HALO, :]
        conv_buf[:, 0:CONV_HALO, :] = conv_buf[:, tm:tm + CONV_HALO, :]

    x = x_ref[...]
    h = _rms(x, g_ref[...]).astype(BF16)

    u_pool = _dot(h, win_ref[:, 0:o_c])
    pos1 = lax.broadcasted_iota(jnp.int32, (tm, gw), 0) + (pos0 + 1)
    ys = []
    for g, w in enumerate(POOL_WINDOWS):
        ug = u_pool[:, g * gw:(g + 1) * gw]
        pool_buf[g, POOL_HALO:POOL_HALO + tm, :] = ug
        acc = ug
        for k in range(1, w):
            acc = acc + pool_buf[g, POOL_HALO - k:POOL_HALO - k + tm, :]
        mean = acc / jnp.minimum(pos1, w).astype(F32)
        ys.append(_dot((mean - ug).astype(BF16), wgrp_ref[g]))
    y = jnp.concatenate(ys, axis=1) * ps_ref[...]
    y_pool = _dot(y.astype(BF16), wpo_ref[...])

    c_gate = _dot(h, win_ref[:, o_c:o_b])
    u_conv = _dot(h, win_ref[:, o_u:o_g])
    zc = c_gate * u_conv
    ycs = []
    for j in range(cwid // LANES):
        sl = slice(j * LANES, (j + 1) * LANES)
        zj = zc[:, sl]
        conv_buf[j, CONV_HALO:CONV_HALO + tm, :] = zj
        acc = cw_ref[0:1, sl] * conv_buf[j, CONV_HALO - (CONV_K - 1):CONV_HALO - (CONV_K - 1) + tm, :]
        for k in range(1, CONV_K - 1):
            off = CONV_HALO - (CONV_K - 1) + k
            acc = acc + cw_ref[k:k + 1, sl] * conv_buf[j, off:off + tm, :]
        ycs.append(acc + cw_ref[CONV_K - 1:CONV_K, sl] * zj)
    b_gate = _dot(h, win_ref[:, o_b:o_u])
    y_conv = _dot((b_gate * jnp.concatenate(ycs, axis=1)).astype(BF16), wco_ref[...])

    merged = jax.nn.sigmoid(_dot(h, win_ref[:, o_g:o_g + d])) * y_pool
    merged = merged + jax.nn.sigmoid(_dot(h, win_ref[:, o_g + d:o_g + 2 * d])) * y_conv
    o_ref[...] = x + _dot(merged.astype(BF16), wmo_ref[...])


def _mixer(x, g, win, wgrp, ps, wpo, cw, wco, wmo, *, seq, tm):
    n, d = x.shape
    n_groups, _, gw = wgrp.shape
    assert seq % tm == 0 and n % tm == 0 and gw == LANES and cw.shape[1] % LANES == 0
    assert len(POOL_WINDOWS) == n_groups and max(POOL_WINDOWS) - 1 <= POOL_HALO and cw.shape[0] == CONV_K
    tile = pl.BlockSpec((tm, d), lambda i: (i, 0))
    return pl.pallas_call(
        functools.partial(_mixer_body, tm=tm, seq=seq),
        grid=(n // tm,),
        in_specs=[tile, _resident(g.shape), _resident(win.shape), _resident(wgrp.shape), _resident(ps.shape),
                  _resident(wpo.shape), _resident(cw.shape), _resident(wco.shape), _resident(wmo.shape)],
        out_specs=tile,
        out_shape=jax.ShapeDtypeStruct((n, d), F32),
        scratch_shapes=[pltpu.VMEM((n_groups, tm + POOL_HALO, gw), F32),
                        pltpu.VMEM((cw.shape[1] // LANES, tm + CONV_HALO, LANES), F32)],
        compiler_params=_params("arbitrary"),
        name="mixer",
    )(x, g, win, wgrp, ps, wpo, cw, wco, wmo)


def _memkv_body(mem_ref, g_ref, wk_ref, wv_ref, kt_ref, v_ref):
    mn = _rms(mem_ref[...], g_ref[...]).astype(BF16)
    kt_ref[...] = _dot(mn, wk_ref[...]).T.astype(BF16)
    v_ref[...] = _dot(mn, wv_ref[...]).astype(BF16)


def _memkv(mem, g_mem, wk, wv):
    b, m, d = mem.shape
    depth = wk.shape[0]
    per_layer = lambda l, i: (l, 0, 0)
    return pl.pallas_call(
        _memkv_body,
        grid=(depth, b),
        in_specs=[pl.BlockSpec((None, m, d), lambda l, i: (i, 0, 0)),
                  pl.BlockSpec((None, 1, d), per_layer),
                  pl.BlockSpec((None, d, d), per_layer),
                  pl.BlockSpec((None, d, d), per_layer)],
        out_specs=[pl.BlockSpec((None, None, d, m), lambda l, i: (l, i, 0, 0)),
                   pl.BlockSpec((None, None, m, d), lambda l, i: (l, i, 0, 0))],
        out_shape=[jax.ShapeDtypeStruct((depth, b, d, m), BF16), jax.ShapeDtypeStruct((depth, b, m, d), BF16)],
        compiler_params=_params("arbitrary", "arbitrary"),
        name="memkv",
    )(mem, g_mem, wk, wv)


def _xattn_body(x_ref, g_ref, wq_ref, kt_ref, v_ref, wo_ref, o_ref):
    x = x_ref[...]
    h = _rms(x, g_ref[...]).astype(BF16)
    q = _dot(h, wq_ref[...])
    dh = q.shape[1] // N_XHEADS
    heads = []
    for hd in range(N_XHEADS):
        sl = slice(hd * dh, (hd + 1) * dh)
        s = _dot(q[:, sl].astype(BF16), kt_ref[sl, :]) * (dh ** -0.5)
        e = jnp.exp(s - jnp.max(s, axis=-1, keepdims=True))
        p = e / jnp.sum(e, axis=-1, keepdims=True)
        heads.append(_dot(p.astype(BF16), v_ref[:, sl]))
    o = jnp.concatenate(heads, axis=1).astype(BF16)
    o_ref[...] = x + _dot(o, wo_ref[...])


def _xattn(x, g, wq, kt, v, wo, *, layer, seq, tm):
    n, d = x.shape
    m = v.shape[2]
    assert seq % tm == 0 and d % N_XHEADS == 0
    tiles_per_seq = seq // tm
    tile = pl.BlockSpec((tm, d), lambda i: (i, 0))
    return pl.pallas_call(
        _xattn_body,
        grid=(n // tm,),
        in_specs=[tile, _resident(g.shape), _resident(wq.shape),
                  pl.BlockSpec((None, None, d, m), lambda i: (layer, i // tiles_per_seq, 0, 0)),
                  pl.BlockSpec((None, None, m, d), lambda i: (layer, i // tiles_per_seq, 0, 0)),
                  _resident(wo.shape)],
        out_specs=tile,
        out_shape=jax.ShapeDtypeStruct((n, d), F32),
        compiler_params=_params("arbitrary"),
        name="xattn",
    )(x, g, wq, kt, v, wo)


def _swiglu(h, wg, wu, wd):
    g = _dot(h, wg)
    return _dot((g * jax.nn.sigmoid(g) * _dot(h, wu)).astype(BF16), wd)


def _ffn_body(x_ref, g_ref, wg_ref, wu_ref, wd_ref, o_ref):
    x = x_ref[...]
    h = _rms(x, g_ref[...]).astype(BF16)
    o_ref[...] = x + _swiglu(h, wg_ref[...], wu_ref[...], wd_ref[...])


def _ffn(x, g, wg, wu, wd, *, tm):
    n, d = x.shape
    tile = pl.BlockSpec((tm, d), lambda i: (i, 0))
    return pl.pallas_call(
        _ffn_body,
        grid=(n // tm,),
        in_specs=[tile, _resident(g.shape), _resident(wg.shape), _resident(wu.shape), _resident(wd.shape)],
        out_specs=tile,
        out_shape=jax.ShapeDtypeStruct((n, d), F32),
        compiler_params=_params("arbitrary"),
        name="ffn",
    )(x, g, wg, wu, wd)


def _router_body(x_ref, g_ref, wrt_ref, h_ref, idx_ref, wt_ref, cnt_ref, base_ref):
    tm = x_ref.shape[0]
    n_exp = wrt_ref.shape[0]

    @pl.when(pl.program_id(0) == 0)
    def _():
        base_ref[...] = jnp.zeros(base_ref.shape, F32)

    h = _rms(x_ref[...], g_ref[...])
    h_ref[...] = h
    logits = [jnp.sum(h * wrt_ref[e:e + 1, :], axis=-1, keepdims=True) for e in range(n_exp)]
    m1, i1 = logits[0], jnp.zeros((tm, 1), jnp.int32)
    for e in range(1, n_exp):
        better = logits[e] > m1
        m1 = jnp.where(better, logits[e], m1)
        i1 = jnp.where(better, e, i1)
    m2, i2 = jnp.full((tm, 1), -jnp.inf, F32), jnp.zeros((tm, 1), jnp.int32)
    for e in range(n_exp):
        better = jnp.logical_and(i1 != e, logits[e] > m2)
        m2 = jnp.where(better, logits[e], m2)
        i2 = jnp.where(better, e, i2)
    e2 = jnp.exp(m2 - m1)
    den = 1.0 + e2
    lane = lax.broadcasted_iota(jnp.int32, (tm, LANES), 1)
    wt_ref[...] = jnp.where(lane == 0, 1.0 / den, jnp.where(lane == 1, e2 / den, 0.0))

    oh1, oh2 = lane == i1, lane == i2
    hits = jnp.logical_or(oh1, oh2)
    earlier = lax.broadcasted_iota(jnp.int32, (tm, tm), 1) < lax.broadcasted_iota(jnp.int32, (tm, tm), 0)
    prefix = _dot(earlier.astype(BF16), hits.astype(BF16)) + base_ref[...]
    r1 = jnp.sum(jnp.where(oh1, prefix, 0.0), axis=-1, keepdims=True).astype(jnp.int32)
    r2 = jnp.sum(jnp.where(oh2, prefix, 0.0), axis=-1, keepdims=True).astype(jnp.int32)
    idx_ref[...] = jnp.where(lane == 0, i1, jnp.where(lane == 1, i2, jnp.where(lane == 2, r1,
                                                                                jnp.where(lane == 3, r2, 0))))
    base_ref[...] += jnp.sum(hits.astype(F32), axis=0, keepdims=True)
    cnt_ref[...] = jnp.broadcast_to(base_ref[...], cnt_ref.shape).astype(jnp.int32)


def _router(x, g, wrt, *, tm):
    n, d = x.shape
    assert wrt.shape[0] <= LANES
    tile = lambda w: pl.BlockSpec((tm, w), lambda i: (i, 0))
    return pl.pallas_call(
        _router_body,
        grid=(n // tm,),
        in_specs=[tile(d), _resident(g.shape), _resident(wrt.shape)],
        out_specs=[tile(d), tile(LANES), tile(LANES), pl.BlockSpec((SUBLANES, LANES), lambda i: (0, 0))],
        out_shape=[jax.ShapeDtypeStruct((n, d), F32), jax.ShapeDtypeStruct((n, LANES), jnp.int32),
                   jax.ShapeDtypeStruct((n, LANES), F32), jax.ShapeDtypeStruct((SUBLANES, LANES), jnp.int32)],
        scratch_shapes=[pltpu.VMEM((1, LANES), F32)],
        compiler_params=_params("arbitrary"),
        name="router",
    )(x, g, wrt)


def _to_row_tiles(v, dst_ref):
    c = v.shape[1] // LANES
    for j in range(c):
        dst_ref[pl.ds(j, v.shape[0], stride=c), :] = v[:, j * LANES:(j + 1) * LANES]


def _from_row_tiles(src_ref, rows, c):
    return [src_ref[pl.ds(j, rows, stride=c), :] for j in range(c)]


def _tile_copy(src, src_row8, dst, dst_row8, sem):
    return pltpu.make_async_copy(src.at[pl.ds(pl.multiple_of(src_row8, SUBLANES), SUBLANES)],
                                 dst.at[pl.ds(pl.multiple_of(dst_row8, SUBLANES), SUBLANES)], sem)


def _tiles_wait(hbm, vmem, sem):
    pltpu.make_async_copy(hbm.at[pl.ds(0, vmem.shape[0])], vmem, sem).wait()


def _dispatch_body(ends_ref, cnt_ref, pos_ref, h_ref, xs_hbm, zbuf, stage, zsem, sems, *, tmb):
    i, n = pl.program_id(0), pl.num_programs(0)
    tm = h_ref.shape[0]
    slot = lax.rem(i, 2)

    @pl.when(i == 0)
    def _():
        n_exp = ends_ref.shape[0]
        zbuf[...] = jnp.zeros(zbuf.shape, zbuf.dtype)
        for e in range(2 * n_exp):
            first = ends_ref[e] - tmb if e < n_exp else ends_ref[n_exp - 1] + (e - n_exp) * tmb
            live = cnt_ref[e] > 0 if e < n_exp else first * SUBLANES < xs_hbm.shape[0]

            @pl.when(live)
            def _():
                start = pl.multiple_of(first * SUBLANES, tmb * SUBLANES)
                fill = pltpu.make_async_copy(zbuf, xs_hbm.at[pl.ds(start, tmb * SUBLANES)], zsem)
                fill.start()
                fill.wait()

    _to_row_tiles(h_ref[...], stage.at[slot])

    def scatter(g, c):
        for j in range(SUBLANES):
            for k in range(TOP_K):
                _tile_copy(stage.at[slot], (g * SUBLANES + j) * SUBLANES,
                           xs_hbm, pos_ref[0, 0, (g * SUBLANES + j) * TOP_K + k], sems.at[slot]).start()
        return c
    lax.fori_loop(0, tm // SUBLANES, scatter, 0)

    def drain(s):
        for _ in range(TOP_K):
            _tiles_wait(xs_hbm, stage.at[s], sems.at[s])

    @pl.when(i > 0)
    def _():
        drain(1 - slot)

    @pl.when(i == n - 1)
    def _():
        drain(slot)


def _dispatch(ends, counts, pos8, h, *, n_rows, tm, tmb):
    n, d = h.shape
    assert d == SUBLANES * LANES and tm % SUBLANES == 0
    return pl.pallas_call(
        functools.partial(_dispatch_body, tmb=tmb),
        grid_spec=pltpu.PrefetchScalarGridSpec(
            num_scalar_prefetch=2,
            grid=(n // tm,),
            in_specs=[pl.BlockSpec((1, 1, TOP_K * tm), lambda i, *_: (i, 0, 0), memory_space=pltpu.SMEM),
                      pl.BlockSpec((tm, d), lambda i, *_: (i, 0))],
            out_specs=pl.BlockSpec(memory_space=pl.ANY),
            scratch_shapes=[pltpu.VMEM((tmb * SUBLANES, LANES), F32), pltpu.VMEM((2, tm * SUBLANES, LANES), F32),
                            pltpu.SemaphoreType.DMA(()), pltpu.SemaphoreType.DMA((2,))],
        ),
        out_shape=jax.ShapeDtypeStruct((n_rows * SUBLANES, LANES), F32),
        compiler_params=_params("arbitrary"),
        name="dispatch",
    )(ends, counts, pos8.reshape(n // tm, 1, TOP_K * tm), h)


def _expert_body(te_ref, nv_ref, xs_ref, wg_ref, wu_ref, wd_ref, y_ref, hb, acc):
    i, f, nf = pl.program_id(0), pl.program_id(1), pl.num_programs(1)
    tmb, d = hb.shape
    valid = i < nv_ref[0]

    @pl.when(jnp.logical_and(valid, f == 0))
    def _():
        for j, col in enumerate(_from_row_tiles(xs_ref, tmb, d // LANES)):
            hb[:, j * LANES:(j + 1) * LANES] = col.astype(BF16)

    @pl.when(jnp.logical_and(jnp.logical_not(valid), f == 0))
    def _():
        y_ref[...] = jnp.zeros(y_ref.shape, F32)

    @pl.when(valid)
    def _():
        contrib = _swiglu(hb[...], wg_ref[...], wu_ref[...], wd_ref[...])

        @pl.when(f == 0)
        def _():
            acc[...] = contrib

        @pl.when(jnp.logical_and(f != 0, f != nf - 1))
        def _():
            acc[...] += contrib

        @pl.when(f == nf - 1)
        def _():
            _to_row_tiles(acc[...] + contrib, y_ref)


def _experts(tile_expert, n_valid, xs, wg, wu, wd, *, tmb, fc):
    n_tiles = tile_expert.shape[0]
    _, d, ff = wg.shape
    c = d // LANES
    assert ff % fc == 0 and ff // fc >= 2 and xs.shape == (n_tiles * tmb * c, LANES)
    return pl.pallas_call(
        _expert_body,
        grid_spec=pltpu.PrefetchScalarGridSpec(
            num_scalar_prefetch=2,
            grid=(n_tiles, ff // fc),
            in_specs=[pl.BlockSpec((tmb * c, LANES), lambda i, f, te, nv: (jnp.minimum(i, nv[0] - 1), 0)),
                      pl.BlockSpec((None, d, fc), lambda i, f, te, nv: (te[i], 0, f)),
                      pl.BlockSpec((None, d, fc), lambda i, f, te, nv: (te[i], 0, f)),
                      pl.BlockSpec((None, fc, d), lambda i, f, te, nv: (te[i], f, 0))],
            out_specs=pl.BlockSpec((tmb * c, LANES), lambda i, f, te, nv: (i, 0)),
            scratch_shapes=[pltpu.VMEM((tmb, d), BF16), pltpu.VMEM((tmb, d), F32)],
        ),
        out_shape=jax.ShapeDtypeStruct((n_tiles * tmb * c, LANES), F32),
        compiler_params=_params("arbitrary", "arbitrary"),
        name="experts",
    )(tile_expert, n_valid, xs, wg, wu, wd)


def _combine_body(pos_ref, pos_next_ref, x_ref, wt_ref, g_ref, y_hbm, o_ref, yrows, sems, *, final_norm):
    i, n = pl.program_id(0), pl.num_programs(0)
    tm, d = x_ref.shape
    slot = lax.rem(i, 2)

    def gather(p_ref, s):
        def body(g, c):
            for j in range(SUBLANES):
                for k in range(TOP_K):
                    _tile_copy(y_hbm, p_ref[0, 0, (g * SUBLANES + j) * TOP_K + k],
                               yrows.at[s, k], (g * SUBLANES + j) * SUBLANES, sems.at[s]).start()
            return c
        lax.fori_loop(0, tm // SUBLANES, body, 0)

    @pl.when(i == 0)
    def _():
        gather(pos_ref, 0)

    @pl.when(i + 1 < n)
    def _():
        gather(pos_next_ref, 1 - slot)

    for k in range(TOP_K):
        _tiles_wait(y_hbm, yrows.at[slot, k], sems.at[slot])
    ys = [_from_row_tiles(yrows.at[slot, k], tm, d // LANES) for k in range(TOP_K)]
    cols = []
    for j in range(d // LANES):
        moe = wt_ref[:, 0:1] * ys[0][j]
        for k in range(1, TOP_K):
            moe = moe + wt_ref[:, k:k + 1] * ys[k][j]
        cols.append(moe)
    out = x_ref[...] + jnp.concatenate(cols, axis=1)
    o_ref[...] = _rms(out, g_ref[...]) if final_norm else out


def _combine(pos8, x, wt, g, y, *, tm, final_norm):
    n, d = x.shape
    assert d == SUBLANES * LANES and tm % SUBLANES == 0
    last = n // tm - 1
    tile = pl.BlockSpec((tm, d), lambda i: (i, 0))
    pos3 = pos8.reshape(n // tm, 1, TOP_K * tm)
    return pl.pallas_call(
        functools.partial(_combine_body, final_norm=final_norm),
        grid=(n // tm,),
        in_specs=[pl.BlockSpec((1, 1, TOP_K * tm), lambda i: (i, 0, 0), memory_space=pltpu.SMEM),
                  pl.BlockSpec((1, 1, TOP_K * tm), lambda i: (jnp.minimum(i + 1, last), 0, 0),
                               memory_space=pltpu.SMEM),
                  tile, pl.BlockSpec((tm, LANES), lambda i: (i, 0)), _resident(g.shape),
                  pl.BlockSpec(memory_space=pl.ANY)],
        out_specs=tile,
        out_shape=jax.ShapeDtypeStruct((n, d), F32),
        scratch_shapes=[pltpu.VMEM((2, TOP_K, tm * SUBLANES, LANES), F32), pltpu.SemaphoreType.DMA((2,))],
        compiler_params=_params("arbitrary"),
        name="combine",
    )(pos3, pos3, x, wt, g, y)


def _routing_tables(idx, cnt, n_exp, tmb):
    n = idx.shape[0]
    experts = jnp.arange(n_exp, dtype=jnp.int32)
    counts = cnt[0, :n_exp]
    padded = ((counts + (tmb - 1)) // tmb) * tmb
    ends = jnp.sum(jnp.where(experts[None, :] <= experts[:, None], padded[None, :], 0), axis=1)
    starts = ends - padded
    first = jnp.sum(jnp.where(idx[:, 0:TOP_K, None] == experts, starts, 0), axis=-1)
    pos8 = (first + idx[:, TOP_K:2 * TOP_K]) * SUBLANES
    n_tiles = (n * TOP_K) // tmb + n_exp
    tile_start = jnp.arange(n_tiles, dtype=jnp.int32) * tmb
    tile_expert = jnp.minimum(jnp.sum((tile_start[:, None] >= ends[None, :]).astype(jnp.int32), axis=1), n_exp - 1)
    return pos8, ends, counts, tile_expert, ends[n_exp - 1:] // tmb, n_tiles


def _moe(x, g, wrt, wg, wu, wd, g_out, *, final_norm):
    h, idx, wt, cnt = _router(x, g, wrt, tm=ROW_TILE)
    pos8, ends, counts, tile_expert, n_valid, n_tiles = _routing_tables(idx, cnt, wrt.shape[0], EXPERT_ROW_TILE)
    xs = _dispatch(ends, counts, pos8, h, n_rows=n_tiles * EXPERT_ROW_TILE, tm=ROW_TILE, tmb=EXPERT_ROW_TILE)
    y = _experts(tile_expert, n_valid, xs, wg, wu, wd, tmb=EXPERT_ROW_TILE, fc=EXPERT_FF_TILE)
    return _combine(pos8, x, wt, g_out, y, tm=COMBINE_ROW_TILE, final_norm=final_norm)


def _final_norm_body(x_ref, g_ref, o_ref):
    o_ref[...] = _rms(x_ref[...], g_ref[...])


def _final_norm(x, g, *, tm):
    n, d = x.shape
    tile = pl.BlockSpec((tm, d), lambda i: (i, 0))
    return pl.pallas_call(
        _final_norm_body, grid=(n // tm,), in_specs=[tile, _resident(g.shape)], out_specs=tile,
        out_shape=jax.ShapeDtypeStruct((n, d), F32), compiler_params=_params("arbitrary"), name="final_norm",
    )(x, g)


def kernel(x, mem, g_mix, w_in, w_pool_group, pool_scale, w_pool_out, conv_w, w_conv_out, w_mix_out, g_xattn, g_mem, w_xq, w_xk, w_xv, w_xo, g_ffn, w_ff_gate, w_ff_up, w_ff_down, w_router, w_e_gate, w_e_up, w_e_down, g_final):
    b, s, d = x.shape
    depth = g_mix.shape[0]
    bf = lambda w: w.astype(BF16)
    row = lambda v: v.reshape(1, -1)

    kt, v = _memkv(mem, g_mem.reshape(depth, 1, d), bf(w_xk), bf(w_xv))
    xf = x.reshape(b * s, d)
    for l in range(depth):
        xf = _mixer(xf, row(g_mix[l]), bf(w_in[l]), bf(w_pool_group[l]), row(pool_scale[l]), bf(w_pool_out[l]),
                    conv_w[l], bf(w_conv_out[l]), bf(w_mix_out[l]), seq=s, tm=ROW_TILE)
        xf = _xattn(xf, row(g_xattn[l]), bf(w_xq[l]), kt, v, bf(w_xo[l]), layer=l, seq=s, tm=ROW_TILE)
        last = l == depth - 1
        i = l // 2
        if l % 2 == 0:
            xf = _ffn(xf, row(g_ffn[l]), bf(w_ff_gate[i]), bf(w_ff_up[i]), bf(w_ff_down[i]), tm=ROW_TILE)
            if last:
                xf = _final_norm(xf, row(g_final), tm=ROW_TILE)
        else:
            xf = _moe(xf, row(g_ffn[l]), w_router[i].T, bf(w_e_gate[i]), bf(w_e_up[i]), bf(w_e_down[i]),
                      row(g_final), final_norm=last)
    return xf.reshape(b, s, d)
```

```python
import functools

import jax
import jax.numpy as jnp
from jax import lax
from jax.experimental import pallas as pl
from jax.experimental.pallas import tpu as pltpu

F32 = jnp.float32
BF16 = jnp.bfloat16

EPS = 1e-6
POOL_WINDOWS = (2, 4, 8, 16)
CONV_K = 3
N_XHEADS = 4
TOP_K = 2

LANES = 128
SUBLANES = 8
VMEM_LIMIT_BYTES = 56 * 1024 * 1024
N_DMA_QUEUES = 2

POOL_HALO = 16
CONV_HALO = 8

ROW_TILE = 512
EXPERT_ROW_TILE = 1024
EXPERT_FF_TILE = 896
COMBINE_ROW_TILE = 256


def _dot(a, b):
    return jnp.dot(a, b, preferred_element_type=F32)


def _rms(xf, g):
    ms = jnp.mean(xf * xf, axis=-1, keepdims=True)
    return (xf * lax.rsqrt(ms + EPS)) * g


def _resident(shape):
    nd = len(shape)
    return pl.BlockSpec(shape, lambda *_: (0,) * nd, pipeline_mode=pl.Buffered(1))


def _params(*sem):
    return pltpu.CompilerParams(dimension_semantics=sem, vmem_limit_bytes=VMEM_LIMIT_BYTES)


def _mixer_body(x_ref, g_ref, win_ref, wgrp_ref, ps_ref, wpo_ref, cw_ref, wco_ref, wmo_ref, o_ref,
                pool_buf, conv_buf, *, tm, seq):
    n_groups, _, gw = wgrp_ref.shape
    pw = n_groups * gw
    cwid = cw_ref.shape[1]
    d = x_ref.shape[1]
    o_c, o_b, o_u, o_g = pw, pw + cwid, pw + 2 * cwid, pw + 3 * cwid

    pos0 = lax.rem(pl.program_id(0) * tm, seq)

    @pl.when(pos0 == 0)
    def _():
        pool_buf[:, 0:POOL_HALO, :] = jnp.zeros((n_groups, POOL_HALO, gw), F32)
        conv_buf[:, 0:CONV_HALO, :] = jnp.zeros((conv_buf.shape[0], CONV_HALO, LANES), F32)

    @pl.when(pos0 != 0)
    def _():
        pool_buf[:, 0:POOL_HALO, :] = pool_buf[:, tm:tm + POOL_HALO, :]
        conv_buf[:, 0:CONV_HALO, :] = conv_buf[:, tm:tm + CONV_HALO, :]

    x = x_ref[...]
    h = _rms(x, g_ref[...]).astype(BF16)

    u_pool = _dot(h, win_ref[:, 0:o_c])
    pos1 = lax.broadcasted_iota(jnp.int32, (tm, gw), 0) + (pos0 + 1)
    ys = []
    for g, w in enumerate(POOL_WINDOWS):
        ug = u_pool[:, g * gw:(g + 1) * gw]
        pool_buf[g, POOL_HALO:POOL_HALO + tm, :] = ug
        acc = ug
        for k in range(1, w):
            acc = acc + pool_buf[g, POOL_HALO - k:POOL_HALO - k + tm, :]
        mean = acc / jnp.minimum(pos1, w).astype(F32)
        ys.append(_dot((mean - ug).astype(BF16), wgrp_ref[g]))
    y = jnp.concatenate(ys, axis=1) * ps_ref[...]
    y_pool = _dot(y.astype(BF16), wpo_ref[...])

    c_gate = _dot(h, win_ref[:, o_c:o_b])
    u_conv = _dot(h, win_ref[:, o_u:o_g])
    zc = c_gate * u_conv
    ycs = []
    for j in range(cwid // LANES):
        sl = slice(j * LANES, (j + 1) * LANES)
        zj = zc[:, sl]
        conv_buf[j, CONV_HALO:CONV_HALO + tm, :] = zj
        acc = cw_ref[0:1, sl] * conv_buf[j, CONV_HALO - (CONV_K - 1):CONV_HALO - (CONV_K - 1) + tm, :]
        for k in range(1, CONV_K - 1):
            off = CONV_HALO - (CONV_K - 1) + k
            acc = acc + cw_ref[k:k + 1, sl] * conv_buf[j, off:off + tm, :]
        ycs.append(acc + cw_ref[CONV_K - 1:CONV_K, sl] * zj)
    b_gate = _dot(h, win_ref[:, o_b:o_u])
    y_conv = _dot((b_gate * jnp.concatenate(ycs, axis=1)).astype(BF16), wco_ref[...])

    merged = jax.nn.sigmoid(_dot(h, win_ref[:, o_g:o_g + d])) * y_pool
    merged = merged + jax.nn.sigmoid(_dot(h, win_ref[:, o_g + d:o_g + 2 * d])) * y_conv
    o_ref[...] = x + _dot(merged.astype(BF16), wmo_ref[...])


def _mixer(x, g, win, wgrp, ps, wpo, cw, wco, wmo, *, seq, tm):
    n, d = x.shape
    n_groups, _, gw = wgrp.shape
    assert seq % tm == 0 and n % tm == 0 and gw == LANES and cw.shape[1] % LANES == 0
    assert len(POOL_WINDOWS) == n_groups and max(POOL_WINDOWS) - 1 <= POOL_HALO and cw.shape[0] == CONV_K
    tile = pl.BlockSpec((tm, d), lambda i: (i, 0))
    return pl.pallas_call(
        functools.partial(_mixer_body, tm=tm, seq=seq),
        grid=(n // tm,),
        in_specs=[tile, _resident(g.shape), _resident(win.shape), _resident(wgrp.shape), _resident(ps.shape),
                  _resident(wpo.shape), _resident(cw.shape), _resident(wco.shape), _resident(wmo.shape)],
        out_specs=tile,
        out_shape=jax.ShapeDtypeStruct((n, d), F32),
        scratch_shapes=[pltpu.VMEM((n_groups, tm + POOL_HALO, gw), F32),
                        pltpu.VMEM((cw.shape[1] // LANES, tm + CONV_HALO, LANES), F32)],
        compiler_params=_params("arbitrary"),
        name="mixer",
    )(x, g, win, wgrp, ps, wpo, cw, wco, wmo)


def _memkv_body(mem_ref, g_ref, wk_ref, wv_ref, kt_ref, v_ref):
    mn = _rms(mem_ref[...], g_ref[...]).astype(BF16)
    kt_ref[...] = _dot(mn, wk_ref[...]).T.astype(BF16)
    v_ref[...] = _dot(mn, wv_ref[...]).astype(BF16)


def _memkv(mem, g_mem, wk, wv):
    b, m, d = mem.shape
    depth = wk.shape[0]
    per_layer = lambda l, i: (l, 0, 0)
    return pl.pallas_call(
        _memkv_body,
        grid=(depth, b),
        in_specs=[pl.BlockSpec((None, m, d), lambda l, i: (i, 0, 0)),
                  pl.BlockSpec((None, 1, d), per_layer),
                  pl.BlockSpec((None, d, d), per_layer),
                  pl.BlockSpec((None, d, d), per_layer)],
        out_specs=[pl.BlockSpec((None, None, d, m), lambda l, i: (l, i, 0, 0)),
                   pl.BlockSpec((None, None, m, d), lambda l, i: (l, i, 0, 0))],
        out_shape=[jax.ShapeDtypeStruct((depth, b, d, m), BF16), jax.ShapeDtypeStruct((depth, b, m, d), BF16)],
        compiler_params=_params("arbitrary", "arbitrary"),
        name="memkv",
    )(mem, g_mem, wk, wv)


def _xattn_body(x_ref, g_ref, wq_ref, kt_ref, v_ref, wo_ref, o_ref):
    x = x_ref[...]
    h = _rms(x, g_ref[...]).astype(BF16)
    q = _dot(h, wq_ref[...])
    dh = q.shape[1] // N_XHEADS
    heads = []
    for hd in range(N_XHEADS):
        sl = slice(hd * dh, (hd + 1) * dh)
        s = _dot(q[:, sl].astype(BF16), kt_ref[sl, :]) * (dh ** -0.5)
        e = jnp.exp(s - jnp.max(s, axis=-1, keepdims=True))
        p = e / jnp.sum(e, axis=-1, keepdims=True)
        heads.append(_dot(p.astype(BF16), v_ref[:, sl]))
    o = jnp.concatenate(heads, axis=1).astype(BF16)
    o_ref[...] = x + _dot(o, wo_ref[...])


def _xattn(x, g, wq, kt, v, wo, *, layer, seq, tm):
    n, d = x.shape
    m = v.shape[2]
    assert seq % tm == 0 and d % N_XHEADS == 0
    tiles_per_seq = seq // tm
    tile = pl.BlockSpec((tm, d), lambda i: (i, 0))
    return pl.pallas_call(
        _xattn_body,
        grid=(n // tm,),
        in_specs=[tile, _resident(g.shape), _resident(wq.shape),
                  pl.BlockSpec((None, None, d, m), lambda i: (layer, i // tiles_per_seq, 0, 0)),
                  pl.BlockSpec((None, None, m, d), lambda i: (layer, i // tiles_per_seq, 0, 0)),
                  _resident(wo.shape)],
        out_specs=tile,
        out_shape=jax.ShapeDtypeStruct((n, d), F32),
        compiler_params=_params("arbitrary"),
        name="xattn",
    )(x, g, wq, kt, v, wo)


def _swiglu(h, wg, wu, wd):
    g = _dot(h, wg)
    return _dot((g * jax.nn.sigmoid(g) * _dot(h, wu)).astype(BF16), wd)


def _ffn_body(x_ref, g_ref, wg_ref, wu_ref, wd_ref, o_ref):
    x = x_ref[...]
    h = _rms(x, g_ref[...]).astype(BF16)
    o_ref[...] = x + _swiglu(h, wg_ref[...], wu_ref[...], wd_ref[...])


def _ffn(x, g, wg, wu, wd, *, tm):
    n, d = x.shape
    tile = pl.BlockSpec((tm, d), lambda i: (i, 0))
    return pl.pallas_call(
        _ffn_body,
        grid=(n // tm,),
        in_specs=[tile, _resident(g.shape), _resident(wg.shape), _resident(wu.shape), _resident(wd.shape)],
        out_specs=tile,
        out_shape=jax.ShapeDtypeStruct((n, d), F32),
        compiler_params=_params("arbitrary"),
        name="ffn",
    )(x, g, wg, wu, wd)


def _router_body(x_ref, g_ref, wrt_ref, h_ref, idx_ref, wt_ref, cnt_ref, base_ref):
    tm = x_ref.shape[0]
    n_exp = wrt_ref.shape[0]

    @pl.when(pl.program_id(0) == 0)
    def _():
        base_ref[...] = jnp.zeros(base_ref.shape, F32)

    h = _rms(x_ref[...], g_ref[...])
    h_ref[...] = h
    logits = [jnp.sum(h * wrt_ref[e:e + 1, :], axis=-1, keepdims=True) for e in range(n_exp)]
    m1, i1 = logits[0], jnp.zeros((tm, 1), jnp.int32)
    for e in range(1, n_exp):
        better = logits[e] > m1
        m1 = jnp.where(better, logits[e], m1)
        i1 = jnp.where(better, e, i1)
    m2, i2 = jnp.full((tm, 1), -jnp.inf, F32), jnp.zeros((tm, 1), jnp.int32)
    for e in range(n_exp):
        better = jnp.logical_and(i1 != e, logits[e] > m2)
        m2 = jnp.where(better, logits[e], m2)
        i2 = jnp.where(better, e, i2)
    e2 = jnp.exp(m2 - m1)
    den = 1.0 + e2
    lane = lax.broadcasted_iota(jnp.int32, (tm, LANES), 1)
    wt_ref[...] = jnp.where(lane == 0, 1.0 / den, jnp.where(lane == 1, e2 / den, 0.0))

    oh1, oh2 = lane == i1, lane == i2
    hits = jnp.logical_or(oh1, oh2)
    earlier = lax.broadcasted_iota(jnp.int32, (tm, tm), 1) < lax.broadcasted_iota(jnp.int32, (tm, tm), 0)
    prefix = _dot(earlier.astype(BF16), hits.astype(BF16)) + base_ref[...]
    r1 = jnp.sum(jnp.where(oh1, prefix, 0.0), axis=-1, keepdims=True).astype(jnp.int32)
    r2 = jnp.sum(jnp.where(oh2, prefix, 0.0), axis=-1, keepdims=True).astype(jnp.int32)
    idx_ref[...] = jnp.where(lane == 0, i1, jnp.where(lane == 1, i2, jnp.where(lane == 2, r1,
                                                                                jnp.where(lane == 3, r2, 0))))
    base_ref[...] += jnp.sum(hits.astype(F32), axis=0, keepdims=True)
    cnt_ref[...] = jnp.broadcast_to(base_ref[...], cnt_ref.shape).astype(jnp.int32)


def _router(x, g, wrt, *, tm):
    n, d = x.shape
    assert wrt.shape[0] <= LANES
    tile = lambda w: pl.BlockSpec((tm, w), lambda i: (i, 0))
    return pl.pallas_call(
        _router_body,
        grid=(n // tm,),
        in_specs=[tile(d), _resident(g.shape), _resident(wrt.shape)],
        out_specs=[tile(d), tile(LANES), tile(LANES), pl.BlockSpec((SUBLANES, LANES), lambda i: (0, 0))],
        out_shape=[jax.ShapeDtypeStruct((n, d), F32), jax.ShapeDtypeStruct((n, LANES), jnp.int32),
                   jax.ShapeDtypeStruct((n, LANES), F32), jax.ShapeDtypeStruct((SUBLANES, LANES), jnp.int32)],
        scratch_shapes=[pltpu.VMEM((1, LANES), F32)],
        compiler_params=_params("arbitrary"),
        name="router",
    )(x, g, wrt)


def _to_row_tiles(v, dst_ref):
    c = v.shape[1] // LANES
    for j in range(c):
        dst_ref[pl.ds(j, v.shape[0], stride=c), :] = v[:, j * LANES:(j + 1) * LANES]


def _from_row_tiles(src_ref, rows, c):
    return [src_ref[pl.ds(j, rows, stride=c), :] for j in range(c)]


def _tile_copy(src, src_row8, dst, dst_row8, sem):
    return pltpu.make_async_copy(src.at[pl.ds(pl.multiple_of(src_row8, SUBLANES), SUBLANES)],
                                 dst.at[pl.ds(pl.multiple_of(dst_row8, SUBLANES), SUBLANES)], sem)


def _tiles_wait(hbm, vmem, sem):
    pltpu.make_async_copy(hbm.at[pl.ds(0, vmem.shape[0])], vmem, sem).wait()


def _dispatch_body(ends_ref, cnt_ref, pos_ref, h_ref, xs_hbm, zbuf, stage, zsem, sems, *, tmb):
    i, n = pl.program_id(0), pl.num_programs(0)
    tm = h_ref.shape[0]
    slot = lax.rem(i, 2)

    @pl.when(i == 0)
    def _():
        n_exp = ends_ref.shape[0]
        zbuf[...] = jnp.zeros(zbuf.shape, zbuf.dtype)
        for e in range(2 * n_exp):
            first = ends_ref[e] - tmb if e < n_exp else ends_ref[n_exp - 1] + (e - n_exp) * tmb
            live = cnt_ref[e] > 0 if e < n_exp else first * SUBLANES < xs_hbm.shape[0]

            @pl.when(live)
            def _():
                start = pl.multiple_of(first * SUBLANES, tmb * SUBLANES)
                fill = pltpu.make_async_copy(zbuf, xs_hbm.at[pl.ds(start, tmb * SUBLANES)], zsem)
                fill.start()
                fill.wait()

    _to_row_tiles(h_ref[...], stage.at[slot])

    def scatter(g, c):
        for j in range(SUBLANES):
            for k in range(TOP_K):
                _tile_copy(stage.at[slot], (g * SUBLANES + j) * SUBLANES,
                           xs_hbm, pos_ref[0, 0, (g * SUBLANES + j) * TOP_K + k],
                           sems.at[slot]).start(priority=(j * TOP_K + k) % N_DMA_QUEUES)
        return c
    lax.fori_loop(0, tm // SUBLANES, scatter, 0)

    def drain(s):
        for _ in range(TOP_K):
            _tiles_wait(xs_hbm, stage.at[s], sems.at[s])

    @pl.when(i > 0)
    def _():
        drain(1 - slot)

    @pl.when(i == n - 1)
    def _():
        drain(slot)


def _dispatch(ends, counts, pos8, h, *, n_rows, tm, tmb):
    n, d = h.shape
    assert d == SUBLANES * LANES and tm % SUBLANES == 0
    return pl.pallas_call(
        functools.partial(_dispatch_body, tmb=tmb),
        grid_spec=pltpu.PrefetchScalarGridSpec(
            num_scalar_prefetch=2,
            grid=(n // tm,),
            in_specs=[pl.BlockSpec((1, 1, TOP_K * tm), lambda i, *_: (i, 0, 0), memory_space=pltpu.SMEM),
                      pl.BlockSpec((tm, d), lambda i, *_: (i, 0))],
            out_specs=pl.BlockSpec(memory_space=pl.ANY),
            scratch_shapes=[pltpu.VMEM((tmb * SUBLANES, LANES), F32), pltpu.VMEM((2, tm * SUBLANES, LANES), F32),
                            pltpu.SemaphoreType.DMA(()), pltpu.SemaphoreType.DMA((2,))],
        ),
        out_shape=jax.ShapeDtypeStruct((n_rows * SUBLANES, LANES), F32),
        compiler_params=_params("arbitrary"),
        name="dispatch",
    )(ends, counts, pos8.reshape(n // tm, 1, TOP_K * tm), h)


def _expert_body(te_ref, nv_ref, xs_ref, wg_ref, wu_ref, wd_ref, y_ref, hb, acc):
    i, f = pl.program_id(0), pl.program_id(1)
    tmb, d = hb.shape
    valid = i < nv_ref[0]

    @pl.when(jnp.logical_and(i == 0, f == 0))
    def _():
        acc[...] = jnp.zeros(acc.shape, F32)

    @pl.when(jnp.logical_and(valid, f == 0))
    def _():
        for j, col in enumerate(_from_row_tiles(xs_ref, tmb, d // LANES)):
            hb[:, j * LANES:(j + 1) * LANES] = col.astype(BF16)

    @pl.when(jnp.logical_and(jnp.logical_not(valid), f == 0))
    def _():
        y_ref[...] = jnp.zeros(y_ref.shape, F32)

    @pl.when(valid)
    def _():
        contrib = _swiglu(hb[...], wg_ref[...], wu_ref[...], wd_ref[...])
        total = jnp.where(f == 0, contrib, acc[...] + contrib)
        acc[...] = total
        _to_row_tiles(total, y_ref)


def _experts(tile_expert, n_valid, xs, wg, wu, wd, *, tmb, fc):
    n_tiles = tile_expert.shape[0]
    _, d, ff = wg.shape
    c = d // LANES
    assert ff % fc == 0 and xs.shape == (n_tiles * tmb * c, LANES)
    return pl.pallas_call(
        _expert_body,
        grid_spec=pltpu.PrefetchScalarGridSpec(
            num_scalar_prefetch=2,
            grid=(n_tiles, ff // fc),
            in_specs=[pl.BlockSpec((tmb * c, LANES), lambda i, f, te, nv: (jnp.minimum(i, nv[0] - 1), 0)),
                      pl.BlockSpec((None, d, fc), lambda i, f, te, nv: (te[i], 0, f)),
                      pl.BlockSpec((None, d, fc), lambda i, f, te, nv: (te[i], 0, f)),
                      pl.BlockSpec((None, fc, d), lambda i, f, te, nv: (te[i], f, 0))],
            out_specs=pl.BlockSpec((tmb * c, LANES), lambda i, f, te, nv: (i, 0)),
            scratch_shapes=[pltpu.VMEM((tmb, d), BF16), pltpu.VMEM((tmb, d), F32)],
        ),
        out_shape=jax.ShapeDtypeStruct((n_tiles * tmb * c, LANES), F32),
        compiler_params=_params("arbitrary", "arbitrary"),
        name="experts",
    )(tile_expert, n_valid, xs, wg, wu, wd)


def _combine_body(pos_ref, pos_next_ref, x_ref, wt_ref, g_ref, y_hbm, o_ref, yrows, sems, *, final_norm):
    i, n = pl.program_id(0), pl.num_programs(0)
    tm, d = x_ref.shape
    slot = lax.rem(i, 2)

    def gather(p_ref, s):
        def body(g, c):
            for j in range(SUBLANES):
                for k in range(TOP_K):
                    _tile_copy(y_hbm, p_ref[0, 0, (g * SUBLANES + j) * TOP_K + k],
                               yrows.at[s, k], (g * SUBLANES + j) * SUBLANES,
                               sems.at[s]).start(priority=(j * TOP_K + k) % N_DMA_QUEUES)
            return c
        lax.fori_loop(0, tm // SUBLANES, body, 0)

    @pl.when(i == 0)
    def _():
        gather(pos_ref, 0)

    @pl.when(i + 1 < n)
    def _():
        gather(pos_next_ref, 1 - slot)

    for k in range(TOP_K):
        _tiles_wait(y_hbm, yrows.at[slot, k], sems.at[slot])
    ys = [_from_row_tiles(yrows.at[slot, k], tm, d // LANES) for k in range(TOP_K)]
    cols = []
    for j in range(d // LANES):
        moe = wt_ref[:, 0:1] * ys[0][j]
        for k in range(1, TOP_K):
            moe = moe + wt_ref[:, k:k + 1] * ys[k][j]
        cols.append(moe)
    out = x_ref[...] + jnp.concatenate(cols, axis=1)
    o_ref[...] = _rms(out, g_ref[...]) if final_norm else out


def _combine(pos8, x, wt, g, y, *, tm, final_norm):
    n, d = x.shape
    assert d == SUBLANES * LANES and tm % SUBLANES == 0
    last = n // tm - 1
    tile = pl.BlockSpec((tm, d), lambda i: (i, 0))
    pos3 = pos8.reshape(n // tm, 1, TOP_K * tm)
    return pl.pallas_call(
        functools.partial(_combine_body, final_norm=final_norm),
        grid=(n // tm,),
        in_specs=[pl.BlockSpec((1, 1, TOP_K * tm), lambda i: (i, 0, 0), memory_space=pltpu.SMEM),
                  pl.BlockSpec((1, 1, TOP_K * tm), lambda i: (jnp.minimum(i + 1, last), 0, 0),
                               memory_space=pltpu.SMEM),
                  tile, pl.BlockSpec((tm, LANES), lambda i: (i, 0)), _resident(g.shape),
                  pl.BlockSpec(memory_space=pl.ANY)],
        out_specs=tile,
        out_shape=jax.ShapeDtypeStruct((n, d), F32),
        scratch_shapes=[pltpu.VMEM((2, TOP_K, tm * SUBLANES, LANES), F32), pltpu.SemaphoreType.DMA((2,))],
        compiler_params=_params("arbitrary"),
        name="combine",
    )(pos3, pos3, x, wt, g, y)


def _routing_tables(idx, cnt, n_exp, tmb):
    n = idx.shape[0]
    experts = jnp.arange(n_exp, dtype=jnp.int32)
    counts = cnt[0, :n_exp]
    padded = ((counts + (tmb - 1)) // tmb) * tmb
    ends = jnp.sum(jnp.where(experts[None, :] <= experts[:, None], padded[None, :], 0), axis=1)
    starts = ends - padded
    first = jnp.sum(jnp.where(idx[:, 0:TOP_K, None] == experts, starts, 0), axis=-1)
    pos8 = (first + idx[:, TOP_K:2 * TOP_K]) * SUBLANES
    n_tiles = (n * TOP_K) // tmb + n_exp
    tile_start = jnp.arange(n_tiles, dtype=jnp.int32) * tmb
    tile_expert = jnp.minimum(jnp.sum((tile_start[:, None] >= ends[None, :]).astype(jnp.int32), axis=1), n_exp - 1)
    return pos8, ends, counts, tile_expert, ends[n_exp - 1:] // tmb, n_tiles


def _moe(x, g, wrt, wg, wu, wd, g_out, *, final_norm):
    h, idx, wt, cnt = _router(x, g, wrt, tm=ROW_TILE)
    pos8, ends, counts, tile_expert, n_valid, n_tiles = _routing_tables(idx, cnt, wrt.shape[0], EXPERT_ROW_TILE)
    xs = _dispatch(ends, counts, pos8, h, n_rows=n_tiles * EXPERT_ROW_TILE, tm=ROW_TILE, tmb=EXPERT_ROW_TILE)
    y = _experts(tile_expert, n_valid, xs, wg, wu, wd, tmb=EXPERT_ROW_TILE, fc=EXPERT_FF_TILE)
    return _combine(pos8, x, wt, g_out, y, tm=COMBINE_ROW_TILE, final_norm=final_norm)


def _final_norm_body(x_ref, g_ref, o_ref):
    o_ref[...] = _rms(x_ref[...], g_ref[...])


def _final_norm(x, g, *, tm):
    n, d = x.shape
    tile = pl.BlockSpec((tm, d), lambda i: (i, 0))
    return pl.pallas_call(
        _final_norm_body, grid=(n // tm,), in_specs=[tile, _resident(g.shape)], out_specs=tile,
        out_shape=jax.ShapeDtypeStruct((n, d), F32), compiler_params=_params("arbitrary"), name="final_norm",
    )(x, g)


def kernel(x, mem, g_mix, w_in, w_pool_group, pool_scale, w_pool_out, conv_w, w_conv_out, w_mix_out, g_xattn, g_mem, w_xq, w_xk, w_xv, w_xo, g_ffn, w_ff_gate, w_ff_up, w_ff_down, w_router, w_e_gate, w_e_up, w_e_down, g_final):
    b, s, d = x.shape
    depth = g_mix.shape[0]
    bf = lambda w: w.astype(BF16)
    row = lambda v: v.reshape(1, -1)

    kt, v = _memkv(mem, g_mem.reshape(depth, 1, d), bf(w_xk), bf(w_xv))
    xf = x.reshape(b * s, d)
    for l in range(depth):
        xf = _mixer(xf, row(g_mix[l]), bf(w_in[l]), bf(w_pool_group[l]), row(pool_scale[l]), bf(w_pool_out[l]),
                    conv_w[l], bf(w_conv_out[l]), bf(w_mix_out[l]), seq=s, tm=ROW_TILE)
        xf = _xattn(xf, row(g_xattn[l]), bf(w_xq[l]), kt, v, bf(w_xo[l]), layer=l, seq=s, tm=ROW_TILE)
        last = l == depth - 1
        i = l // 2
        if l % 2 == 0:
            xf = _ffn(xf, row(g_ffn[l]), bf(w_ff_gate[i]), bf(w_ff_up[i]), bf(w_ff_down[i]), tm=ROW_TILE)
            if last:
                xf = _final_norm(xf, row(g_final), tm=ROW_TILE)
        else:
            xf = _moe(xf, row(g_ffn[l]), w_router[i].T, bf(w_e_gate[i]), bf(w_e_up[i]), bf(w_e_down[i]),
                      row(g_final), final_norm=last)
    return xf.reshape(b, s, d)
```

```python
import functools

import jax
import jax.numpy as jnp
from jax import lax
from jax.experimental import pallas as pl
from jax.experimental.pallas import tpu as pltpu

F32 = jnp.float32
BF16 = jnp.bfloat16

EPS = 1e-6
POOL_WINDOWS = (2, 4, 8, 16)
CONV_K = 3
N_XHEADS = 4
TOP_K = 2

LANES = 128
SUBLANES = 8
VMEM_LIMIT_BYTES = 56 * 1024 * 1024
N_DMA_QUEUES = 2

POOL_HALO = 16
CONV_HALO = 8

ROW_TILE = 512
EXPERT_ROW_TILE = 1024
EXPERT_FF_TILE = 512


def _dot(a, b):
    return jnp.dot(a, b, preferred_element_type=F32)


def _rms(xf, g):
    ms = jnp.mean(xf * xf, axis=-1, keepdims=True)
    return (xf * lax.rsqrt(ms + EPS)) * g


def _resident(shape):
    nd = len(shape)
    return pl.BlockSpec(shape, lambda *_: (0,) * nd, pipeline_mode=pl.Buffered(1))


def _params(*sem):
    return pltpu.CompilerParams(dimension_semantics=sem, vmem_limit_bytes=VMEM_LIMIT_BYTES)


def _mixer_body(x_ref, g_ref, win_ref, wgrp_ref, ps_ref, wpo_ref, cw_ref, wco_ref, wmo_ref, o_ref,
                pool_buf, conv_buf, *, tm, seq):
    n_groups, _, gw = wgrp_ref.shape
    pw = n_groups * gw
    cwid = cw_ref.shape[1]
    d = x_ref.shape[1]
    o_c, o_b, o_u, o_g = pw, pw + cwid, pw + 2 * cwid, pw + 3 * cwid

    pos0 = lax.rem(pl.program_id(0) * tm, seq)

    @pl.when(pos0 == 0)
    def _():
        pool_buf[:, 0:POOL_HALO, :] = jnp.zeros((n_groups, POOL_HALO, gw), F32)
        conv_buf[:, 0:CONV_HALO, :] = jnp.zeros((conv_buf.shape[0], CONV_HALO, LANES), F32)

    @pl.when(pos0 != 0)
    def _():
        pool_buf[:, 0:POOL_HALO, :] = pool_buf[:, tm:tm + POOL_HALO, :]
        conv_buf[:, 0:CONV_HALO, :] = conv_buf[:, tm:tm + CONV_HALO, :]

    x = x_ref[...]
    h = _rms(x, g_ref[...]).astype(BF16)

    u_pool = _dot(h, win_ref[:, 0:o_c])
    pos1 = lax.broadcasted_iota(jnp.int32, (tm, gw), 0) + (pos0 + 1)
    ys = []
    for g, w in enumerate(POOL_WINDOWS):
        ug = u_pool[:, g * gw:(g + 1) * gw]
        pool_buf[g, POOL_HALO:POOL_HALO + tm, :] = ug
        acc = ug
        for k in range(1, w):
            acc = acc + pool_buf[g, POOL_HALO - k:POOL_HALO - k + tm, :]
        mean = acc / jnp.minimum(pos1, w).astype(F32)
        ys.append(_dot((mean - ug).astype(BF16), wgrp_ref[g]))
    y = jnp.concatenate(ys, axis=1) * ps_ref[...]
    y_pool = _dot(y.astype(BF16), wpo_ref[...])

    c_gate = _dot(h, win_ref[:, o_c:o_b])
    u_conv = _dot(h, win_ref[:, o_u:o_g])
    zc = c_gate * u_conv
    ycs = []
    for j in range(cwid // LANES):
        sl = slice(j * LANES, (j + 1) * LANES)
        zj = zc[:, sl]
        conv_buf[j, CONV_HALO:CONV_HALO + tm, :] = zj
        acc = cw_ref[0:1, sl] * conv_buf[j, CONV_HALO - (CONV_K - 1):CONV_HALO - (CONV_K - 1) + tm, :]
        for k in range(1, CONV_K - 1):
            off = CONV_HALO - (CONV_K - 1) + k
            acc = acc + cw_ref[k:k + 1, sl] * conv_buf[j, off:off + tm, :]
        ycs.append(acc + cw_ref[CONV_K - 1:CONV_K, sl] * zj)
    b_gate = _dot(h, win_ref[:, o_b:o_u])
    y_conv = _dot((b_gate * jnp.concatenate(ycs, axis=1)).astype(BF16), wco_ref[...])

    merged = jax.nn.sigmoid(_dot(h, win_ref[:, o_g:o_g + d])) * y_pool
    merged = merged + jax.nn.sigmoid(_dot(h, win_ref[:, o_g + d:o_g + 2 * d])) * y_conv
    o_ref[...] = x + _dot(merged.astype(BF16), wmo_ref[...])


def _mixer(x, g, win, wgrp, ps, wpo, cw, wco, wmo, *, seq, tm):
    n, d = x.shape
    n_groups, _, gw = wgrp.shape
    assert seq % tm == 0 and n % tm == 0 and gw == LANES and cw.shape[1] % LANES == 0
    assert len(POOL_WINDOWS) == n_groups and max(POOL_WINDOWS) - 1 <= POOL_HALO and cw.shape[0] == CONV_K
    tile = pl.BlockSpec((tm, d), lambda i: (i, 0))
    return pl.pallas_call(
        functools.partial(_mixer_body, tm=tm, seq=seq),
        grid=(n // tm,),
        in_specs=[tile, _resident(g.shape), _resident(win.shape), _resident(wgrp.shape), _resident(ps.shape),
                  _resident(wpo.shape), _resident(cw.shape), _resident(wco.shape), _resident(wmo.shape)],
        out_specs=tile,
        out_shape=jax.ShapeDtypeStruct((n, d), F32),
        scratch_shapes=[pltpu.VMEM((n_groups, tm + POOL_HALO, gw), F32),
                        pltpu.VMEM((cw.shape[1] // LANES, tm + CONV_HALO, LANES), F32)],
        compiler_params=_params("arbitrary"),
        name="mixer",
    )(x, g, win, wgrp, ps, wpo, cw, wco, wmo)


def _memkv_body(mem_ref, g_ref, wk_ref, wv_ref, kt_ref, v_ref):
    mn = _rms(mem_ref[...], g_ref[...]).astype(BF16)
    kt_ref[...] = _dot(mn, wk_ref[...]).T.astype(BF16)
    v_ref[...] = _dot(mn, wv_ref[...]).astype(BF16)


def _memkv(mem, g_mem, wk, wv):
    b, m, d = mem.shape
    depth = wk.shape[0]
    per_layer = lambda l, i: (l, 0, 0)
    return pl.pallas_call(
        _memkv_body,
        grid=(depth, b),
        in_specs=[pl.BlockSpec((None, m, d), lambda l, i: (i, 0, 0)),
                  pl.BlockSpec((None, 1, d), per_layer),
                  pl.BlockSpec((None, d, d), per_layer),
                  pl.BlockSpec((None, d, d), per_layer)],
        out_specs=[pl.BlockSpec((None, None, d, m), lambda l, i: (l, i, 0, 0)),
                   pl.BlockSpec((None, None, m, d), lambda l, i: (l, i, 0, 0))],
        out_shape=[jax.ShapeDtypeStruct((depth, b, d, m), BF16), jax.ShapeDtypeStruct((depth, b, m, d), BF16)],
        compiler_params=_params("arbitrary", "arbitrary"),
        name="memkv",
    )(mem, g_mem, wk, wv)


def _xattn_body(x_ref, g_ref, wq_ref, kt_ref, v_ref, wo_ref, o_ref):
    x = x_ref[...]
    h = _rms(x, g_ref[...]).astype(BF16)
    q = _dot(h, wq_ref[...])
    dh = q.shape[1] // N_XHEADS
    heads = []
    for hd in range(N_XHEADS):
        sl = slice(hd * dh, (hd + 1) * dh)
        s = _dot(q[:, sl].astype(BF16), kt_ref[sl, :]) * (dh ** -0.5)
        e = jnp.exp(s - jnp.max(s, axis=-1, keepdims=True))
        p = e / jnp.sum(e, axis=-1, keepdims=True)
        heads.append(_dot(p.astype(BF16), v_ref[:, sl]))
    o = jnp.concatenate(heads, axis=1).astype(BF16)
    o_ref[...] = x + _dot(o, wo_ref[...])


def _xattn(x, g, wq, kt, v, wo, *, layer, seq, tm):
    n, d = x.shape
    m = v.shape[2]
    assert seq % tm == 0 and d % N_XHEADS == 0
    tiles_per_seq = seq // tm
    tile = pl.BlockSpec((tm, d), lambda i: (i, 0))
    return pl.pallas_call(
        _xattn_body,
        grid=(n // tm,),
        in_specs=[tile, _resident(g.shape), _resident(wq.shape),
                  pl.BlockSpec((None, None, d, m), lambda i: (layer, i // tiles_per_seq, 0, 0)),
                  pl.BlockSpec((None, None, m, d), lambda i: (layer, i // tiles_per_seq, 0, 0)),
                  _resident(wo.shape)],
        out_specs=tile,
        out_shape=jax.ShapeDtypeStruct((n, d), F32),
        compiler_params=_params("arbitrary"),
        name="xattn",
    )(x, g, wq, kt, v, wo)


def _swiglu(h, wg, wu, wd):
    g = _dot(h, wg)
    return _dot((g * jax.nn.sigmoid(g) * _dot(h, wu)).astype(BF16), wd)


def _ffn_body(x_ref, g_ref, wg_ref, wu_ref, wd_ref, o_ref):
    x = x_ref[...]
    h = _rms(x, g_ref[...]).astype(BF16)
    o_ref[...] = x + _swiglu(h, wg_ref[...], wu_ref[...], wd_ref[...])


def _ffn(x, g, wg, wu, wd, *, tm):
    n, d = x.shape
    tile = pl.BlockSpec((tm, d), lambda i: (i, 0))
    return pl.pallas_call(
        _ffn_body,
        grid=(n // tm,),
        in_specs=[tile, _resident(g.shape), _resident(wg.shape), _resident(wu.shape), _resident(wd.shape)],
        out_specs=tile,
        out_shape=jax.ShapeDtypeStruct((n, d), F32),
        compiler_params=_params("arbitrary"),
        name="ffn",
    )(x, g, wg, wu, wd)


def _router_body(x_ref, g_ref, wrt_ref, h_ref, idx_ref, wt_ref, cnt_ref, base_ref):
    tm = x_ref.shape[0]
    n_exp = wrt_ref.shape[0]

    @pl.when(pl.program_id(0) == 0)
    def _():
        base_ref[...] = jnp.zeros(base_ref.shape, F32)

    h = _rms(x_ref[...], g_ref[...])
    h_ref[...] = h
    logits = [jnp.sum(h * wrt_ref[e:e + 1, :], axis=-1, keepdims=True) for e in range(n_exp)]
    m1, i1 = logits[0], jnp.zeros((tm, 1), jnp.int32)
    for e in range(1, n_exp):
        better = logits[e] > m1
        m1 = jnp.where(better, logits[e], m1)
        i1 = jnp.where(better, e, i1)
    m2, i2 = jnp.full((tm, 1), -jnp.inf, F32), jnp.zeros((tm, 1), jnp.int32)
    for e in range(n_exp):
        better = jnp.logical_and(i1 != e, logits[e] > m2)
        m2 = jnp.where(better, logits[e], m2)
        i2 = jnp.where(better, e, i2)
    e2 = jnp.exp(m2 - m1)
    den = 1.0 + e2
    lane = lax.broadcasted_iota(jnp.int32, (tm, LANES), 1)
    wt_ref[...] = jnp.where(lane == 0, 1.0 / den, jnp.where(lane == 1, e2 / den, 0.0))

    oh1, oh2 = lane == i1, lane == i2
    hits = jnp.logical_or(oh1, oh2)
    earlier = lax.broadcasted_iota(jnp.int32, (tm, tm), 1) < lax.broadcasted_iota(jnp.int32, (tm, tm), 0)
    prefix = _dot(earlier.astype(BF16), hits.astype(BF16)) + base_ref[...]
    r1 = jnp.sum(jnp.where(oh1, prefix, 0.0), axis=-1, keepdims=True).astype(jnp.int32)
    r2 = jnp.sum(jnp.where(oh2, prefix, 0.0), axis=-1, keepdims=True).astype(jnp.int32)
    idx = jnp.where(lane == 0, i1, jnp.where(lane == 1, i2, jnp.where(lane == 2, r1, jnp.where(lane == 3, r2, 0))))
    idx_ref[...] = idx.T[0:SUBLANES, :]
    base_ref[...] += jnp.sum(hits.astype(F32), axis=0, keepdims=True)
    cnt_ref[...] = jnp.broadcast_to(base_ref[...], cnt_ref.shape).astype(jnp.int32)


def _router(x, g, wrt, *, tm):
    n, d = x.shape
    assert wrt.shape[0] <= LANES
    tile = lambda w: pl.BlockSpec((tm, w), lambda i: (i, 0))
    return pl.pallas_call(
        _router_body,
        grid=(n // tm,),
        in_specs=[tile(d), _resident(g.shape), _resident(wrt.shape)],
        out_specs=[tile(d), pl.BlockSpec((SUBLANES, tm), lambda i: (i, 0)), tile(LANES),
                   pl.BlockSpec((SUBLANES, LANES), lambda i: (0, 0))],
        out_shape=[jax.ShapeDtypeStruct((n, d), F32), jax.ShapeDtypeStruct((n // tm * SUBLANES, tm), jnp.int32),
                   jax.ShapeDtypeStruct((n, LANES), F32), jax.ShapeDtypeStruct((SUBLANES, LANES), jnp.int32)],
        scratch_shapes=[pltpu.VMEM((1, LANES), F32)],
        compiler_params=_params("arbitrary"),
        name="router",
    )(x, g, wrt)


def _to_row_tiles(v, dst_ref):
    c = v.shape[1] // LANES
    for j in range(c):
        dst_ref[pl.ds(j, v.shape[0], stride=c), :] = v[:, j * LANES:(j + 1) * LANES]


def _from_row_tiles(src_ref, rows, c):
    return [src_ref[pl.ds(j, rows, stride=c), :] for j in range(c)]


def _tile_copy(src, src_row8, dst, dst_row8, sem):
    return pltpu.make_async_copy(src.at[pl.ds(pl.multiple_of(src_row8, SUBLANES), SUBLANES)],
                                 dst.at[pl.ds(pl.multiple_of(dst_row8, SUBLANES), SUBLANES)], sem)


def _tiles_wait(hbm, vmem, sem):
    pltpu.make_async_copy(hbm.at[pl.ds(0, vmem.shape[0])], vmem, sem).wait()


def _dispatch_body(ends_ref, cnt_ref, pos_ref, h_ref, xs_hbm, zbuf, stage, zsem, sems, *, tmb):
    i, n = pl.program_id(0), pl.num_programs(0)
    tm = h_ref.shape[0]
    slot = lax.rem(i, 2)

    @pl.when(i == 0)
    def _():
        n_exp = ends_ref.shape[0]
        zbuf[...] = jnp.zeros(zbuf.shape, zbuf.dtype)
        for e in range(2 * n_exp):
            first = ends_ref[e] - tmb if e < n_exp else ends_ref[n_exp - 1] + (e - n_exp) * tmb
            live = cnt_ref[e] > 0 if e < n_exp else first * SUBLANES < xs_hbm.shape[0]

            @pl.when(live)
            def _():
                start = pl.multiple_of(first * SUBLANES, tmb * SUBLANES)
                fill = pltpu.make_async_copy(zbuf, xs_hbm.at[pl.ds(start, tmb * SUBLANES)], zsem)
                fill.start()
                fill.wait()

    _to_row_tiles(h_ref[...], stage.at[slot])

    def scatter(g, c):
        for j in range(SUBLANES):
            for k in range(TOP_K):
                _tile_copy(stage.at[slot], (g * SUBLANES + j) * SUBLANES,
                           xs_hbm, pos_ref[0, 0, k * tm + g * SUBLANES + j],
                           sems.at[slot]).start(priority=(j * TOP_K + k) % N_DMA_QUEUES)
        return c
    lax.fori_loop(0, tm // SUBLANES, scatter, 0)

    def drain(s):
        for _ in range(TOP_K):
            _tiles_wait(xs_hbm, stage.at[s], sems.at[s])

    @pl.when(i > 0)
    def _():
        drain(1 - slot)

    @pl.when(i == n - 1)
    def _():
        drain(slot)


def _dispatch(ends, counts, pos8, h, *, n_rows, tm, tmb):
    n, d = h.shape
    assert d == SUBLANES * LANES and tm % SUBLANES == 0
    return pl.pallas_call(
        functools.partial(_dispatch_body, tmb=tmb),
        grid_spec=pltpu.PrefetchScalarGridSpec(
            num_scalar_prefetch=2,
            grid=(n // tm,),
            in_specs=[pl.BlockSpec((1, 1, TOP_K * tm), lambda i, *_: (i, 0, 0), memory_space=pltpu.SMEM),
                      pl.BlockSpec((tm, d), lambda i, *_: (i, 0))],
            out_specs=pl.BlockSpec(memory_space=pl.ANY),
            scratch_shapes=[pltpu.VMEM((tmb * SUBLANES, LANES), F32), pltpu.VMEM((2, tm * SUBLANES, LANES), F32),
                            pltpu.SemaphoreType.DMA(()), pltpu.SemaphoreType.DMA((2,))],
        ),
        out_shape=jax.ShapeDtypeStruct((n_rows * SUBLANES, LANES), F32),
        compiler_params=_params("arbitrary"),
        name="dispatch",
    )(ends, counts, pos8, h)


def _expert_body(te_ref, nv_ref, xs_ref, wg_ref, wu_ref, wd_ref, y_ref, hb, acc):
    i, f = pl.program_id(0), pl.program_id(1)
    tmb, d = hb.shape
    valid = i < nv_ref[0]

    @pl.when(jnp.logical_and(i == 0, f == 0))
    def _():
        acc[...] = jnp.zeros(acc.shape, F32)

    @pl.when(jnp.logical_and(valid, f == 0))
    def _():
        for j, col in enumerate(_from_row_tiles(xs_ref, tmb, d // LANES)):
            hb[:, j * LANES:(j + 1) * LANES] = col.astype(BF16)

    @pl.when(jnp.logical_and(jnp.logical_not(valid), f == 0))
    def _():
        y_ref[...] = jnp.zeros(y_ref.shape, F32)

    @pl.when(valid)
    def _():
        contrib = _swiglu(hb[...], wg_ref[...].astype(BF16), wu_ref[...].astype(BF16), wd_ref[...].astype(BF16))
        acc[...] = jnp.where(f == 0, contrib, acc[...] + contrib)

    @pl.when(jnp.logical_and(valid, f == pl.num_programs(1) - 1))
    def _():
        _to_row_tiles(acc[...], y_ref)


def _experts(tile_expert, n_valid, xs, wg, wu, wd, *, tmb, fc):
    n_tiles = tile_expert.shape[0]
    _, d, ff = wg.shape
    c = d // LANES
    assert ff % fc == 0 and xs.shape == (n_tiles * tmb * c, LANES)
    return pl.pallas_call(
        _expert_body,
        grid_spec=pltpu.PrefetchScalarGridSpec(
            num_scalar_prefetch=2,
            grid=(n_tiles, ff // fc),
            in_specs=[pl.BlockSpec((tmb * c, LANES), lambda i, f, te, nv: (jnp.minimum(i, nv[0] - 1), 0)),
                      pl.BlockSpec((None, d, fc), lambda i, f, te, nv: (te[i], 0, f)),
                      pl.BlockSpec((None, d, fc), lambda i, f, te, nv: (te[i], 0, f)),
                      pl.BlockSpec((None, fc, d), lambda i, f, te, nv: (te[i], f, 0))],
            out_specs=pl.BlockSpec((tmb * c, LANES), lambda i, f, te, nv: (i, 0)),
            scratch_shapes=[pltpu.VMEM((tmb, d), BF16), pltpu.VMEM((tmb, d), F32)],
        ),
        out_shape=jax.ShapeDtypeStruct((n_tiles * tmb * c, LANES), F32),
        compiler_params=_params("arbitrary", "arbitrary"),
        name="experts",
    )(tile_expert, n_valid, xs, wg, wu, wd)


def _combine_body(pos_ref, pos_next_ref, x_ref, wt_ref, g_ref, y_hbm, o_ref, yrows, sems, *, final_norm):
    i, n = pl.program_id(0), pl.num_programs(0)
    tm, d = x_ref.shape
    slot = lax.rem(i, 2)

    def gather(p_ref, s):
        def body(g, c):
            for j in range(SUBLANES):
                for k in range(TOP_K):
                    _tile_copy(y_hbm, p_ref[0, 0, k * tm + g * SUBLANES + j],
                               yrows.at[s, k], (g * SUBLANES + j) * SUBLANES,
                               sems.at[s]).start(priority=(j * TOP_K + k) % N_DMA_QUEUES)
            return c
        lax.fori_loop(0, tm // SUBLANES, body, 0)

    @pl.when(i == 0)
    def _():
        gather(pos_ref, 0)

    @pl.when(i + 1 < n)
    def _():
        gather(pos_next_ref, 1 - slot)

    for k in range(TOP_K):
        _tiles_wait(y_hbm, yrows.at[slot, k], sems.at[slot])
    ys = [_from_row_tiles(yrows.at[slot, k], tm, d // LANES) for k in range(TOP_K)]
    cols = []
    for j in range(d // LANES):
        moe = wt_ref[:, 0:1] * ys[0][j]
        for k in range(1, TOP_K):
            moe = moe + wt_ref[:, k:k + 1] * ys[k][j]
        cols.append(moe)
    out = x_ref[...] + jnp.concatenate(cols, axis=1)
    o_ref[...] = _rms(out, g_ref[...]) if final_norm else out


def _combine(pos8, x, wt, g, y, *, tm, final_norm):
    n, d = x.shape
    assert d == SUBLANES * LANES and tm % SUBLANES == 0
    last = n // tm - 1
    tile = pl.BlockSpec((tm, d), lambda i: (i, 0))
    return pl.pallas_call(
        functools.partial(_combine_body, final_norm=final_norm),
        grid=(n // tm,),
        in_specs=[pl.BlockSpec((1, 1, TOP_K * tm), lambda i: (i, 0, 0), memory_space=pltpu.SMEM),
                  pl.BlockSpec((1, 1, TOP_K * tm), lambda i: (jnp.minimum(i + 1, last), 0, 0),
                               memory_space=pltpu.SMEM),
                  tile, pl.BlockSpec((tm, LANES), lambda i: (i, 0)), _resident(g.shape),
                  pl.BlockSpec(memory_space=pl.ANY)],
        out_specs=tile,
        out_shape=jax.ShapeDtypeStruct((n, d), F32),
        scratch_shapes=[pltpu.VMEM((2, TOP_K, tm * SUBLANES, LANES), F32), pltpu.SemaphoreType.DMA((2,))],
        compiler_params=_params("arbitrary"),
        name="combine",
    )(pos8, pos8, x, wt, g, y)


def _routing_tables(idx, cnt, n, n_exp, tmb):
    tm = idx.shape[1]
    idx = idx.reshape(n // tm, SUBLANES, tm)
    experts = jnp.arange(n_exp, dtype=jnp.int32)
    counts = cnt[0, :n_exp]
    padded = ((counts + (tmb - 1)) // tmb) * tmb
    ends = jnp.sum(jnp.where(experts[None, :] <= experts[:, None], padded[None, :], 0), axis=1)
    starts = ends - padded
    chosen = idx[:, 0:TOP_K, :]
    first = sum(jnp.where(chosen == e, starts[e], 0) for e in range(n_exp))
    pos8 = ((first + idx[:, TOP_K:2 * TOP_K, :]) * SUBLANES).reshape(n // tm, 1, TOP_K * tm)
    n_tiles = (n * TOP_K) // tmb + n_exp
    tile_start = jnp.arange(n_tiles, dtype=jnp.int32) * tmb
    tile_expert = jnp.minimum(jnp.sum((tile_start[:, None] >= ends[None, :]).astype(jnp.int32), axis=1), n_exp - 1)
    return pos8, ends, counts, tile_expert, ends[n_exp - 1:] // tmb, n_tiles


def _moe(x, g, wrt, wg, wu, wd, g_out, *, final_norm):
    h, idx, wt, cnt = _router(x, g, wrt, tm=ROW_TILE)
    pos8, ends, counts, tile_expert, n_valid, n_tiles = _routing_tables(idx, cnt, x.shape[0], wrt.shape[0],
                                                                        EXPERT_ROW_TILE)
    xs = _dispatch(ends, counts, pos8, h, n_rows=n_tiles * EXPERT_ROW_TILE, tm=ROW_TILE, tmb=EXPERT_ROW_TILE)
    y = _experts(tile_expert, n_valid, xs, wg, wu, wd, tmb=EXPERT_ROW_TILE, fc=EXPERT_FF_TILE)
    return _combine(pos8, x, wt, g_out, y, tm=ROW_TILE, final_norm=final_norm)


def _final_norm_body(x_ref, g_ref, o_ref):
    o_ref[...] = _rms(x_ref[...], g_ref[...])


def _final_norm(x, g, *, tm):
    n, d = x.shape
    tile = pl.BlockSpec((tm, d), lambda i: (i, 0))
    return pl.pallas_call(
        _final_norm_body, grid=(n // tm,), in_specs=[tile, _resident(g.shape)], out_specs=tile,
        out_shape=jax.ShapeDtypeStruct((n, d), F32), compiler_params=_params("arbitrary"), name="final_norm",
    )(x, g)


def kernel(x, mem, g_mix, w_in, w_pool_group, pool_scale, w_pool_out, conv_w, w_conv_out, w_mix_out, g_xattn, g_mem, w_xq, w_xk, w_xv, w_xo, g_ffn, w_ff_gate, w_ff_up, w_ff_down, w_router, w_e_gate, w_e_up, w_e_down, g_final):
    b, s, d = x.shape
    depth = g_mix.shape[0]
    bf = lambda w: w.astype(BF16)
    row = lambda v: v.reshape(1, -1)

    kt, v = _memkv(mem, g_mem.reshape(depth, 1, d), bf(w_xk), bf(w_xv))
    xf = x.reshape(b * s, d)
    for l in range(depth):
        xf = _mixer(xf, row(g_mix[l]), bf(w_in[l]), bf(w_pool_group[l]), row(pool_scale[l]), bf(w_pool_out[l]),
                    conv_w[l], bf(w_conv_out[l]), bf(w_mix_out[l]), seq=s, tm=ROW_TILE)
        xf = _xattn(xf, row(g_xattn[l]), bf(w_xq[l]), kt, v, bf(w_xo[l]), layer=l, seq=s, tm=ROW_TILE)
        last = l == depth - 1
        i = l // 2
        if l % 2 == 0:
            xf = _ffn(xf, row(g_ffn[l]), bf(w_ff_gate[i]), bf(w_ff_up[i]), bf(w_ff_down[i]), tm=ROW_TILE)
            if last:
                xf = _final_norm(xf, row(g_final), tm=ROW_TILE)
        else:
            xf = _moe(xf, row(g_ffn[l]), w_router[i].T, w_e_gate[i], w_e_up[i], w_e_down[i],
                      row(g_final), final_norm=last)
    return xf.reshape(b, s, d)
```

```python
import functools

import jax
import jax.numpy as jnp
from jax import lax
from jax.experimental import pallas as pl
from jax.experimental.pallas import tpu as pltpu

F32 = jnp.float32
BF16 = jnp.bfloat16

EPS = 1e-6
POOL_WINDOWS = (2, 4, 8, 16)
CONV_K = 3
N_XHEADS = 4
TOP_K = 2

LANES = 128
SUBLANES = 8
VMEM_LIMIT_BYTES = 56 * 1024 * 1024
N_DMA_QUEUES = 2

POOL_HALO = 16
CONV_HALO = 8

ROW_TILE = 512
MIXER_ROW_TILE = 1024
ROW_CHUNKS = 2
EXPERT_ROW_TILE = 1024
EXPERT_FF_TILE = 512


def _dot(a, b):
    return jnp.dot(a, b, preferred_element_type=F32)


def _sigmoid(z):
    return 0.5 * jnp.tanh(0.5 * z) + 0.5


def _rms(xf, g):
    ms = jnp.mean(xf * xf, axis=-1, keepdims=True)
    return (xf * lax.rsqrt(ms + EPS)) * g


def _resident(shape):
    nd = len(shape)
    return pl.BlockSpec(shape, lambda *_: (0,) * nd, pipeline_mode=pl.Buffered(1))


def _params(*sem):
    return pltpu.CompilerParams(dimension_semantics=sem, vmem_limit_bytes=VMEM_LIMIT_BYTES)


def _mixer_body(x_ref, g_ref, win_ref, wgrp_ref, ps_ref, wpo_ref, cw_ref, wco_ref, wmo_ref, o_ref,
                pool_buf, conv_buf, *, tm, seq):
    n_groups, _, gw = wgrp_ref.shape
    pw = n_groups * gw
    cwid = cw_ref.shape[1]
    d = x_ref.shape[1]
    o_c, o_b, o_u, o_g = pw, pw + cwid, pw + 2 * cwid, pw + 3 * cwid

    pos0 = lax.rem(pl.program_id(0) * tm, seq)

    @pl.when(pos0 == 0)
    def _():
        pool_buf[:, 0:POOL_HALO, :] = jnp.zeros((n_groups, POOL_HALO, gw), F32)
        conv_buf[:, 0:CONV_HALO, :] = jnp.zeros((conv_buf.shape[0], CONV_HALO, LANES), F32)

    @pl.when(pos0 != 0)
    def _():
        pool_buf[:, 0:POOL_HALO, :] = pool_buf[:, tm:tm + POOL_HALO, :]
        conv_buf[:, 0:CONV_HALO, :] = conv_buf[:, tm:tm + CONV_HALO, :]

    x = x_ref[...]
    h = _rms(x, g_ref[...]).astype(BF16)

    u_pool = _dot(h, win_ref[:, 0:o_c])
    pos1 = lax.broadcasted_iota(jnp.int32, (tm, gw), 0) + (pos0 + 1)
    ys = []
    for g, w in enumerate(POOL_WINDOWS):
        ug = u_pool[:, g * gw:(g + 1) * gw]
        pool_buf[g, POOL_HALO:POOL_HALO + tm, :] = ug
        acc = ug
        for k in range(1, w):
            acc = acc + pool_buf[g, POOL_HALO - k:POOL_HALO - k + tm, :]
        head = acc[0:POOL_HALO] / jnp.minimum(pos1[0:POOL_HALO], w).astype(F32)
        mean = jnp.concatenate([head, acc[POOL_HALO:] * (1.0 / w)], axis=0)
        ys.append(_dot((mean - ug).astype(BF16), wgrp_ref[g]))
    y = jnp.concatenate(ys, axis=1) * ps_ref[...]
    y_pool = _dot(y.astype(BF16), wpo_ref[...])

    c_gate = _dot(h, win_ref[:, o_c:o_b])
    u_conv = _dot(h, win_ref[:, o_u:o_g])
    zc = c_gate * u_conv
    ycs = []
    for j in range(cwid // LANES):
        sl = slice(j * LANES, (j + 1) * LANES)
        zj = zc[:, sl]
        conv_buf[j, CONV_HALO:CONV_HALO + tm, :] = zj
        acc = cw_ref[0:1, sl] * conv_buf[j, CONV_HALO - (CONV_K - 1):CONV_HALO - (CONV_K - 1) + tm, :]
        for k in range(1, CONV_K - 1):
            off = CONV_HALO - (CONV_K - 1) + k
            acc = acc + cw_ref[k:k + 1, sl] * conv_buf[j, off:off + tm, :]
        ycs.append(acc + cw_ref[CONV_K - 1:CONV_K, sl] * zj)
    b_gate = _dot(h, win_ref[:, o_b:o_u])
    y_conv = _dot((b_gate * jnp.concatenate(ycs, axis=1)).astype(BF16), wco_ref[...])

    merged = _sigmoid(_dot(h, win_ref[:, o_g:o_g + d])) * y_pool
    merged = merged + _sigmoid(_dot(h, win_ref[:, o_g + d:o_g + 2 * d])) * y_conv
    o_ref[...] = x + _dot(merged.astype(BF16), wmo_ref[...])


def _mixer(x, g, win, wgrp, ps, wpo, cw, wco, wmo, *, seq, tm):
    n, d = x.shape
    n_groups, _, gw = wgrp.shape
    assert seq % tm == 0 and n % tm == 0 and gw == LANES and cw.shape[1] % LANES == 0
    assert len(POOL_WINDOWS) == n_groups and max(POOL_WINDOWS) <= POOL_HALO and cw.shape[0] == CONV_K
    assert all(w & (w - 1) == 0 for w in POOL_WINDOWS)
    tile = pl.BlockSpec((tm, d), lambda i: (i, 0))
    return pl.pallas_call(
        functools.partial(_mixer_body, tm=tm, seq=seq),
        grid=(n // tm,),
        in_specs=[tile, _resident(g.shape), _resident(win.shape), _resident(wgrp.shape), _resident(ps.shape),
                  _resident(wpo.shape), _resident(cw.shape), _resident(wco.shape), _resident(wmo.shape)],
        out_specs=tile,
        out_shape=jax.ShapeDtypeStruct((n, d), F32),
        scratch_shapes=[pltpu.VMEM((n_groups, tm + POOL_HALO, gw), F32),
                        pltpu.VMEM((cw.shape[1] // LANES, tm + CONV_HALO, LANES), F32)],
        compiler_params=_params("arbitrary"),
        name="mixer",
    )(x, g, win, wgrp, ps, wpo, cw, wco, wmo)


def _memkv_body(mem_ref, g_ref, wk_ref, wv_ref, kt_ref, v_ref):
    mn = _rms(mem_ref[...], g_ref[...]).astype(BF16)
    kt_ref[...] = _dot(mn, wk_ref[...]).T.astype(BF16)
    v_ref[...] = _dot(mn, wv_ref[...]).astype(BF16)


def _memkv(mem, g_mem, wk, wv):
    b, m, d = mem.shape
    depth = wk.shape[0]
    per_layer = lambda l, i: (l, 0, 0)
    return pl.pallas_call(
        _memkv_body,
        grid=(depth, b),
        in_specs=[pl.BlockSpec((None, m, d), lambda l, i: (i, 0, 0)),
                  pl.BlockSpec((None, 1, d), per_layer),
                  pl.BlockSpec((None, d, d), per_layer),
                  pl.BlockSpec((None, d, d), per_layer)],
        out_specs=[pl.BlockSpec((None, None, d, m), lambda l, i: (l, i, 0, 0)),
                   pl.BlockSpec((None, None, m, d), lambda l, i: (l, i, 0, 0))],
        out_shape=[jax.ShapeDtypeStruct((depth, b, d, m), BF16), jax.ShapeDtypeStruct((depth, b, m, d), BF16)],
        compiler_params=_params("arbitrary", "arbitrary"),
        name="memkv",
    )(mem, g_mem, wk, wv)


def _xattn_body(x_ref, g_ref, wq_ref, kt_ref, v_ref, wo_ref, o_ref):
    tm = x_ref.shape[0]
    for r0 in range(0, tm, tm // ROW_CHUNKS):
        rows = slice(r0, r0 + tm // ROW_CHUNKS)
        x = x_ref[rows, :]
        h = _rms(x, g_ref[...]).astype(BF16)
        q = _dot(h, wq_ref[...])
        dh = q.shape[1] // N_XHEADS
        heads = []
        for hd in range(N_XHEADS):
            sl = slice(hd * dh, (hd + 1) * dh)
            s = _dot(q[:, sl].astype(BF16), kt_ref[sl, :]) * (dh ** -0.5)
            e = jnp.exp(s - jnp.max(s, axis=-1, keepdims=True))
            p = e / jnp.sum(e, axis=-1, keepdims=True)
            heads.append(_dot(p.astype(BF16), v_ref[:, sl]))
        o = jnp.concatenate(heads, axis=1).astype(BF16)
        o_ref[rows, :] = x + _dot(o, wo_ref[...])


def _xattn(x, g, wq, kt, v, wo, *, layer, seq, tm):
    n, d = x.shape
    m = v.shape[2]
    assert seq % tm == 0 and d % N_XHEADS == 0
    tiles_per_seq = seq // tm
    tile = pl.BlockSpec((tm, d), lambda i: (i, 0))
    return pl.pallas_call(
        _xattn_body,
        grid=(n // tm,),
        in_specs=[tile, _resident(g.shape), _resident(wq.shape),
                  pl.BlockSpec((None, None, d, m), lambda i: (layer, i // tiles_per_seq, 0, 0)),
                  pl.BlockSpec((None, None, m, d), lambda i: (layer, i // tiles_per_seq, 0, 0)),
                  _resident(wo.shape)],
        out_specs=tile,
        out_shape=jax.ShapeDtypeStruct((n, d), F32),
        compiler_params=_params("arbitrary"),
        name="xattn",
    )(x, g, wq, kt, v, wo)


def _swiglu(h, wg, wu, wd):
    g = _dot(h, wg)
    return _dot((g * _sigmoid(g) * _dot(h, wu)).astype(BF16), wd)


def _ffn_body(x_ref, g_ref, wg_ref, wu_ref, wd_ref, o_ref):
    x = x_ref[...]
    h = _rms(x, g_ref[...]).astype(BF16)
    o_ref[...] = x + _swiglu(h, wg_ref[...], wu_ref[...], wd_ref[...])


def _ffn(x, g, wg, wu, wd, *, tm):
    n, d = x.shape
    tile = pl.BlockSpec((tm, d), lambda i: (i, 0))
    return pl.pallas_call(
        _ffn_body,
        grid=(n // tm,),
        in_specs=[tile, _resident(g.shape), _resident(wg.shape), _resident(wu.shape), _resident(wd.shape)],
        out_specs=tile,
        out_shape=jax.ShapeDtypeStruct((n, d), F32),
        compiler_params=_params("arbitrary"),
        name="ffn",
    )(x, g, wg, wu, wd)


def _router_body(x_ref, g_ref, wrt_ref, h_ref, idx_ref, wt_ref, cnt_ref, base_ref):
    tm = x_ref.shape[0]
    n_exp = wrt_ref.shape[0]

    @pl.when(pl.program_id(0) == 0)
    def _():
        base_ref[...] = jnp.zeros(base_ref.shape, F32)

    h = _rms(x_ref[...], g_ref[...])
    h_ref[...] = h
    logits = [jnp.sum(h * wrt_ref[e:e + 1, :], axis=-1, keepdims=True) for e in range(n_exp)]
    m1, i1 = logits[0], jnp.zeros((tm, 1), jnp.int32)
    for e in range(1, n_exp):
        better = logits[e] > m1
        m1 = jnp.where(better, logits[e], m1)
        i1 = jnp.where(better, e, i1)
    m2, i2 = jnp.full((tm, 1), -jnp.inf, F32), jnp.zeros((tm, 1), jnp.int32)
    for e in range(n_exp):
        better = jnp.logical_and(i1 != e, logits[e] > m2)
        m2 = jnp.where(better, logits[e], m2)
        i2 = jnp.where(better, e, i2)
    e2 = jnp.exp(m2 - m1)
    den = 1.0 + e2
    lane = lax.broadcasted_iota(jnp.int32, (tm, LANES), 1)
    wt_ref[...] = jnp.where(lane == 0, 1.0 / den, jnp.where(lane == 1, e2 / den, 0.0))

    oh1, oh2 = lane == i1, lane == i2
    hits = jnp.logical_or(oh1, oh2)
    earlier = lax.broadcasted_iota(jnp.int32, (tm, tm), 1) < lax.broadcasted_iota(jnp.int32, (tm, tm), 0)
    prefix = _dot(earlier.astype(BF16), hits.astype(BF16)) + base_ref[...]
    r1 = jnp.sum(jnp.where(oh1, prefix, 0.0), axis=-1, keepdims=True).astype(jnp.int32)
    r2 = jnp.sum(jnp.where(oh2, prefix, 0.0), axis=-1, keepdims=True).astype(jnp.int32)
    idx = jnp.where(lane == 0, i1, jnp.where(lane == 1, i2, jnp.where(lane == 2, r1, jnp.where(lane == 3, r2, 0))))
    idx_ref[...] = idx.T[0:SUBLANES, :]
    base_ref[...] += jnp.sum(hits.astype(F32), axis=0, keepdims=True)
    cnt_ref[...] = jnp.broadcast_to(base_ref[...], cnt_ref.shape).astype(jnp.int32)


def _router(x, g, wrt, *, tm):
    n, d = x.shape
    assert wrt.shape[0] <= LANES
    tile = lambda w: pl.BlockSpec((tm, w), lambda i: (i, 0))
    return pl.pallas_call(
        _router_body,
        grid=(n // tm,),
        in_specs=[tile(d), _resident(g.shape), _resident(wrt.shape)],
        out_specs=[tile(d), pl.BlockSpec((SUBLANES, tm), lambda i: (i, 0)), tile(LANES),
                   pl.BlockSpec((SUBLANES, LANES), lambda i: (0, 0))],
        out_shape=[jax.ShapeDtypeStruct((n, d), F32), jax.ShapeDtypeStruct((n // tm * SUBLANES, tm), jnp.int32),
                   jax.ShapeDtypeStruct((n, LANES), F32), jax.ShapeDtypeStruct((SUBLANES, LANES), jnp.int32)],
        scratch_shapes=[pltpu.VMEM((1, LANES), F32)],
        compiler_params=_params("arbitrary"),
        name="router",
    )(x, g, wrt)


def _to_row_tiles(v, dst_ref):
    c = v.shape[1] // LANES
    for j in range(c):
        dst_ref[pl.ds(j, v.shape[0], stride=c), :] = v[:, j * LANES:(j + 1) * LANES]


def _from_row_tiles(src_ref, rows, c):
    return [src_ref[pl.ds(j, rows, stride=c), :] for j in range(c)]


def _tile_copy(src, src_row8, dst, dst_row8, sem):
    return pltpu.make_async_copy(src.at[pl.ds(pl.multiple_of(src_row8, SUBLANES), SUBLANES)],
                                 dst.at[pl.ds(pl.multiple_of(dst_row8, SUBLANES), SUBLANES)], sem)


def _tiles_wait(hbm, vmem, sem):
    pltpu.make_async_copy(hbm.at[pl.ds(0, vmem.shape[0])], vmem, sem).wait()


def _dispatch_body(ends_ref, cnt_ref, pos_ref, h_ref, xs_hbm, zbuf, stage, zsem, sems, *, tmb):
    i, n = pl.program_id(0), pl.num_programs(0)
    tm = h_ref.shape[0]
    slot = lax.rem(i, 2)

    @pl.when(i == 0)
    def _():
        n_exp = ends_ref.shape[0]
        zbuf[...] = jnp.zeros(zbuf.shape, zbuf.dtype)
        for e in range(2 * n_exp):
            first = ends_ref[e] - tmb if e < n_exp else ends_ref[n_exp - 1] + (e - n_exp) * tmb
            live = cnt_ref[e] > 0 if e < n_exp else first * SUBLANES < xs_hbm.shape[0]

            @pl.when(live)
            def _():
                start = pl.multiple_of(first * SUBLANES, tmb * SUBLANES)
                fill = pltpu.make_async_copy(zbuf, xs_hbm.at[pl.ds(start, tmb * SUBLANES)], zsem)
                fill.start()
                fill.wait()

    _to_row_tiles(h_ref[...], stage.at[slot])

    def scatter(g, c):
        for j in range(SUBLANES):
            for k in range(TOP_K):
                _tile_copy(stage.at[slot], (g * SUBLANES + j) * SUBLANES,
                           xs_hbm, pos_ref[0, 0, k * tm + g * SUBLANES + j],
                           sems.at[slot]).start(priority=(j * TOP_K + k) % N_DMA_QUEUES)
        return c
    lax.fori_loop(0, tm // SUBLANES, scatter, 0)

    def drain(s):
        for _ in range(TOP_K):
            _tiles_wait(xs_hbm, stage.at[s], sems.at[s])

    @pl.when(i > 0)
    def _():
        drain(1 - slot)

    @pl.when(i == n - 1)
    def _():
        drain(slot)


def _dispatch(ends, counts, pos8, h, *, n_rows, tm, tmb):
    n, d = h.shape
    assert d == SUBLANES * LANES and tm % SUBLANES == 0
    return pl.pallas_call(
        functools.partial(_dispatch_body, tmb=tmb),
        grid_spec=pltpu.PrefetchScalarGridSpec(
            num_scalar_prefetch=2,
            grid=(n // tm,),
            in_specs=[pl.BlockSpec((1, 1, TOP_K * tm), lambda i, *_: (i, 0, 0), memory_space=pltpu.SMEM),
                      pl.BlockSpec((tm, d), lambda i, *_: (i, 0))],
            out_specs=pl.BlockSpec(memory_space=pl.ANY),
            scratch_shapes=[pltpu.VMEM((tmb * SUBLANES, LANES), F32), pltpu.VMEM((2, tm * SUBLANES, LANES), F32),
                            pltpu.SemaphoreType.DMA(()), pltpu.SemaphoreType.DMA((2,))],
        ),
        out_shape=jax.ShapeDtypeStruct((n_rows * SUBLANES, LANES), F32),
        compiler_params=_params("arbitrary"),
        name="dispatch",
    )(ends, counts, pos8, h)


def _expert_body(te_ref, nv_ref, xs_ref, wg_ref, wu_ref, wd_ref, y_ref, hb, acc):
    i, f = pl.program_id(0), pl.program_id(1)
    tmb, d = hb.shape
    valid = i < nv_ref[0]

    @pl.when(jnp.logical_and(i == 0, f == 0))
    def _():
        acc[...] = jnp.zeros(acc.shape, F32)

    @pl.when(jnp.logical_and(valid, f == 0))
    def _():
        for j, col in enumerate(_from_row_tiles(xs_ref, tmb, d // LANES)):
            hb[:, j * LANES:(j + 1) * LANES] = col.astype(BF16)

    @pl.when(jnp.logical_and(jnp.logical_not(valid), f == 0))
    def _():
        y_ref[...] = jnp.zeros(y_ref.shape, F32)

    @pl.when(valid)
    def _():
        contrib = _swiglu(hb[...], wg_ref[...].astype(BF16), wu_ref[...].astype(BF16), wd_ref[...].astype(BF16))
        acc[...] = jnp.where(f == 0, contrib, acc[...] + contrib)

    @pl.when(jnp.logical_and(valid, f == pl.num_programs(1) - 1))
    def _():
        _to_row_tiles(acc[...], y_ref)


def _experts(tile_expert, n_valid, xs, wg, wu, wd, *, tmb, fc):
    n_tiles = tile_expert.shape[0]
    _, d, ff = wg.shape
    c = d // LANES
    assert ff % fc == 0 and xs.shape == (n_tiles * tmb * c, LANES)
    return pl.pallas_call(
        _expert_body,
        grid_spec=pltpu.PrefetchScalarGridSpec(
            num_scalar_prefetch=2,
            grid=(n_tiles, ff // fc),
            in_specs=[pl.BlockSpec((tmb * c, LANES), lambda i, f, te, nv: (jnp.minimum(i, nv[0] - 1), 0)),
                      pl.BlockSpec((None, d, fc), lambda i, f, te, nv: (te[i], 0, f)),
                      pl.BlockSpec((None, d, fc), lambda i, f, te, nv: (te[i], 0, f)),
                      pl.BlockSpec((None, fc, d), lambda i, f, te, nv: (te[i], f, 0))],
            out_specs=pl.BlockSpec((tmb * c, LANES), lambda i, f, te, nv: (i, 0)),
            scratch_shapes=[pltpu.VMEM((tmb, d), BF16), pltpu.VMEM((tmb, d), F32)],
        ),
        out_shape=jax.ShapeDtypeStruct((n_tiles * tmb * c, LANES), F32),
        compiler_params=_params("arbitrary", "arbitrary"),
        name="experts",
    )(tile_expert, n_valid, xs, wg, wu, wd)


def _combine_body(pos_ref, pos_next_ref, x_ref, wt_ref, g_ref, y_hbm, o_ref, yrows, sems, *, final_norm):
    i, n = pl.program_id(0), pl.num_programs(0)
    tm, d = x_ref.shape
    slot = lax.rem(i, 2)

    def gather(p_ref, s):
        def body(g, c):
            for j in range(SUBLANES):
                for k in range(TOP_K):
                    _tile_copy(y_hbm, p_ref[0, 0, k * tm + g * SUBLANES + j],
                               yrows.at[s, k], (g * SUBLANES + j) * SUBLANES,
                               sems.at[s]).start(priority=(j * TOP_K + k) % N_DMA_QUEUES)
            return c
        lax.fori_loop(0, tm // SUBLANES, body, 0)

    @pl.when(i == 0)
    def _():
        gather(pos_ref, 0)

    @pl.when(i + 1 < n)
    def _():
        gather(pos_next_ref, 1 - slot)

    for k in range(TOP_K):
        _tiles_wait(y_hbm, yrows.at[slot, k], sems.at[slot])
    ys = [_from_row_tiles(yrows.at[slot, k], tm, d // LANES) for k in range(TOP_K)]
    cols = []
    for j in range(d // LANES):
        moe = wt_ref[:, 0:1] * ys[0][j]
        for k in range(1, TOP_K):
            moe = moe + wt_ref[:, k:k + 1] * ys[k][j]
        cols.append(moe)
    out = x_ref[...] + jnp.concatenate(cols, axis=1)
    o_ref[...] = _rms(out, g_ref[...]) if final_norm else out


def _combine(pos8, x, wt, g, y, *, tm, final_norm):
    n, d = x.shape
    assert d == SUBLANES * LANES and tm % SUBLANES == 0
    last = n // tm - 1
    tile = pl.BlockSpec((tm, d), lambda i: (i, 0))
    return pl.pallas_call(
        functools.partial(_combine_body, final_norm=final_norm),
        grid=(n // tm,),
        in_specs=[pl.BlockSpec((1, 1, TOP_K * tm), lambda i: (i, 0, 0), memory_space=pltpu.SMEM),
                  pl.BlockSpec((1, 1, TOP_K * tm), lambda i: (jnp.minimum(i + 1, last), 0, 0),
                               memory_space=pltpu.SMEM),
                  tile, pl.BlockSpec((tm, LANES), lambda i: (i, 0)), _resident(g.shape),
                  pl.BlockSpec(memory_space=pl.ANY)],
        out_specs=tile,
        out_shape=jax.ShapeDtypeStruct((n, d), F32),
        scratch_shapes=[pltpu.VMEM((2, TOP_K, tm * SUBLANES, LANES), F32), pltpu.SemaphoreType.DMA((2,))],
        compiler_params=_params("arbitrary"),
        name="combine",
    )(pos8, pos8, x, wt, g, y)


def _routing_tables(idx, cnt, n, n_exp, tmb):
    tm = idx.shape[1]
    idx = idx.reshape(n // tm, SUBLANES, tm)
    experts = jnp.arange(n_exp, dtype=jnp.int32)
    counts = cnt[0, :n_exp]
    padded = ((counts + (tmb - 1)) // tmb) * tmb
    ends = jnp.sum(jnp.where(experts[None, :] <= experts[:, None], padded[None, :], 0), axis=1)
    starts = ends - padded
    chosen = idx[:, 0:TOP_K, :]
    first = sum(jnp.where(chosen == e, starts[e], 0) for e in range(n_exp))
    pos8 = ((first + idx[:, TOP_K:2 * TOP_K, :]) * SUBLANES).reshape(n // tm, 1, TOP_K * tm)
    n_tiles = (n * TOP_K) // tmb + n_exp
    tile_start = jnp.arange(n_tiles, dtype=jnp.int32) * tmb
    tile_expert = jnp.minimum(jnp.sum((tile_start[:, None] >= ends[None, :]).astype(jnp.int32), axis=1), n_exp - 1)
    return pos8, ends, counts, tile_expert, ends[n_exp - 1:] // tmb, n_tiles


def _moe(x, g, wrt, wg, wu, wd, g_out, *, final_norm):
    h, idx, wt, cnt = _router(x, g, wrt, tm=ROW_TILE)
    pos8, ends, counts, tile_expert, n_valid, n_tiles = _routing_tables(idx, cnt, x.shape[0], wrt.shape[0],
                                                                        EXPERT_ROW_TILE)
    xs = _dispatch(ends, counts, pos8, h, n_rows=n_tiles * EXPERT_ROW_TILE, tm=ROW_TILE, tmb=EXPERT_ROW_TILE)
    y = _experts(tile_expert, n_valid, xs, wg, wu, wd, tmb=EXPERT_ROW_TILE, fc=EXPERT_FF_TILE)
    return _combine(pos8, x, wt, g_out, y, tm=ROW_TILE, final_norm=final_norm)


def _final_norm_body(x_ref, g_ref, o_ref):
    o_ref[...] = _rms(x_ref[...], g_ref[...])


def _final_norm(x, g, *, tm):
    n, d = x.shape
    tile = pl.BlockSpec((tm, d), lambda i: (i, 0))
    return pl.pallas_call(
        _final_norm_body, grid=(n // tm,), in_specs=[tile, _resident(g.shape)], out_specs=tile,
        out_shape=jax.ShapeDtypeStruct((n, d), F32), compiler_params=_params("arbitrary"), name="final_norm",
    )(x, g)


def kernel(x, mem, g_mix, w_in, w_pool_group, pool_scale, w_pool_out, conv_w, w_conv_out, w_mix_out, g_xattn, g_mem, w_xq, w_xk, w_xv, w_xo, g_ffn, w_ff_gate, w_ff_up, w_ff_down, w_router, w_e_gate, w_e_up, w_e_down, g_final):
    b, s, d = x.shape
    depth = g_mix.shape[0]
    bf = lambda w: w.astype(BF16)
    row = lambda v: v.reshape(1, -1)

    kt, v = _memkv(mem, g_mem.reshape(depth, 1, d), bf(w_xk), bf(w_xv))
    xf = x.reshape(b * s, d)
    for l in range(depth):
        xf = _mixer(xf, row(g_mix[l]), bf(w_in[l]), bf(w_pool_group[l]), row(pool_scale[l]), bf(w_pool_out[l]),
                    conv_w[l], bf(w_conv_out[l]), bf(w_mix_out[l]), seq=s, tm=MIXER_ROW_TILE)
        xf = _xattn(xf, row(g_xattn[l]), bf(w_xq[l]), kt, v, bf(w_xo[l]), layer=l, seq=s, tm=MIXER_ROW_TILE)
        last = l == depth - 1
        i = l // 2
        if l % 2 == 0:
            xf = _ffn(xf, row(g_ffn[l]), bf(w_ff_gate[i]), bf(w_ff_up[i]), bf(w_ff_down[i]), tm=ROW_TILE)
            if last:
                xf = _final_norm(xf, row(g_final), tm=ROW_TILE)
        else:
            xf = _moe(xf, row(g_ffn[l]), w_router[i].T, w_e_gate[i], w_e_up[i], w_e_down[i],
                      row(g_final), final_norm=last)
    return xf.reshape(b, s, d)
```

```python
import functools

import jax
import jax.numpy as jnp
from jax import lax
from jax.experimental import pallas as pl
from jax.experimental.pallas import tpu as pltpu

F32 = jnp.float32
BF16 = jnp.bfloat16

EPS = 1e-6
POOL_WINDOWS = (2, 4, 8, 16)
CONV_K = 3
N_XHEADS = 4
TOP_K = 2

LANES = 128
SUBLANES = 8
VMEM_LIMIT_BYTES = 56 * 1024 * 1024
N_DMA_QUEUES = 2

POOL_HALO = 16
CONV_HALO = 8

ROW_TILE = 512
MIXER_ROW_TILE = 1024
ROW_CHUNKS = 2
EXPERT_ROW_TILE = 1024
EXPERT_FF_TILE = 512


def _dot(a, b):
    return jnp.dot(a, b, preferred_element_type=F32)


def _sigmoid(z):
    return 0.5 * jnp.tanh(0.5 * z) + 0.5


def _rms(xf, g):
    ms = jnp.mean(xf * xf, axis=-1, keepdims=True)
    return (xf * lax.rsqrt(ms + EPS)) * g


def _resident(shape):
    nd = len(shape)
    return pl.BlockSpec(shape, lambda *_: (0,) * nd, pipeline_mode=pl.Buffered(1))


def _params(*sem):
    return pltpu.CompilerParams(dimension_semantics=sem, vmem_limit_bytes=VMEM_LIMIT_BYTES)


def _mixer_body(x_ref, g_ref, win_ref, wgrp_ref, ps_ref, wpo_ref, cw_ref, wco_ref, wmo_ref, o_ref,
                pool_buf, conv_buf, *, tm, seq):
    n_groups, _, gw = wgrp_ref.shape
    pw = n_groups * gw
    cwid = cw_ref.shape[1]
    d = x_ref.shape[1]
    o_c, o_b, o_u, o_g = pw, pw + cwid, pw + 2 * cwid, pw + 3 * cwid

    pos0 = lax.rem(pl.program_id(0) * tm, seq)

    @pl.when(pos0 == 0)
    def _():
        pool_buf[:, 0:POOL_HALO, :] = jnp.zeros((n_groups, POOL_HALO, gw), F32)
        conv_buf[:, 0:CONV_HALO, :] = jnp.zeros((conv_buf.shape[0], CONV_HALO, LANES), F32)

    @pl.when(pos0 != 0)
    def _():
        pool_buf[:, 0:POOL_HALO, :] = pool_buf[:, tm:tm + POOL_HALO, :]
        conv_buf[:, 0:CONV_HALO, :] = conv_buf[:, tm:tm + CONV_HALO, :]

    x = x_ref[...]
    h = _rms(x, g_ref[...]).astype(BF16)

    u_pool = _dot(h, win_ref[:, 0:o_c])
    pos1 = lax.broadcasted_iota(jnp.int32, (tm, gw), 0) + (pos0 + 1)
    ys = []
    for g, w in enumerate(POOL_WINDOWS):
        ug = u_pool[:, g * gw:(g + 1) * gw]
        pool_buf[g, POOL_HALO:POOL_HALO + tm, :] = ug
        acc = ug
        for k in range(1, w):
            acc = acc + pool_buf[g, POOL_HALO - k:POOL_HALO - k + tm, :]
        head = acc[0:POOL_HALO] / jnp.minimum(pos1[0:POOL_HALO], w).astype(F32)
        mean = jnp.concatenate([head, acc[POOL_HALO:] * (1.0 / w)], axis=0)
        ys.append(_dot((mean - ug).astype(BF16), wgrp_ref[g]))
    y = jnp.concatenate(ys, axis=1) * ps_ref[...]
    y_pool = _dot(y.astype(BF16), wpo_ref[...])

    c_gate = _dot(h, win_ref[:, o_c:o_b])
    u_conv = _dot(h, win_ref[:, o_u:o_g])
    zc = c_gate * u_conv
    ycs = []
    for j in range(cwid // LANES):
        sl = slice(j * LANES, (j + 1) * LANES)
        zj = zc[:, sl]
        conv_buf[j, CONV_HALO:CONV_HALO + tm, :] = zj
        acc = cw_ref[0:1, sl] * conv_buf[j, CONV_HALO - (CONV_K - 1):CONV_HALO - (CONV_K - 1) + tm, :]
        for k in range(1, CONV_K - 1):
            off = CONV_HALO - (CONV_K - 1) + k
            acc = acc + cw_ref[k:k + 1, sl] * conv_buf[j, off:off + tm, :]
        ycs.append(acc + cw_ref[CONV_K - 1:CONV_K, sl] * zj)
    b_gate = _dot(h, win_ref[:, o_b:o_u])
    y_conv = _dot((b_gate * jnp.concatenate(ycs, axis=1)).astype(BF16), wco_ref[...])

    merged = _sigmoid(_dot(h, win_ref[:, o_g:o_g + d])) * y_pool
    merged = merged + _sigmoid(_dot(h, win_ref[:, o_g + d:o_g + 2 * d])) * y_conv
    o_ref[...] = x + _dot(merged.astype(BF16), wmo_ref[...])


def _mixer(x, g, win, wgrp, ps, wpo, cw, wco, wmo, *, seq, tm):
    n, d = x.shape
    n_groups, _, gw = wgrp.shape
    assert seq % tm == 0 and n % tm == 0 and gw == LANES and cw.shape[1] % LANES == 0
    assert len(POOL_WINDOWS) == n_groups and max(POOL_WINDOWS) <= POOL_HALO and cw.shape[0] == CONV_K
    assert all(w & (w - 1) == 0 for w in POOL_WINDOWS)
    tile = pl.BlockSpec((tm, d), lambda i: (i, 0))
    return pl.pallas_call(
        functools.partial(_mixer_body, tm=tm, seq=seq),
        grid=(n // tm,),
        in_specs=[tile, _resident(g.shape), _resident(win.shape), _resident(wgrp.shape), _resident(ps.shape),
                  _resident(wpo.shape), _resident(cw.shape), _resident(wco.shape), _resident(wmo.shape)],
        out_specs=tile,
        out_shape=jax.ShapeDtypeStruct((n, d), F32),
        scratch_shapes=[pltpu.VMEM((n_groups, tm + POOL_HALO, gw), F32),
                        pltpu.VMEM((cw.shape[1] // LANES, tm + CONV_HALO, LANES), F32)],
        compiler_params=_params("arbitrary"),
        name="mixer",
    )(x, g, win, wgrp, ps, wpo, cw, wco, wmo)


def _memkv_body(mem_ref, g_ref, wk_ref, wv_ref, kt_ref, v_ref):
    mn = _rms(mem_ref[...], g_ref[...]).astype(BF16)
    kt_ref[...] = _dot(mn, wk_ref[...]).T.astype(BF16)
    v_ref[...] = _dot(mn, wv_ref[...]).astype(BF16)


def _memkv(mem, g_mem, wk, wv):
    b, m, d = mem.shape
    depth = wk.shape[0]
    per_layer = lambda l, i: (l, 0, 0)
    return pl.pallas_call(
        _memkv_body,
        grid=(depth, b),
        in_specs=[pl.BlockSpec((None, m, d), lambda l, i: (i, 0, 0)),
                  pl.BlockSpec((None, 1, d), per_layer),
                  pl.BlockSpec((None, d, d), per_layer),
                  pl.BlockSpec((None, d, d), per_layer)],
        out_specs=[pl.BlockSpec((None, None, d, m), lambda l, i: (l, i, 0, 0)),
                   pl.BlockSpec((None, None, m, d), lambda l, i: (l, i, 0, 0))],
        out_shape=[jax.ShapeDtypeStruct((depth, b, d, m), BF16), jax.ShapeDtypeStruct((depth, b, m, d), BF16)],
        compiler_params=_params("arbitrary", "arbitrary"),
        name="memkv",
    )(mem, g_mem, wk, wv)


def _xattn_body(x_ref, g_ref, wq_ref, kt_ref, v_ref, wo_ref, o_ref):
    tm = x_ref.shape[0]
    for r0 in range(0, tm, tm // ROW_CHUNKS):
        rows = slice(r0, r0 + tm // ROW_CHUNKS)
        x = x_ref[rows, :]
        h = _rms(x, g_ref[...]).astype(BF16)
        q = _dot(h, wq_ref[...])
        dh = q.shape[1] // N_XHEADS
        heads = []
        for hd in range(N_XHEADS):
            sl = slice(hd * dh, (hd + 1) * dh)
            s = _dot(q[:, sl].astype(BF16), kt_ref[sl, :]) * (dh ** -0.5)
            e = jnp.exp(s - jnp.max(s, axis=-1, keepdims=True))
            p = e / jnp.sum(e, axis=-1, keepdims=True)
            heads.append(_dot(p.astype(BF16), v_ref[:, sl]))
        o = jnp.concatenate(heads, axis=1).astype(BF16)
        o_ref[rows, :] = x + _dot(o, wo_ref[...])


def _xattn(x, g, wq, kt, v, wo, *, layer, seq, tm):
    n, d = x.shape
    m = v.shape[2]
    assert seq % tm == 0 and d % N_XHEADS == 0
    tiles_per_seq = seq // tm
    tile = pl.BlockSpec((tm, d), lambda i: (i, 0))
    return pl.pallas_call(
        _xattn_body,
        grid=(n // tm,),
        in_specs=[tile, _resident(g.shape), _resident(wq.shape),
                  pl.BlockSpec((None, None, d, m), lambda i: (layer, i // tiles_per_seq, 0, 0)),
                  pl.BlockSpec((None, None, m, d), lambda i: (layer, i // tiles_per_seq, 0, 0)),
                  _resident(wo.shape)],
        out_specs=tile,
        out_shape=jax.ShapeDtypeStruct((n, d), F32),
        compiler_params=_params("arbitrary"),
        name="xattn",
    )(x, g, wq, kt, v, wo)


def _swiglu(h, wg, wu, wd):
    g = _dot(h, wg)
    return _dot((g * _sigmoid(g) * _dot(h, wu)).astype(BF16), wd)


def _ffn_body(x_ref, g_ref, wg_ref, wu_ref, wd_ref, o_ref):
    x = x_ref[...]
    h = _rms(x, g_ref[...]).astype(BF16)
    o_ref[...] = x + _swiglu(h, wg_ref[...], wu_ref[...], wd_ref[...])


def _ffn(x, g, wg, wu, wd, *, tm):
    n, d = x.shape
    tile = pl.BlockSpec((tm, d), lambda i: (i, 0))
    return pl.pallas_call(
        _ffn_body,
        grid=(n // tm,),
        in_specs=[tile, _resident(g.shape), _resident(wg.shape), _resident(wu.shape), _resident(wd.shape)],
        out_specs=tile,
        out_shape=jax.ShapeDtypeStruct((n, d), F32),
        compiler_params=_params("arbitrary"),
        name="ffn",
    )(x, g, wg, wu, wd)


def _router_body(x_ref, g_ref, wrt_ref, h_ref, idx_ref, wt_ref, cnt_ref, base_ref):
    tm = x_ref.shape[0]
    n_exp = wrt_ref.shape[0]

    @pl.when(pl.program_id(0) == 0)
    def _():
        base_ref[...] = jnp.zeros(base_ref.shape, F32)

    h = _rms(x_ref[...], g_ref[...])
    h_ref[...] = h
    logits = [jnp.sum(h * wrt_ref[e:e + 1, :], axis=-1, keepdims=True) for e in range(n_exp)]
    m1, i1 = logits[0], jnp.zeros((tm, 1), jnp.int32)
    for e in range(1, n_exp):
        better = logits[e] > m1
        m1 = jnp.where(better, logits[e], m1)
        i1 = jnp.where(better, e, i1)
    m2, i2 = jnp.full((tm, 1), -jnp.inf, F32), jnp.zeros((tm, 1), jnp.int32)
    for e in range(n_exp):
        better = jnp.logical_and(i1 != e, logits[e] > m2)
        m2 = jnp.where(better, logits[e], m2)
        i2 = jnp.where(better, e, i2)
    e2 = jnp.exp(m2 - m1)
    den = 1.0 + e2
    lane = lax.broadcasted_iota(jnp.int32, (tm, LANES), 1)
    wt_ref[...] = jnp.where(lane == 0, 1.0 / den, jnp.where(lane == 1, e2 / den, 0.0))

    oh1, oh2 = lane == i1, lane == i2
    hits = jnp.logical_or(oh1, oh2)
    earlier = lax.broadcasted_iota(jnp.int32, (tm, tm), 1) < lax.broadcasted_iota(jnp.int32, (tm, tm), 0)
    prefix = _dot(earlier.astype(BF16), hits.astype(BF16)) + base_ref[...]
    r1 = jnp.sum(jnp.where(oh1, prefix, 0.0), axis=-1, keepdims=True).astype(jnp.int32)
    r2 = jnp.sum(jnp.where(oh2, prefix, 0.0), axis=-1, keepdims=True).astype(jnp.int32)
    idx = jnp.where(lane == 0, i1, jnp.where(lane == 1, i2, jnp.where(lane == 2, r1, jnp.where(lane == 3, r2, 0))))
    idx_ref[...] = idx.T[0:SUBLANES, :]
    base_ref[...] += jnp.sum(hits.astype(F32), axis=0, keepdims=True)
    cnt_ref[...] = jnp.broadcast_to(base_ref[...], cnt_ref.shape).astype(jnp.int32)


def _router(x, g, wrt, *, tm):
    n, d = x.shape
    assert wrt.shape[0] <= LANES
    tile = lambda w: pl.BlockSpec((tm, w), lambda i: (i, 0))
    return pl.pallas_call(
        _router_body,
        grid=(n // tm,),
        in_specs=[tile(d), _resident(g.shape), _resident(wrt.shape)],
        out_specs=[tile(d), pl.BlockSpec((SUBLANES, tm), lambda i: (i, 0)), tile(LANES),
                   pl.BlockSpec((SUBLANES, LANES), lambda i: (0, 0))],
        out_shape=[jax.ShapeDtypeStruct((n, d), F32), jax.ShapeDtypeStruct((n // tm * SUBLANES, tm), jnp.int32),
                   jax.ShapeDtypeStruct((n, LANES), F32), jax.ShapeDtypeStruct((SUBLANES, LANES), jnp.int32)],
        scratch_shapes=[pltpu.VMEM((1, LANES), F32)],
        compiler_params=_params("arbitrary"),
        name="router",
    )(x, g, wrt)


def _to_row_tiles(v, dst_ref):
    c = v.shape[1] // LANES
    for j in range(c):
        dst_ref[pl.ds(j, v.shape[0], stride=c), :] = v[:, j * LANES:(j + 1) * LANES]


def _from_row_tiles(src_ref, rows, c):
    return [src_ref[pl.ds(j, rows, stride=c), :] for j in range(c)]


def _tile_copy(src, src_row8, dst, dst_row8, sem):
    return pltpu.make_async_copy(src.at[pl.ds(pl.multiple_of(src_row8, SUBLANES), SUBLANES)],
                                 dst.at[pl.ds(pl.multiple_of(dst_row8, SUBLANES), SUBLANES)], sem)


def _tiles_wait(hbm, vmem, sem):
    pltpu.make_async_copy(hbm.at[pl.ds(0, vmem.shape[0])], vmem, sem).wait()


def _dispatch_body(ends_ref, cnt_ref, pos_ref, h_ref, xs_hbm, zbuf, stage, zsem, sems, *, tmb):
    i, n = pl.program_id(0), pl.num_programs(0)
    tm = h_ref.shape[0]
    slot = lax.rem(i, 2)

    @pl.when(i == 0)
    def _():
        n_exp = ends_ref.shape[0]
        zbuf[...] = jnp.zeros(zbuf.shape, zbuf.dtype)
        for e in range(2 * n_exp):
            first = ends_ref[e] - tmb if e < n_exp else ends_ref[n_exp - 1] + (e - n_exp) * tmb
            live = cnt_ref[e] > 0 if e < n_exp else first * SUBLANES < xs_hbm.shape[0]

            @pl.when(live)
            def _():
                start = pl.multiple_of(first * SUBLANES, tmb * SUBLANES)
                fill = pltpu.make_async_copy(zbuf, xs_hbm.at[pl.ds(start, tmb * SUBLANES)], zsem)
                fill.start()
                fill.wait()

    _to_row_tiles(h_ref[...], stage.at[slot])

    def scatter(g, c):
        for j in range(SUBLANES):
            for k in range(TOP_K):
                _tile_copy(stage.at[slot], (g * SUBLANES + j) * SUBLANES,
                           xs_hbm, pos_ref[0, 0, k * tm + g * SUBLANES + j],
                           sems.at[slot]).start(priority=(j * TOP_K + k) % N_DMA_QUEUES)
        return c
    lax.fori_loop(0, tm // SUBLANES, scatter, 0)

    def drain(s):
        for _ in range(TOP_K):
            _tiles_wait(xs_hbm, stage.at[s], sems.at[s])

    @pl.when(i > 0)
    def _():
        drain(1 - slot)

    @pl.when(i == n - 1)
    def _():
        drain(slot)


def _dispatch(ends, counts, pos8, h, *, n_rows, tm, tmb):
    n, d = h.shape
    assert d == SUBLANES * LANES and tm % SUBLANES == 0
    return pl.pallas_call(
        functools.partial(_dispatch_body, tmb=tmb),
        grid_spec=pltpu.PrefetchScalarGridSpec(
            num_scalar_prefetch=2,
            grid=(n // tm,),
            in_specs=[pl.BlockSpec((1, 1, TOP_K * tm), lambda i, *_: (i, 0, 0), memory_space=pltpu.SMEM),
                      pl.BlockSpec((tm, d), lambda i, *_: (i, 0))],
            out_specs=pl.BlockSpec(memory_space=pl.ANY),
            scratch_shapes=[pltpu.VMEM((tmb * SUBLANES, LANES), F32), pltpu.VMEM((2, tm * SUBLANES, LANES), F32),
                            pltpu.SemaphoreType.DMA(()), pltpu.SemaphoreType.DMA((2,))],
        ),
        out_shape=jax.ShapeDtypeStruct((n_rows * SUBLANES, LANES), F32),
        compiler_params=_params("arbitrary"),
        name="dispatch",
    )(ends, counts, pos8, h)


def _expert_body(te_ref, nv_ref, xs_ref, wg_hbm, wu_hbm, wd_hbm, y_ref, hb, acc, wgb, wub, wdb, sems, *, fc):
    i = pl.program_id(0)
    tmb, d = hb.shape
    nf = wg_hbm.shape[2] // fc
    n_valid = nv_ref[0]

    def weight_copies(tile, f, slot):
        e = te_ref[tile]
        cols = pl.ds(pl.multiple_of(f * fc, fc), fc)
        return (pltpu.make_async_copy(wg_hbm.at[e, :, cols], wgb.at[slot], sems.at[0, slot]),
                pltpu.make_async_copy(wu_hbm.at[e, :, cols], wub.at[slot], sems.at[1, slot]),
                pltpu.make_async_copy(wd_hbm.at[e, cols, :], wdb.at[slot], sems.at[2, slot]))

    @pl.when(i >= n_valid)
    def _():
        y_ref[...] = jnp.zeros(y_ref.shape, F32)

    @pl.when(i < n_valid)
    def _():
        @pl.when(i == 0)
        def _():
            acc[...] = jnp.zeros(acc.shape, F32)
            for cp in weight_copies(0, 0, 0):
                cp.start()

        for j, col in enumerate(_from_row_tiles(xs_ref, tmb, d // LANES)):
            hb[:, j * LANES:(j + 1) * LANES] = col.astype(BF16)

        def chunk(f, carry):
            slot = lax.rem(i * nf + f, 2)
            last = f == nf - 1
            next_tile = jnp.where(last, i + 1, i)

            @pl.when(next_tile < n_valid)
            def _():
                for cp in weight_copies(next_tile, jnp.where(last, 0, f + 1), 1 - slot):
                    cp.start()

            for cp in weight_copies(i, f, slot):
                cp.wait()
            contrib = _swiglu(hb[...], wgb[slot].astype(BF16), wub[slot].astype(BF16), wdb[slot].astype(BF16))
            acc[...] = jnp.where(f == 0, contrib, acc[...] + contrib)
            return carry
        lax.fori_loop(0, nf, chunk, 0)
        _to_row_tiles(acc[...], y_ref)


def _experts(tile_expert, n_valid, xs, wg, wu, wd, *, tmb, fc):
    n_tiles = tile_expert.shape[0]
    _, d, ff = wg.shape
    c = d // LANES
    assert ff % fc == 0 and xs.shape == (n_tiles * tmb * c, LANES)
    return pl.pallas_call(
        functools.partial(_expert_body, fc=fc),
        grid_spec=pltpu.PrefetchScalarGridSpec(
            num_scalar_prefetch=2,
            grid=(n_tiles,),
            in_specs=[pl.BlockSpec((tmb * c, LANES), lambda i, te, nv: (jnp.minimum(i, nv[0] - 1), 0)),
                      pl.BlockSpec(memory_space=pl.ANY), pl.BlockSpec(memory_space=pl.ANY),
                      pl.BlockSpec(memory_space=pl.ANY)],
            out_specs=pl.BlockSpec((tmb * c, LANES), lambda i, te, nv: (i, 0)),
            scratch_shapes=[pltpu.VMEM((tmb, d), BF16), pltpu.VMEM((tmb, d), F32),
                            pltpu.VMEM((2, d, fc), wg.dtype), pltpu.VMEM((2, d, fc), wu.dtype),
                            pltpu.VMEM((2, fc, d), wd.dtype), pltpu.SemaphoreType.DMA((3, 2))],
        ),
        out_shape=jax.ShapeDtypeStruct((n_tiles * tmb * c, LANES), F32),
        compiler_params=_params("arbitrary"),
        name="experts",
    )(tile_expert, n_valid, xs, wg, wu, wd)


def _combine_body(pos_ref, pos_next_ref, x_ref, wt_ref, g_ref, y_hbm, o_ref, yrows, sems, *, final_norm):
    i, n = pl.program_id(0), pl.num_programs(0)
    tm, d = x_ref.shape
    slot = lax.rem(i, 2)

    def gather(p_ref, s):
        def body(g, c):
            for j in range(SUBLANES):
                for k in range(TOP_K):
                    _tile_copy(y_hbm, p_ref[0, 0, k * tm + g * SUBLANES + j],
                               yrows.at[s, k], (g * SUBLANES + j) * SUBLANES,
                               sems.at[s]).start(priority=(j * TOP_K + k) % N_DMA_QUEUES)
            return c
        lax.fori_loop(0, tm // SUBLANES, body, 0)

    @pl.when(i == 0)
    def _():
        gather(pos_ref, 0)

    @pl.when(i + 1 < n)
    def _():
        gather(pos_next_ref, 1 - slot)

    for k in range(TOP_K):
        _tiles_wait(y_hbm, yrows.at[slot, k], sems.at[slot])
    ys = [_from_row_tiles(yrows.at[slot, k], tm, d // LANES) for k in range(TOP_K)]
    cols = []
    for j in range(d // LANES):
        moe = wt_ref[:, 0:1] * ys[0][j]
        for k in range(1, TOP_K):
            moe = moe + wt_ref[:, k:k + 1] * ys[k][j]
        cols.append(moe)
    out = x_ref[...] + jnp.concatenate(cols, axis=1)
    o_ref[...] = _rms(out, g_ref[...]) if final_norm else out


def _combine(pos8, x, wt, g, y, *, tm, final_norm):
    n, d = x.shape
    assert d == SUBLANES * LANES and tm % SUBLANES == 0
    last = n // tm - 1
    tile = pl.BlockSpec((tm, d), lambda i: (i, 0))
    return pl.pallas_call(
        functools.partial(_combine_body, final_norm=final_norm),
        grid=(n // tm,),
        in_specs=[pl.BlockSpec((1, 1, TOP_K * tm), lambda i: (i, 0, 0), memory_space=pltpu.SMEM),
                  pl.BlockSpec((1, 1, TOP_K * tm), lambda i: (jnp.minimum(i + 1, last), 0, 0),
                               memory_space=pltpu.SMEM),
                  tile, pl.BlockSpec((tm, LANES), lambda i: (i, 0)), _resident(g.shape),
                  pl.BlockSpec(memory_space=pl.ANY)],
        out_specs=tile,
        out_shape=jax.ShapeDtypeStruct((n, d), F32),
        scratch_shapes=[pltpu.VMEM((2, TOP_K, tm * SUBLANES, LANES), F32), pltpu.SemaphoreType.DMA((2,))],
        compiler_params=_params("arbitrary"),
        name="combine",
    )(pos8, pos8, x, wt, g, y)


def _routing_tables(idx, cnt, n, n_exp, tmb):
    tm = idx.shape[1]
    idx = idx.reshape(n // tm, SUBLANES, tm)
    experts = jnp.arange(n_exp, dtype=jnp.int32)
    counts = cnt[0, :n_exp]
    padded = ((counts + (tmb - 1)) // tmb) * tmb
    ends = jnp.sum(jnp.where(experts[None, :] <= experts[:, None], padded[None, :], 0), axis=1)
    starts = ends - padded
    chosen = idx[:, 0:TOP_K, :]
    first = sum(jnp.where(chosen == e, starts[e], 0) for e in range(n_exp))
    pos8 = ((first + idx[:, TOP_K:2 * TOP_K, :]) * SUBLANES).reshape(n // tm, 1, TOP_K * tm)
    n_tiles = (n * TOP_K) // tmb + n_exp
    tile_start = jnp.arange(n_tiles, dtype=jnp.int32) * tmb
    tile_expert = jnp.minimum(jnp.sum((tile_start[:, None] >= ends[None, :]).astype(jnp.int32), axis=1), n_exp - 1)
    return pos8, ends, counts, tile_expert, ends[n_exp - 1:] // tmb, n_tiles


def _moe(x, g, wrt, wg, wu, wd, g_out, *, final_norm):
    h, idx, wt, cnt = _router(x, g, wrt, tm=ROW_TILE)
    pos8, ends, counts, tile_expert, n_valid, n_tiles = _routing_tables(idx, cnt, x.shape[0], wrt.shape[0],
                                                                        EXPERT_ROW_TILE)
    xs = _dispatch(ends, counts, pos8, h, n_rows=n_tiles * EXPERT_ROW_TILE, tm=ROW_TILE, tmb=EXPERT_ROW_TILE)
    y = _experts(tile_expert, n_valid, xs, wg, wu, wd, tmb=EXPERT_ROW_TILE, fc=EXPERT_FF_TILE)
    return _combine(pos8, x, wt, g_out, y, tm=ROW_TILE, final_norm=final_norm)


def _final_norm_body(x_ref, g_ref, o_ref):
    o_ref[...] = _rms(x_ref[...], g_ref[...])


def _final_norm(x, g, *, tm):
    n, d = x.shape
    tile = pl.BlockSpec((tm, d), lambda i: (i, 0))
    return pl.pallas_call(
        _final_norm_body, grid=(n // tm,), in_specs=[tile, _resident(g.shape)], out_specs=tile,
        out_shape=jax.ShapeDtypeStruct((n, d), F32), compiler_params=_params("arbitrary"), name="final_norm",
    )(x, g)


def kernel(x, mem, g_mix, w_in, w_pool_group, pool_scale, w_pool_out, conv_w, w_conv_out, w_mix_out, g_xattn, g_mem, w_xq, w_xk, w_xv, w_xo, g_ffn, w_ff_gate, w_ff_up, w_ff_down, w_router, w_e_gate, w_e_up, w_e_down, g_final):
    b, s, d = x.shape
    depth = g_mix.shape[0]
    bf = lambda w: w.astype(BF16)
    row = lambda v: v.reshape(1, -1)

    kt, v = _memkv(mem, g_mem.reshape(depth, 1, d), bf(w_xk), bf(w_xv))
    xf = x.reshape(b * s, d)
    for l in range(depth):
        xf = _mixer(xf, row(g_mix[l]), bf(w_in[l]), bf(w_pool_group[l]), row(pool_scale[l]), bf(w_pool_out[l]),
                    conv_w[l], bf(w_conv_out[l]), bf(w_mix_out[l]), seq=s, tm=MIXER_ROW_TILE)
        xf = _xattn(xf, row(g_xattn[l]), bf(w_xq[l]), kt, v, bf(w_xo[l]), layer=l, seq=s, tm=MIXER_ROW_TILE)
        last = l == depth - 1
        i = l // 2
        if l % 2 == 0:
            xf = _ffn(xf, row(g_ffn[l]), bf(w_ff_gate[i]), bf(w_ff_up[i]), bf(w_ff_down[i]), tm=ROW_TILE)
            if last:
                xf = _final_norm(xf, row(g_final), tm=ROW_TILE)
        else:
            xf = _moe(xf, row(g_ffn[l]), w_router[i].T, w_e_gate[i], w_e_up[i], w_e_down[i],
                      row(g_final), final_norm=last)
    return xf.reshape(b, s, d)
```

```python
import functools

import jax
import jax.numpy as jnp
from jax import lax
from jax.experimental import pallas as pl
from jax.experimental.pallas import tpu as pltpu

F32 = jnp.float32
BF16 = jnp.bfloat16

EPS = 1e-6
POOL_WINDOWS = (2, 4, 8, 16)
CONV_K = 3
N_XHEADS = 4
TOP_K = 2

LANES = 128
SUBLANES = 8
VMEM_LIMIT_BYTES = 56 * 1024 * 1024
N_DMA_QUEUES = 2

POOL_HALO = 16
CONV_HALO = 8

ROW_TILE = 512
MIXER_ROW_TILE = 1024
ROW_CHUNKS = 2
MIXER_ROW_CHUNKS = 1
EXPERT_ROW_TILE = 1024
EXPERT_FF_TILE = 512


def _dot(a, b):
    return jnp.dot(a, b, preferred_element_type=F32)


def _sigmoid(z):
    return 0.5 * jnp.tanh(0.5 * z) + 0.5


def _rms(xf, g):
    ms = jnp.mean(xf * xf, axis=-1, keepdims=True)
    return (xf * lax.rsqrt(ms + EPS)) * g


def _resident(shape):
    nd = len(shape)
    return pl.BlockSpec(shape, lambda *_: (0,) * nd, pipeline_mode=pl.Buffered(1))


def _params(*sem):
    return pltpu.CompilerParams(dimension_semantics=sem, vmem_limit_bytes=VMEM_LIMIT_BYTES)


def _mixer_body(x_ref, g_ref, win_ref, wgrp_ref, ps_ref, wpo_ref, cw_ref, wco_ref, wmo_ref, o_ref,
                pool_buf, conv_buf, *, tm, seq):
    n_groups, _, gw = wgrp_ref.shape
    pw = n_groups * gw
    cwid = cw_ref.shape[1]
    d = x_ref.shape[1]
    o_c, o_b, o_u, o_g = pw, pw + cwid, pw + 2 * cwid, pw + 3 * cwid

    pos0 = lax.rem(pl.program_id(0) * tm, seq)

    @pl.when(pos0 == 0)
    def _():
        pool_buf[:, 0:POOL_HALO, :] = jnp.zeros((n_groups, POOL_HALO, gw), F32)
        conv_buf[:, 0:CONV_HALO, :] = jnp.zeros((conv_buf.shape[0], CONV_HALO, LANES), F32)

    @pl.when(pos0 != 0)
    def _():
        pool_buf[:, 0:POOL_HALO, :] = pool_buf[:, tm:tm + POOL_HALO, :]
        conv_buf[:, 0:CONV_HALO, :] = conv_buf[:, tm:tm + CONV_HALO, :]

    cm = tm // MIXER_ROW_CHUNKS
    for r0 in range(0, tm, cm):
        x = x_ref[r0:r0 + cm, :]
        h = _rms(x, g_ref[...]).astype(BF16)

        u_pool = _dot(h, win_ref[:, 0:o_c])
        ys = []
        for g, w in enumerate(POOL_WINDOWS):
            ug = u_pool[:, g * gw:(g + 1) * gw]
            lo = POOL_HALO + r0
            pool_buf[g, lo:lo + cm, :] = ug
            acc = ug
            for k in range(1, w):
                acc = acc + pool_buf[g, lo - k:lo - k + cm, :]
            if r0 == 0:
                pos1 = lax.broadcasted_iota(jnp.int32, (POOL_HALO, gw), 0) + (pos0 + 1)
                head = acc[0:POOL_HALO] / jnp.minimum(pos1, w).astype(F32)
                mean = jnp.concatenate([head, acc[POOL_HALO:] * (1.0 / w)], axis=0)
            else:
                mean = acc * (1.0 / w)
            ys.append(_dot((mean - ug).astype(BF16), wgrp_ref[g]))
        y = jnp.concatenate(ys, axis=1) * ps_ref[...]
        y_pool = _dot(y.astype(BF16), wpo_ref[...])

        c_gate = _dot(h, win_ref[:, o_c:o_b])
        u_conv = _dot(h, win_ref[:, o_u:o_g])
        zc = c_gate * u_conv
        ycs = []
        for j in range(cwid // LANES):
            sl = slice(j * LANES, (j + 1) * LANES)
            zj = zc[:, sl]
            lo = CONV_HALO + r0
            conv_buf[j, lo:lo + cm, :] = zj
            acc = cw_ref[0:1, sl] * conv_buf[j, lo - (CONV_K - 1):lo - (CONV_K - 1) + cm, :]
            for k in range(1, CONV_K - 1):
                off = lo - (CONV_K - 1) + k
                acc = acc + cw_ref[k:k + 1, sl] * conv_buf[j, off:off + cm, :]
            ycs.append(acc + cw_ref[CONV_K - 1:CONV_K, sl] * zj)
        b_gate = _dot(h, win_ref[:, o_b:o_u])
        y_conv = _dot((b_gate * jnp.concatenate(ycs, axis=1)).astype(BF16), wco_ref[...])

        merged = _sigmoid(_dot(h, win_ref[:, o_g:o_g + d])) * y_pool
        merged = merged + _sigmoid(_dot(h, win_ref[:, o_g + d:o_g + 2 * d])) * y_conv
        o_ref[r0:r0 + cm, :] = x + _dot(merged.astype(BF16), wmo_ref[...])


def _mixer(x, g, win, wgrp, ps, wpo, cw, wco, wmo, *, seq, tm):
    n, d = x.shape
    n_groups, _, gw = wgrp.shape
    assert seq % tm == 0 and n % tm == 0 and gw == LANES and cw.shape[1] % LANES == 0
    assert (tm // MIXER_ROW_CHUNKS) % SUBLANES == 0 and tm // MIXER_ROW_CHUNKS >= POOL_HALO
    assert len(POOL_WINDOWS) == n_groups and max(POOL_WINDOWS) <= POOL_HALO and cw.shape[0] == CONV_K
    assert all(w & (w - 1) == 0 for w in POOL_WINDOWS)
    tile = pl.BlockSpec((tm, d), lambda i: (i, 0))
    return pl.pallas_call(
        functools.partial(_mixer_body, tm=tm, seq=seq),
        grid=(n // tm,),
        in_specs=[tile, _resident(g.shape), _resident(win.shape), _resident(wgrp.shape), _resident(ps.shape),
                  _resident(wpo.shape), _resident(cw.shape), _resident(wco.shape), _resident(wmo.shape)],
        out_specs=tile,
        out_shape=jax.ShapeDtypeStruct((n, d), F32),
        scratch_shapes=[pltpu.VMEM((n_groups, tm + POOL_HALO, gw), F32),
                        pltpu.VMEM((cw.shape[1] // LANES, tm + CONV_HALO, LANES), F32)],
        compiler_params=_params("arbitrary"),
        name="mixer",
    )(x, g, win, wgrp, ps, wpo, cw, wco, wmo)


def _memkv_body(mem_ref, g_ref, wk_ref, wv_ref, kt_ref, v_ref):
    mn = _rms(mem_ref[...], g_ref[...]).astype(BF16)
    kt_ref[...] = _dot(mn, wk_ref[...]).T.astype(BF16)
    v_ref[...] = _dot(mn, wv_ref[...]).astype(BF16)


def _memkv(mem, g_mem, wk, wv):
    b, m, d = mem.shape
    depth = wk.shape[0]
    per_layer = lambda l, i: (l, 0, 0)
    return pl.pallas_call(
        _memkv_body,
        grid=(depth, b),
        in_specs=[pl.BlockSpec((None, m, d), lambda l, i: (i, 0, 0)),
                  pl.BlockSpec((None, 1, d), per_layer),
                  pl.BlockSpec((None, d, d), per_layer),
                  pl.BlockSpec((None, d, d), per_layer)],
        out_specs=[pl.BlockSpec((None, None, d, m), lambda l, i: (l, i, 0, 0)),
                   pl.BlockSpec((None, None, m, d), lambda l, i: (l, i, 0, 0))],
        out_shape=[jax.ShapeDtypeStruct((depth, b, d, m), BF16), jax.ShapeDtypeStruct((depth, b, m, d), BF16)],
        compiler_params=_params("arbitrary", "arbitrary"),
        name="memkv",
    )(mem, g_mem, wk, wv)


def _xattn_body(x_ref, g_ref, wq_ref, kt_ref, v_ref, wo_ref, o_ref):
    tm = x_ref.shape[0]
    for r0 in range(0, tm, tm // ROW_CHUNKS):
        rows = slice(r0, r0 + tm // ROW_CHUNKS)
        x = x_ref[rows, :]
        h = _rms(x, g_ref[...]).astype(BF16)
        q = _dot(h, wq_ref[...])
        dh = q.shape[1] // N_XHEADS
        heads = []
        for hd in range(N_XHEADS):
            sl = slice(hd * dh, (hd + 1) * dh)
            s = _dot(q[:, sl].astype(BF16), kt_ref[sl, :]) * (dh ** -0.5)
            e = jnp.exp(s - jnp.max(s, axis=-1, keepdims=True))
            p = e / jnp.sum(e, axis=-1, keepdims=True)
            heads.append(_dot(p.astype(BF16), v_ref[:, sl]))
        o = jnp.concatenate(heads, axis=1).astype(BF16)
        o_ref[rows, :] = x + _dot(o, wo_ref[...])


def _xattn(x, g, wq, kt, v, wo, *, layer, seq, tm):
    n, d = x.shape
    m = v.shape[2]
    assert seq % tm == 0 and d % N_XHEADS == 0
    tiles_per_seq = seq // tm
    tile = pl.BlockSpec((tm, d), lambda i: (i, 0))
    return pl.pallas_call(
        _xattn_body,
        grid=(n // tm,),
        in_specs=[tile, _resident(g.shape), _resident(wq.shape),
                  pl.BlockSpec((None, None, d, m), lambda i: (layer, i // tiles_per_seq, 0, 0)),
                  pl.BlockSpec((None, None, m, d), lambda i: (layer, i // tiles_per_seq, 0, 0)),
                  _resident(wo.shape)],
        out_specs=tile,
        out_shape=jax.ShapeDtypeStruct((n, d), F32),
        compiler_params=_params("arbitrary"),
        name="xattn",
    )(x, g, wq, kt, v, wo)


def _swiglu(h, wg, wu, wd):
    g = _dot(h, wg)
    return _dot((g * _sigmoid(g) * _dot(h, wu)).astype(BF16), wd)


def _ffn_body(x_ref, g_ref, wg_ref, wu_ref, wd_ref, o_ref):
    x = x_ref[...]
    h = _rms(x, g_ref[...]).astype(BF16)
    o_ref[...] = x + _swiglu(h, wg_ref[...], wu_ref[...], wd_ref[...])


def _ffn(x, g, wg, wu, wd, *, tm):
    n, d = x.shape
    tile = pl.BlockSpec((tm, d), lambda i: (i, 0))
    return pl.pallas_call(
        _ffn_body,
        grid=(n // tm,),
        in_specs=[tile, _resident(g.shape), _resident(wg.shape), _resident(wu.shape), _resident(wd.shape)],
        out_specs=tile,
        out_shape=jax.ShapeDtypeStruct((n, d), F32),
        compiler_params=_params("arbitrary"),
        name="ffn",
    )(x, g, wg, wu, wd)


def _router_body(x_ref, g_ref, wrt_ref, h_ref, idx_ref, wt_ref, cnt_ref, base_ref):
    tm = x_ref.shape[0]
    n_exp = wrt_ref.shape[0]

    @pl.when(pl.program_id(0) == 0)
    def _():
        base_ref[...] = jnp.zeros(base_ref.shape, F32)

    h = _rms(x_ref[...], g_ref[...])
    h_ref[...] = h
    logits = [jnp.sum(h * wrt_ref[e:e + 1, :], axis=-1, keepdims=True) for e in range(n_exp)]
    m1, i1 = logits[0], jnp.zeros((tm, 1), jnp.int32)
    for e in range(1, n_exp):
        better = logits[e] > m1
        m1 = jnp.where(better, logits[e], m1)
        i1 = jnp.where(better, e, i1)
    m2, i2 = jnp.full((tm, 1), -jnp.inf, F32), jnp.zeros((tm, 1), jnp.int32)
    for e in range(n_exp):
        better = jnp.logical_and(i1 != e, logits[e] > m2)
        m2 = jnp.where(better, logits[e], m2)
        i2 = jnp.where(better, e, i2)
    e2 = jnp.exp(m2 - m1)
    den = 1.0 + e2
    lane = lax.broadcasted_iota(jnp.int32, (tm, LANES), 1)
    wt_ref[...] = jnp.where(lane == 0, 1.0 / den, jnp.where(lane == 1, e2 / den, 0.0))

    oh1, oh2 = lane == i1, lane == i2
    hits = jnp.logical_or(oh1, oh2)
    earlier = lax.broadcasted_iota(jnp.int32, (tm, tm), 1) < lax.broadcasted_iota(jnp.int32, (tm, tm), 0)
    prefix = _dot(earlier.astype(BF16), hits.astype(BF16)) + base_ref[...]
    r1 = jnp.sum(jnp.where(oh1, prefix, 0.0), axis=-1, keepdims=True).astype(jnp.int32)
    r2 = jnp.sum(jnp.where(oh2, prefix, 0.0), axis=-1, keepdims=True).astype(jnp.int32)
    idx = jnp.where(lane == 0, i1, jnp.where(lane == 1, i2, jnp.where(lane == 2, r1, jnp.where(lane == 3, r2, 0))))
    idx_ref[...] = idx.T[0:SUBLANES, :]
    base_ref[...] += jnp.sum(hits.astype(F32), axis=0, keepdims=True)
    cnt_ref[...] = jnp.broadcast_to(base_ref[...], cnt_ref.shape).astype(jnp.int32)


def _router(x, g, wrt, *, tm):
    n, d = x.shape
    assert wrt.shape[0] <= LANES
    tile = lambda w: pl.BlockSpec((tm, w), lambda i: (i, 0))
    return pl.pallas_call(
        _router_body,
        grid=(n // tm,),
        in_specs=[tile(d), _resident(g.shape), _resident(wrt.shape)],
        out_specs=[tile(d), pl.BlockSpec((SUBLANES, tm), lambda i: (i, 0)), tile(LANES),
                   pl.BlockSpec((SUBLANES, LANES), lambda i: (0, 0))],
        out_shape=[jax.ShapeDtypeStruct((n, d), F32), jax.ShapeDtypeStruct((n // tm * SUBLANES, tm), jnp.int32),
                   jax.ShapeDtypeStruct((n, LANES), F32), jax.ShapeDtypeStruct((SUBLANES, LANES), jnp.int32)],
        scratch_shapes=[pltpu.VMEM((1, LANES), F32)],
        compiler_params=_params("arbitrary"),
        name="router",
    )(x, g, wrt)


def _to_row_tiles(v, dst_ref):
    c = v.shape[1] // LANES
    for j in range(c):
        dst_ref[pl.ds(j, v.shape[0], stride=c), :] = v[:, j * LANES:(j + 1) * LANES]


def _from_row_tiles(src_ref, rows, c):
    return [src_ref[pl.ds(j, rows, stride=c), :] for j in range(c)]


def _tile_copy(src, src_row8, dst, dst_row8, sem):
    return pltpu.make_async_copy(src.at[pl.ds(pl.multiple_of(src_row8, SUBLANES), SUBLANES)],
                                 dst.at[pl.ds(pl.multiple_of(dst_row8, SUBLANES), SUBLANES)], sem)


def _tiles_wait(hbm, vmem, sem):
    pltpu.make_async_copy(hbm.at[pl.ds(0, vmem.shape[0])], vmem, sem).wait()


def _dispatch_body(ends_ref, cnt_ref, pos_ref, h_ref, xs_hbm, zbuf, stage, zsem, sems, *, tmb):
    i, n = pl.program_id(0), pl.num_programs(0)
    tm = h_ref.shape[0]
    slot = lax.rem(i, 2)

    @pl.when(i == 0)
    def _():
        n_exp = ends_ref.shape[0]
        zbuf[...] = jnp.zeros(zbuf.shape, zbuf.dtype)
        for e in range(2 * n_exp):
            first = ends_ref[e] - tmb if e < n_exp else ends_ref[n_exp - 1] + (e - n_exp) * tmb
            live = cnt_ref[e] > 0 if e < n_exp else first * SUBLANES < xs_hbm.shape[0]

            @pl.when(live)
            def _():
                start = pl.multiple_of(first * SUBLANES, tmb * SUBLANES)
                fill = pltpu.make_async_copy(zbuf, xs_hbm.at[pl.ds(start, tmb * SUBLANES)], zsem)
                fill.start()
                fill.wait()

    _to_row_tiles(h_ref[...], stage.at[slot])

    def scatter(g, c):
        for j in range(SUBLANES):
            for k in range(TOP_K):
                _tile_copy(stage.at[slot], (g * SUBLANES + j) * SUBLANES,
                           xs_hbm, pos_ref[0, 0, k * tm + g * SUBLANES + j],
                           sems.at[slot]).start(priority=(j * TOP_K + k) % N_DMA_QUEUES)
        return c
    lax.fori_loop(0, tm // SUBLANES, scatter, 0)

    def drain(s):
        for _ in range(TOP_K):
            _tiles_wait(xs_hbm, stage.at[s], sems.at[s])

    @pl.when(i > 0)
    def _():
        drain(1 - slot)

    @pl.when(i == n - 1)
    def _():
        drain(slot)


def _dispatch(ends, counts, pos8, h, *, n_rows, tm, tmb):
    n, d = h.shape
    assert d == SUBLANES * LANES and tm % SUBLANES == 0
    return pl.pallas_call(
        functools.partial(_dispatch_body, tmb=tmb),
        grid_spec=pltpu.PrefetchScalarGridSpec(
            num_scalar_prefetch=2,
            grid=(n // tm,),
            in_specs=[pl.BlockSpec((1, 1, TOP_K * tm), lambda i, *_: (i, 0, 0), memory_space=pltpu.SMEM),
                      pl.BlockSpec((tm, d), lambda i, *_: (i, 0))],
            out_specs=pl.BlockSpec(memory_space=pl.ANY),
            scratch_shapes=[pltpu.VMEM((tmb * SUBLANES, LANES), F32), pltpu.VMEM((2, tm * SUBLANES, LANES), F32),
                            pltpu.SemaphoreType.DMA(()), pltpu.SemaphoreType.DMA((2,))],
        ),
        out_shape=jax.ShapeDtypeStruct((n_rows * SUBLANES, LANES), F32),
        compiler_params=_params("arbitrary"),
        name="dispatch",
    )(ends, counts, pos8, h)


def _expert_body(te_ref, nv_ref, tr_ref, xs_ref, wg_hbm, wu_hbm, wd_hbm, y_ref, hb, acc, wgb, wub, wdb, sems, *, fc):
    i = pl.program_id(0)
    tmb, d = hb.shape
    nf = wg_hbm.shape[2] // fc
    n_valid = nv_ref[0]

    def weight_copies(tile, f, slot):
        e = te_ref[tile]
        cols = pl.ds(pl.multiple_of(f * fc, fc), fc)
        return (pltpu.make_async_copy(wg_hbm.at[e, :, cols], wgb.at[slot], sems.at[0, slot]),
                pltpu.make_async_copy(wu_hbm.at[e, :, cols], wub.at[slot], sems.at[1, slot]),
                pltpu.make_async_copy(wd_hbm.at[e, cols, :], wdb.at[slot], sems.at[2, slot]))

    @pl.when(i >= n_valid)
    def _():
        y_ref[...] = jnp.zeros(y_ref.shape, F32)

    @pl.when(i < n_valid)
    def _():
        @pl.when(i == 0)
        def _():
            acc[...] = jnp.zeros(acc.shape, F32)
            for cp in weight_copies(0, 0, 0):
                cp.start()

        for j, col in enumerate(_from_row_tiles(xs_ref, tmb, d // LANES)):
            hb[:, j * LANES:(j + 1) * LANES] = col.astype(BF16)

        def chunk(f, carry):
            slot = lax.rem(i * nf + f, 2)
            last = f == nf - 1
            next_tile = jnp.where(last, i + 1, i)

            @pl.when(next_tile < n_valid)
            def _():
                for cp in weight_copies(next_tile, jnp.where(last, 0, f + 1), 1 - slot):
                    cp.start()

            for cp in weight_copies(i, f, slot):
                cp.wait()

            def accumulate(m):
                contrib = _swiglu(hb[0:m, :], wgb[slot].astype(BF16), wub[slot].astype(BF16), wdb[slot].astype(BF16))
                acc[0:m, :] = jnp.where(f == 0, contrib, acc[0:m, :] + contrib)

            @pl.when(tr_ref[i] > tmb // 2)
            def _():
                accumulate(tmb)

            @pl.when(tr_ref[i] <= tmb // 2)
            def _():
                accumulate(tmb // 2)

                @pl.when(f == 0)
                def _():
                    acc[tmb // 2:tmb, :] = jnp.zeros((tmb - tmb // 2, d), F32)
            return carry
        lax.fori_loop(0, nf, chunk, 0)
        _to_row_tiles(acc[...], y_ref)


def _experts(tile_expert, n_valid, tile_rows, xs, wg, wu, wd, *, tmb, fc):
    n_tiles = tile_expert.shape[0]
    _, d, ff = wg.shape
    c = d // LANES
    assert ff % fc == 0 and xs.shape == (n_tiles * tmb * c, LANES)
    return pl.pallas_call(
        functools.partial(_expert_body, fc=fc),
        grid_spec=pltpu.PrefetchScalarGridSpec(
            num_scalar_prefetch=3,
            grid=(n_tiles,),
            in_specs=[pl.BlockSpec((tmb * c, LANES), lambda i, te, nv, tr: (jnp.minimum(i, nv[0] - 1), 0)),
                      pl.BlockSpec(memory_space=pl.ANY), pl.BlockSpec(memory_space=pl.ANY),
                      pl.BlockSpec(memory_space=pl.ANY)],
            out_specs=pl.BlockSpec((tmb * c, LANES), lambda i, te, nv, tr: (i, 0)),
            scratch_shapes=[pltpu.VMEM((tmb, d), BF16), pltpu.VMEM((tmb, d), F32),
                            pltpu.VMEM((2, d, fc), wg.dtype), pltpu.VMEM((2, d, fc), wu.dtype),
                            pltpu.VMEM((2, fc, d), wd.dtype), pltpu.SemaphoreType.DMA((3, 2))],
        ),
        out_shape=jax.ShapeDtypeStruct((n_tiles * tmb * c, LANES), F32),
        compiler_params=_params("arbitrary"),
        name="experts",
    )(tile_expert, n_valid, tile_rows, xs, wg, wu, wd)


def _combine_body(pos_ref, pos_next_ref, x_ref, wt_ref, g_ref, y_hbm, o_ref, yrows, sems, *, final_norm):
    i, n = pl.program_id(0), pl.num_programs(0)
    tm, d = x_ref.shape
    slot = lax.rem(i, 2)

    def gather(p_ref, s):
        def body(g, c):
            for j in range(SUBLANES):
                for k in range(TOP_K):
                    _tile_copy(y_hbm, p_ref[0, 0, k * tm + g * SUBLANES + j],
                               yrows.at[s, k], (g * SUBLANES + j) * SUBLANES,
                               sems.at[s]).start(priority=(j * TOP_K + k) % N_DMA_QUEUES)
            return c
        lax.fori_loop(0, tm // SUBLANES, body, 0)

    @pl.when(i == 0)
    def _():
        gather(pos_ref, 0)

    @pl.when(i + 1 < n)
    def _():
        gather(pos_next_ref, 1 - slot)

    for k in range(TOP_K):
        _tiles_wait(y_hbm, yrows.at[slot, k], sems.at[slot])
    ys = [_from_row_tiles(yrows.at[slot, k], tm, d // LANES) for k in range(TOP_K)]
    cols = []
    for j in range(d // LANES):
        moe = wt_ref[:, 0:1] * ys[0][j]
        for k in range(1, TOP_K):
            moe = moe + wt_ref[:, k:k + 1] * ys[k][j]
        cols.append(moe)
    out = x_ref[...] + jnp.concatenate(cols, axis=1)
    o_ref[...] = _rms(out, g_ref[...]) if final_norm else out


def _combine(pos8, x, wt, g, y, *, tm, final_norm):
    n, d = x.shape
    assert d == SUBLANES * LANES and tm % SUBLANES == 0
    last = n // tm - 1
    tile = pl.BlockSpec((tm, d), lambda i: (i, 0))
    return pl.pallas_call(
        functools.partial(_combine_body, final_norm=final_norm),
        grid=(n // tm,),
        in_specs=[pl.BlockSpec((1, 1, TOP_K * tm), lambda i: (i, 0, 0), memory_space=pltpu.SMEM),
                  pl.BlockSpec((1, 1, TOP_K * tm), lambda i: (jnp.minimum(i + 1, last), 0, 0),
                               memory_space=pltpu.SMEM),
                  tile, pl.BlockSpec((tm, LANES), lambda i: (i, 0)), _resident(g.shape),
                  pl.BlockSpec(memory_space=pl.ANY)],
        out_specs=tile,
        out_shape=jax.ShapeDtypeStruct((n, d), F32),
        scratch_shapes=[pltpu.VMEM((2, TOP_K, tm * SUBLANES, LANES), F32), pltpu.SemaphoreType.DMA((2,))],
        compiler_params=_params("arbitrary"),
        name="combine",
    )(pos8, pos8, x, wt, g, y)


def _routing_tables(idx, cnt, n, n_exp, tmb):
    tm = idx.shape[1]
    idx = idx.reshape(n // tm, SUBLANES, tm)
    experts = jnp.arange(n_exp, dtype=jnp.int32)
    counts = cnt[0, :n_exp]
    padded = ((counts + (tmb - 1)) // tmb) * tmb
    ends = jnp.sum(jnp.where(experts[None, :] <= experts[:, None], padded[None, :], 0), axis=1)
    starts = ends - padded
    chosen = idx[:, 0:TOP_K, :]
    first = sum(jnp.where(chosen == e, starts[e], 0) for e in range(n_exp))
    pos8 = ((first + idx[:, TOP_K:2 * TOP_K, :]) * SUBLANES).reshape(n // tm, 1, TOP_K * tm)
    n_tiles = (n * TOP_K) // tmb + n_exp
    tile_start = jnp.arange(n_tiles, dtype=jnp.int32) * tmb
    tile_expert = jnp.minimum(jnp.sum((tile_start[:, None] >= ends[None, :]).astype(jnp.int32), axis=1), n_exp - 1)
    tile_rows = jnp.clip(sum(jnp.where(tile_expert == e, starts[e] + counts[e], 0) for e in range(n_exp)) - tile_start,
                         0, tmb)
    return pos8, ends, counts, tile_expert, ends[n_exp - 1:] // tmb, tile_rows, n_tiles


def _moe(x, g, wrt, wg, wu, wd, g_out, *, final_norm):
    h, idx, wt, cnt = _router(x, g, wrt, tm=ROW_TILE)
    pos8, ends, counts, tile_expert, n_valid, tile_rows, n_tiles = _routing_tables(idx, cnt, x.shape[0], wrt.shape[0],
                                                                        EXPERT_ROW_TILE)
    xs = _dispatch(ends, counts, pos8, h, n_rows=n_tiles * EXPERT_ROW_TILE, tm=ROW_TILE, tmb=EXPERT_ROW_TILE)
    y = _experts(tile_expert, n_valid, tile_rows, xs, wg, wu, wd, tmb=EXPERT_ROW_TILE, fc=EXPERT_FF_TILE)
    return _combine(pos8, x, wt, g_out, y, tm=ROW_TILE, final_norm=final_norm)


def _final_norm_body(x_ref, g_ref, o_ref):
    o_ref[...] = _rms(x_ref[...], g_ref[...])


def _final_norm(x, g, *, tm):
    n, d = x.shape
    tile = pl.BlockSpec((tm, d), lambda i: (i, 0))
    return pl.pallas_call(
        _final_norm_body, grid=(n // tm,), in_specs=[tile, _resident(g.shape)], out_specs=tile,
        out_shape=jax.ShapeDtypeStruct((n, d), F32), compiler_params=_params("arbitrary"), name="final_norm",
    )(x, g)


def kernel(x, mem, g_mix, w_in, w_pool_group, pool_scale, w_pool_out, conv_w, w_conv_out, w_mix_out, g_xattn, g_mem, w_xq, w_xk, w_xv, w_xo, g_ffn, w_ff_gate, w_ff_up, w_ff_down, w_router, w_e_gate, w_e_up, w_e_down, g_final):
    b, s, d = x.shape
    depth = g_mix.shape[0]
    bf = lambda w: w.astype(BF16)
    row = lambda v: v.reshape(1, -1)

    kt, v = _memkv(mem, g_mem.reshape(depth, 1, d), bf(w_xk), bf(w_xv))
    xf = x.reshape(b * s, d)
    for l in range(depth):
        xf = _mixer(xf, row(g_mix[l]), bf(w_in[l]), bf(w_pool_group[l]), row(pool_scale[l]), bf(w_pool_out[l]),
                    conv_w[l], bf(w_conv_out[l]), bf(w_mix_out[l]), seq=s, tm=MIXER_ROW_TILE)
        xf = _xattn(xf, row(g_xattn[l]), bf(w_xq[l]), kt, v, bf(w_xo[l]), layer=l, seq=s, tm=MIXER_ROW_TILE)
        last = l == depth - 1
        i = l // 2
        if l % 2 == 0:
            xf = _ffn(xf, row(g_ffn[l]), bf(w_ff_gate[i]), bf(w_ff_up[i]), bf(w_ff_down[i]), tm=ROW_TILE)
            if last:
                xf = _final_norm(xf, row(g_final), tm=ROW_TILE)
        else:
            xf = _moe(xf, row(g_ffn[l]), w_router[i].T, w_e_gate[i], w_e_up[i], w_e_down[i],
                      row(g_final), final_norm=last)
    return xf.reshape(b, s, d)
```

```python
import functools

import jax
import jax.numpy as jnp
from jax import lax
from jax.experimental import pallas as pl
from jax.experimental.pallas import tpu as pltpu

F32 = jnp.float32
BF16 = jnp.bfloat16

EPS = 1e-6
POOL_WINDOWS = (2, 4, 8, 16)
CONV_K = 3
N_XHEADS = 4
TOP_K = 2

LANES = 128
SUBLANES = 8
VMEM_LIMIT_BYTES = 56 * 1024 * 1024
N_DMA_QUEUES = 2

POOL_HALO = 16
CONV_HALO = 8

ROW_TILE = 512
MIXER_ROW_TILE = 1024
ROW_CHUNKS = 2
MIXER_ROW_CHUNKS = 1
EXPERT_ROW_TILE = 1024
EXPERT_FF_TILE = 512


def _dot(a, b):
    return jnp.dot(a, b, preferred_element_type=F32)


def _sigmoid(z):
    return 0.5 * jnp.tanh(0.5 * z) + 0.5


def _rms(xf, g):
    ms = jnp.mean(xf * xf, axis=-1, keepdims=True)
    return (xf * lax.rsqrt(ms + EPS)) * g


def _resident(shape):
    nd = len(shape)
    return pl.BlockSpec(shape, lambda *_: (0,) * nd, pipeline_mode=pl.Buffered(1))


def _params(*sem):
    return pltpu.CompilerParams(dimension_semantics=sem, vmem_limit_bytes=VMEM_LIMIT_BYTES)


def _mixer_body(x_ref, g_ref, win_ref, wgrp_ref, ps_ref, wpo_ref, cw_ref, wco_ref, wmo_ref, o_ref,
                pool_buf, conv_buf, *, tm, seq):
    n_groups, _, gw = wgrp_ref.shape
    pw = n_groups * gw
    cwid = cw_ref.shape[1]
    d = x_ref.shape[1]
    o_c, o_b, o_u, o_g = pw, pw + cwid, pw + 2 * cwid, pw + 3 * cwid

    pos0 = lax.rem(pl.program_id(0) * tm, seq)

    @pl.when(pos0 == 0)
    def _():
        pool_buf[:, 0:POOL_HALO, :] = jnp.zeros((n_groups, POOL_HALO, gw), F32)
        conv_buf[:, 0:CONV_HALO, :] = jnp.zeros((conv_buf.shape[0], CONV_HALO, LANES), F32)

    @pl.when(pos0 != 0)
    def _():
        pool_buf[:, 0:POOL_HALO, :] = pool_buf[:, tm:tm + POOL_HALO, :]
        conv_buf[:, 0:CONV_HALO, :] = conv_buf[:, tm:tm + CONV_HALO, :]

    cm = tm // MIXER_ROW_CHUNKS
    for r0 in range(0, tm, cm):
        x = x_ref[r0:r0 + cm, :]
        h = _rms(x, g_ref[...]).astype(BF16)

        u_pool = _dot(h, win_ref[:, 0:o_c])
        ys = []
        for g, w in enumerate(POOL_WINDOWS):
            ug = u_pool[:, g * gw:(g + 1) * gw]
            lo = POOL_HALO + r0
            pool_buf[g, lo:lo + cm, :] = ug
            acc = ug
            for k in range(1, w):
                acc = acc + pool_buf[g, lo - k:lo - k + cm, :]
            if r0 == 0:
                pos1 = lax.broadcasted_iota(jnp.int32, (POOL_HALO, gw), 0) + (pos0 + 1)
                head = acc[0:POOL_HALO] / jnp.minimum(pos1, w).astype(F32)
                mean = jnp.concatenate([head, acc[POOL_HALO:] * (1.0 / w)], axis=0)
            else:
                mean = acc * (1.0 / w)
            ys.append(_dot((mean - ug).astype(BF16), wgrp_ref[g]))
        y = jnp.concatenate(ys, axis=1) * ps_ref[...]
        y_pool = _dot(y.astype(BF16), wpo_ref[...])

        c_gate = _dot(h, win_ref[:, o_c:o_b])
        u_conv = _dot(h, win_ref[:, o_u:o_g])
        zc = c_gate * u_conv
        ycs = []
        for j in range(cwid // LANES):
            sl = slice(j * LANES, (j + 1) * LANES)
            zj = zc[:, sl]
            lo = CONV_HALO + r0
            conv_buf[j, lo:lo + cm, :] = zj
            acc = cw_ref[0:1, sl] * conv_buf[j, lo - (CONV_K - 1):lo - (CONV_K - 1) + cm, :]
            for k in range(1, CONV_K - 1):
                off = lo - (CONV_K - 1) + k
                acc = acc + cw_ref[k:k + 1, sl] * conv_buf[j, off:off + cm, :]
            ycs.append(acc + cw_ref[CONV_K - 1:CONV_K, sl] * zj)
        b_gate = _dot(h, win_ref[:, o_b:o_u])
        y_conv = _dot((b_gate * jnp.concatenate(ycs, axis=1)).astype(BF16), wco_ref[...])

        merged = _sigmoid(_dot(h, win_ref[:, o_g:o_g + d])) * y_pool
        merged = merged + _sigmoid(_dot(h, win_ref[:, o_g + d:o_g + 2 * d])) * y_conv
        o_ref[r0:r0 + cm, :] = x + _dot(merged.astype(BF16), wmo_ref[...])


def _mixer(x, g, win, wgrp, ps, wpo, cw, wco, wmo, *, seq, tm):
    n, d = x.shape
    n_groups, _, gw = wgrp.shape
    assert seq % tm == 0 and n % tm == 0 and gw == LANES and cw.shape[1] % LANES == 0
    assert (tm // MIXER_ROW_CHUNKS) % SUBLANES == 0 and tm // MIXER_ROW_CHUNKS >= POOL_HALO
    assert len(POOL_WINDOWS) == n_groups and max(POOL_WINDOWS) <= POOL_HALO and cw.shape[0] == CONV_K
    assert all(w & (w - 1) == 0 for w in POOL_WINDOWS)
    tile = pl.BlockSpec((tm, d), lambda i: (i, 0))
    return pl.pallas_call(
        functools.partial(_mixer_body, tm=tm, seq=seq),
        grid=(n // tm,),
        in_specs=[tile, _resident(g.shape), _resident(win.shape), _resident(wgrp.shape), _resident(ps.shape),
                  _resident(wpo.shape), _resident(cw.shape), _resident(wco.shape), _resident(wmo.shape)],
        out_specs=tile,
        out_shape=jax.ShapeDtypeStruct((n, d), F32),
        scratch_shapes=[pltpu.VMEM((n_groups, tm + POOL_HALO, gw), F32),
                        pltpu.VMEM((cw.shape[1] // LANES, tm + CONV_HALO, LANES), F32)],
        compiler_params=_params("arbitrary"),
        name="mixer",
    )(x, g, win, wgrp, ps, wpo, cw, wco, wmo)


def _memkv_body(mem_ref, g_ref, wk_ref, wv_ref, kt_ref, v_ref):
    mn = _rms(mem_ref[...], g_ref[...]).astype(BF16)
    kt_ref[...] = _dot(mn, wk_ref[...]).T.astype(BF16)
    v_ref[...] = _dot(mn, wv_ref[...]).astype(BF16)


def _memkv(mem, g_mem, wk, wv):
    b, m, d = mem.shape
    depth = wk.shape[0]
    per_layer = lambda l, i: (l, 0, 0)
    return pl.pallas_call(
        _memkv_body,
        grid=(depth, b),
        in_specs=[pl.BlockSpec((None, m, d), lambda l, i: (i, 0, 0)),
                  pl.BlockSpec((None, 1, d), per_layer),
                  pl.BlockSpec((None, d, d), per_layer),
                  pl.BlockSpec((None, d, d), per_layer)],
        out_specs=[pl.BlockSpec((None, None, d, m), lambda l, i: (l, i, 0, 0)),
                   pl.BlockSpec((None, None, m, d), lambda l, i: (l, i, 0, 0))],
        out_shape=[jax.ShapeDtypeStruct((depth, b, d, m), BF16), jax.ShapeDtypeStruct((depth, b, m, d), BF16)],
        compiler_params=_params("arbitrary", "arbitrary"),
        name="memkv",
    )(mem, g_mem, wk, wv)


def _xattn_body(x_ref, g_ref, wq_ref, kt_ref, v_ref, wo_ref, o_ref):
    tm = x_ref.shape[0]
    for r0 in range(0, tm, tm // ROW_CHUNKS):
        rows = slice(r0, r0 + tm // ROW_CHUNKS)
        x = x_ref[rows, :]
        h = _rms(x, g_ref[...]).astype(BF16)
        q = _dot(h, wq_ref[...])
        dh = q.shape[1] // N_XHEADS
        heads = []
        for hd in range(N_XHEADS):
            sl = slice(hd * dh, (hd + 1) * dh)
            s = _dot(q[:, sl].astype(BF16), kt_ref[sl, :]) * (dh ** -0.5)
            e = jnp.exp(s - jnp.max(s, axis=-1, keepdims=True))
            p = e / jnp.sum(e, axis=-1, keepdims=True)
            heads.append(_dot(p.astype(BF16), v_ref[:, sl]))
        o = jnp.concatenate(heads, axis=1).astype(BF16)
        o_ref[rows, :] = x + _dot(o, wo_ref[...])


def _xattn(x, g, wq, kt, v, wo, *, layer, seq, tm):
    n, d = x.shape
    m = v.shape[2]
    assert seq % tm == 0 and d % N_XHEADS == 0
    tiles_per_seq = seq // tm
    tile = pl.BlockSpec((tm, d), lambda i: (i, 0))
    return pl.pallas_call(
        _xattn_body,
        grid=(n // tm,),
        in_specs=[tile, _resident(g.shape), _resident(wq.shape),
                  pl.BlockSpec((None, None, d, m), lambda i: (layer, i // tiles_per_seq, 0, 0)),
                  pl.BlockSpec((None, None, m, d), lambda i: (layer, i // tiles_per_seq, 0, 0)),
                  _resident(wo.shape)],
        out_specs=tile,
        out_shape=jax.ShapeDtypeStruct((n, d), F32),
        compiler_params=_params("arbitrary"),
        name="xattn",
    )(x, g, wq, kt, v, wo)


def _swiglu(h, wg, wu, wd):
    g = _dot(h, wg)
    return _dot((g * _sigmoid(g) * _dot(h, wu)).astype(BF16), wd)


def _ffn_body(x_ref, g_ref, wg_ref, wu_ref, wd_ref, o_ref):
    x = x_ref[...]
    h = _rms(x, g_ref[...]).astype(BF16)
    o_ref[...] = x + _swiglu(h, wg_ref[...], wu_ref[...], wd_ref[...])


def _ffn(x, g, wg, wu, wd, *, tm):
    n, d = x.shape
    tile = pl.BlockSpec((tm, d), lambda i: (i, 0))
    return pl.pallas_call(
        _ffn_body,
        grid=(n // tm,),
        in_specs=[tile, _resident(g.shape), _resident(wg.shape), _resident(wu.shape), _resident(wd.shape)],
        out_specs=tile,
        out_shape=jax.ShapeDtypeStruct((n, d), F32),
        compiler_params=_params("arbitrary"),
        name="ffn",
    )(x, g, wg, wu, wd)


def _router_body(x_ref, g_ref, wrt_ref, h_ref, idx_ref, wt_ref, cnt_ref, base_ref):
    tm = x_ref.shape[0]
    n_exp = wrt_ref.shape[0]

    @pl.when(pl.program_id(0) == 0)
    def _():
        base_ref[...] = jnp.zeros(base_ref.shape, F32)

    h = _rms(x_ref[...], g_ref[...])
    h_ref[...] = h
    logits = [jnp.sum(h * wrt_ref[e:e + 1, :], axis=-1, keepdims=True) for e in range(n_exp)]
    m1, i1 = logits[0], jnp.zeros((tm, 1), jnp.int32)
    for e in range(1, n_exp):
        better = logits[e] > m1
        m1 = jnp.where(better, logits[e], m1)
        i1 = jnp.where(better, e, i1)
    m2, i2 = jnp.full((tm, 1), -jnp.inf, F32), jnp.zeros((tm, 1), jnp.int32)
    for e in range(n_exp):
        better = jnp.logical_and(i1 != e, logits[e] > m2)
        m2 = jnp.where(better, logits[e], m2)
        i2 = jnp.where(better, e, i2)
    e2 = jnp.exp(m2 - m1)
    den = 1.0 + e2
    lane = lax.broadcasted_iota(jnp.int32, (tm, LANES), 1)
    wt_ref[...] = jnp.where(lane == 0, 1.0 / den, jnp.where(lane == 1, e2 / den, 0.0))

    oh1, oh2 = lane == i1, lane == i2
    hits = jnp.logical_or(oh1, oh2)
    earlier = lax.broadcasted_iota(jnp.int32, (tm, tm), 1) < lax.broadcasted_iota(jnp.int32, (tm, tm), 0)
    prefix = _dot(earlier.astype(BF16), hits.astype(BF16)) + base_ref[...]
    r1 = jnp.sum(jnp.where(oh1, prefix, 0.0), axis=-1, keepdims=True).astype(jnp.int32)
    r2 = jnp.sum(jnp.where(oh2, prefix, 0.0), axis=-1, keepdims=True).astype(jnp.int32)
    idx = jnp.where(lane == 0, i1, jnp.where(lane == 1, i2, jnp.where(lane == 2, r1, jnp.where(lane == 3, r2, 0))))
    idx_ref[...] = idx.T[0:SUBLANES, :]
    base_ref[...] += jnp.sum(hits.astype(F32), axis=0, keepdims=True)
    cnt_ref[...] = jnp.broadcast_to(base_ref[...], cnt_ref.shape).astype(jnp.int32)


def _router(x, g, wrt, *, tm):
    n, d = x.shape
    assert wrt.shape[0] <= LANES
    tile = lambda w: pl.BlockSpec((tm, w), lambda i: (i, 0))
    return pl.pallas_call(
        _router_body,
        grid=(n // tm,),
        in_specs=[tile(d), _resident(g.shape), _resident(wrt.shape)],
        out_specs=[tile(d), pl.BlockSpec((SUBLANES, tm), lambda i: (i, 0)), tile(LANES),
                   pl.BlockSpec((SUBLANES, LANES), lambda i: (0, 0))],
        out_shape=[jax.ShapeDtypeStruct((n, d), F32), jax.ShapeDtypeStruct((n // tm * SUBLANES, tm), jnp.int32),
                   jax.ShapeDtypeStruct((n, LANES), F32), jax.ShapeDtypeStruct((SUBLANES, LANES), jnp.int32)],
        scratch_shapes=[pltpu.VMEM((1, LANES), F32)],
        compiler_params=_params("arbitrary"),
        name="router",
    )(x, g, wrt)


def _to_row_tiles(v, dst_ref):
    c = v.shape[1] // LANES
    for j in range(c):
        dst_ref[pl.ds(j, v.shape[0], stride=c), :] = v[:, j * LANES:(j + 1) * LANES]


def _from_row_tiles(src_ref, rows, c):
    return [src_ref[pl.ds(j, rows, stride=c), :] for j in range(c)]


def _tile_copy(src, src_row8, dst, dst_row8, sem):
    return pltpu.make_async_copy(src.at[pl.ds(pl.multiple_of(src_row8, SUBLANES), SUBLANES)],
                                 dst.at[pl.ds(pl.multiple_of(dst_row8, SUBLANES), SUBLANES)], sem)


def _tiles_wait(hbm, vmem, sem):
    pltpu.make_async_copy(hbm.at[pl.ds(0, vmem.shape[0])], vmem, sem).wait()


def _dispatch_body(ends_ref, cnt_ref, pos_ref, h_ref, xs_hbm, zbuf, stage, zsem, sems, *, tmb):
    i, n = pl.program_id(0), pl.num_programs(0)
    tm = h_ref.shape[0]
    slot = lax.rem(i, 2)

    @pl.when(i == 0)
    def _():
        n_exp = ends_ref.shape[0]
        zbuf[...] = jnp.zeros(zbuf.shape, zbuf.dtype)
        for e in range(2 * n_exp):
            first = ends_ref[e] - tmb if e < n_exp else ends_ref[n_exp - 1] + (e - n_exp) * tmb
            live = cnt_ref[e] > 0 if e < n_exp else first * SUBLANES < xs_hbm.shape[0]

            @pl.when(live)
            def _():
                start = pl.multiple_of(first * SUBLANES, tmb * SUBLANES)
                fill = pltpu.make_async_copy(zbuf, xs_hbm.at[pl.ds(start, tmb * SUBLANES)], zsem)
                fill.start()
                fill.wait()

    _to_row_tiles(h_ref[...], stage.at[slot])

    def scatter(g, c):
        for j in range(SUBLANES):
            for k in range(TOP_K):
                _tile_copy(stage.at[slot], (g * SUBLANES + j) * SUBLANES,
                           xs_hbm, pos_ref[0, 0, k * tm + g * SUBLANES + j],
                           sems.at[slot]).start(priority=(j * TOP_K + k) % N_DMA_QUEUES)
        return c
    lax.fori_loop(0, tm // SUBLANES, scatter, 0)

    def drain(s):
        for _ in range(TOP_K):
            _tiles_wait(xs_hbm, stage.at[s], sems.at[s])

    @pl.when(i > 0)
    def _():
        drain(1 - slot)

    @pl.when(i == n - 1)
    def _():
        drain(slot)


def _dispatch(ends, counts, pos8, h, *, n_rows, tm, tmb):
    n, d = h.shape
    assert d == SUBLANES * LANES and tm % SUBLANES == 0
    return pl.pallas_call(
        functools.partial(_dispatch_body, tmb=tmb),
        grid_spec=pltpu.PrefetchScalarGridSpec(
            num_scalar_prefetch=2,
            grid=(n // tm,),
            in_specs=[pl.BlockSpec((1, 1, TOP_K * tm), lambda i, *_: (i, 0, 0), memory_space=pltpu.SMEM),
                      pl.BlockSpec((tm, d), lambda i, *_: (i, 0))],
            out_specs=pl.BlockSpec(memory_space=pl.ANY),
            scratch_shapes=[pltpu.VMEM((tmb * SUBLANES, LANES), F32), pltpu.VMEM((2, tm * SUBLANES, LANES), F32),
                            pltpu.SemaphoreType.DMA(()), pltpu.SemaphoreType.DMA((2,))],
        ),
        out_shape=jax.ShapeDtypeStruct((n_rows * SUBLANES, LANES), F32),
        compiler_params=_params("arbitrary"),
        name="dispatch",
    )(ends, counts, pos8, h)


def _expert_body(te_ref, nv_ref, tr_ref, xs_ref, wg_hbm, wu_hbm, wd_hbm, y_ref, hb, acc, wgb, wub, wdb, sems, *, fc):
    i = pl.program_id(0)
    tmb, d = hb.shape
    nf = wg_hbm.shape[2] // fc
    n_valid = nv_ref[0]

    def weight_copies(tile, f, slot):
        e = te_ref[tile]
        cols = pl.ds(pl.multiple_of(f * fc, fc), fc)
        return (pltpu.make_async_copy(wg_hbm.at[e, :, cols], wgb.at[slot], sems.at[0, slot]),
                pltpu.make_async_copy(wu_hbm.at[e, :, cols], wub.at[slot], sems.at[1, slot]),
                pltpu.make_async_copy(wd_hbm.at[e, cols, :], wdb.at[slot], sems.at[2, slot]))

    @pl.when(i >= n_valid)
    def _():
        y_ref[...] = jnp.zeros(y_ref.shape, F32)

    def run_chunk(f, first, last):
        slot = lax.rem(i * nf + f, 2)
        next_tile = i + 1 if last else i

        @pl.when(next_tile < n_valid)
        def _():
            for cp in weight_copies(next_tile, 0 if last else f + 1, 1 - slot):
                cp.start()

        for cp in weight_copies(i, f, slot):
            cp.wait()

        def accumulate(m):
            if first:
                h = jnp.concatenate([col.astype(BF16) for col in _from_row_tiles(xs_ref, m, d // LANES)], axis=1)
                hb[0:m, :] = h
            else:
                h = hb[0:m, :]
            contrib = _swiglu(h, wgb[slot].astype(BF16), wub[slot].astype(BF16), wdb[slot].astype(BF16))
            total = contrib if first else acc[0:m, :] + contrib
            if last:
                _to_row_tiles(total, y_ref)
                if m < tmb:
                    y_ref[m * (d // LANES):, :] = jnp.zeros(((tmb - m) * (d // LANES), LANES), F32)
            else:
                acc[0:m, :] = total

        @pl.when(tr_ref[i] > tmb // 2)
        def _():
            accumulate(tmb)

        @pl.when(tr_ref[i] <= tmb // 2)
        def _():
            accumulate(tmb // 2)

    @pl.when(i < n_valid)
    def _():
        @pl.when(i == 0)
        def _():
            for cp in weight_copies(0, 0, 0):
                cp.start()

        run_chunk(0, True, False)

        def middle(f, carry):
            run_chunk(f, False, False)
            return carry
        lax.fori_loop(1, nf - 1, middle, 0)
        run_chunk(nf - 1, False, True)


def _experts(tile_expert, n_valid, tile_rows, xs, wg, wu, wd, *, tmb, fc):
    n_tiles = tile_expert.shape[0]
    _, d, ff = wg.shape
    c = d // LANES
    assert ff % fc == 0 and ff // fc >= 2 and xs.shape == (n_tiles * tmb * c, LANES)
    return pl.pallas_call(
        functools.partial(_expert_body, fc=fc),
        grid_spec=pltpu.PrefetchScalarGridSpec(
            num_scalar_prefetch=3,
            grid=(n_tiles,),
            in_specs=[pl.BlockSpec((tmb * c, LANES), lambda i, te, nv, tr: (jnp.minimum(i, nv[0] - 1), 0)),
                      pl.BlockSpec(memory_space=pl.ANY), pl.BlockSpec(memory_space=pl.ANY),
                      pl.BlockSpec(memory_space=pl.ANY)],
            out_specs=pl.BlockSpec((tmb * c, LANES), lambda i, te, nv, tr: (i, 0)),
            scratch_shapes=[pltpu.VMEM((tmb, d), BF16), pltpu.VMEM((tmb, d), F32),
                            pltpu.VMEM((2, d, fc), wg.dtype), pltpu.VMEM((2, d, fc), wu.dtype),
                            pltpu.VMEM((2, fc, d), wd.dtype), pltpu.SemaphoreType.DMA((3, 2))],
        ),
        out_shape=jax.ShapeDtypeStruct((n_tiles * tmb * c, LANES), F32),
        compiler_params=_params("arbitrary"),
        name="experts",
    )(tile_expert, n_valid, tile_rows, xs, wg, wu, wd)


def _combine_body(pos_ref, pos_next_ref, x_ref, wt_ref, g_ref, y_hbm, o_ref, yrows, sems, *, final_norm):
    i, n = pl.program_id(0), pl.num_programs(0)
    tm, d = x_ref.shape
    slot = lax.rem(i, 2)

    def gather(p_ref, s):
        def body(g, c):
            for j in range(SUBLANES):
                for k in range(TOP_K):
                    _tile_copy(y_hbm, p_ref[0, 0, k * tm + g * SUBLANES + j],
                               yrows.at[s, k], (g * SUBLANES + j) * SUBLANES,
                               sems.at[s]).start(priority=(j * TOP_K + k) % N_DMA_QUEUES)
            return c
        lax.fori_loop(0, tm // SUBLANES, body, 0)

    @pl.when(i == 0)
    def _():
        gather(pos_ref, 0)

    @pl.when(i + 1 < n)
    def _():
        gather(pos_next_ref, 1 - slot)

    for k in range(TOP_K):
        _tiles_wait(y_hbm, yrows.at[slot, k], sems.at[slot])
    ys = [_from_row_tiles(yrows.at[slot, k], tm, d // LANES) for k in range(TOP_K)]
    cols = []
    for j in range(d // LANES):
        moe = wt_ref[:, 0:1] * ys[0][j]
        for k in range(1, TOP_K):
            moe = moe + wt_ref[:, k:k + 1] * ys[k][j]
        cols.append(moe)
    out = x_ref[...] + jnp.concatenate(cols, axis=1)
    o_ref[...] = _rms(out, g_ref[...]) if final_norm else out


def _combine(pos8, x, wt, g, y, *, tm, final_norm):
    n, d = x.shape
    assert d == SUBLANES * LANES and tm % SUBLANES == 0
    last = n // tm - 1
    tile = pl.BlockSpec((tm, d), lambda i: (i, 0))
    return pl.pallas_call(
        functools.partial(_combine_body, final_norm=final_norm),
        grid=(n // tm,),
        in_specs=[pl.BlockSpec((1, 1, TOP_K * tm), lambda i: (i, 0, 0), memory_space=pltpu.SMEM),
                  pl.BlockSpec((1, 1, TOP_K * tm), lambda i: (jnp.minimum(i + 1, last), 0, 0),
                               memory_space=pltpu.SMEM),
                  tile, pl.BlockSpec((tm, LANES), lambda i: (i, 0)), _resident(g.shape),
                  pl.BlockSpec(memory_space=pl.ANY)],
        out_specs=tile,
        out_shape=jax.ShapeDtypeStruct((n, d), F32),
        scratch_shapes=[pltpu.VMEM((2, TOP_K, tm * SUBLANES, LANES), F32), pltpu.SemaphoreType.DMA((2,))],
        compiler_params=_params("arbitrary"),
        name="combine",
    )(pos8, pos8, x, wt, g, y)


def _routing_tables(idx, cnt, n, n_exp, tmb):
    tm = idx.shape[1]
    idx = idx.reshape(n // tm, SUBLANES, tm)
    experts = jnp.arange(n_exp, dtype=jnp.int32)
    counts = cnt[0, :n_exp]
    padded = ((counts + (tmb - 1)) // tmb) * tmb
    ends = jnp.sum(jnp.where(experts[None, :] <= experts[:, None], padded[None, :], 0), axis=1)
    starts = ends - padded
    chosen = idx[:, 0:TOP_K, :]
    first = sum(jnp.where(chosen == e, starts[e], 0) for e in range(n_exp))
    pos8 = ((first + idx[:, TOP_K:2 * TOP_K, :]) * SUBLANES).reshape(n // tm, 1, TOP_K * tm)
    n_tiles = (n * TOP_K) // tmb + n_exp
    tile_start = jnp.arange(n_tiles, dtype=jnp.int32) * tmb
    tile_expert = jnp.minimum(jnp.sum((tile_start[:, None] >= ends[None, :]).astype(jnp.int32), axis=1), n_exp - 1)
    tile_rows = jnp.clip(sum(jnp.where(tile_expert == e, starts[e] + counts[e], 0) for e in range(n_exp)) - tile_start,
                         0, tmb)
    return pos8, ends, counts, tile_expert, ends[n_exp - 1:] // tmb, tile_rows, n_tiles


def _moe(x, g, wrt, wg, wu, wd, g_out, *, final_norm):
    h, idx, wt, cnt = _router(x, g, wrt, tm=ROW_TILE)
    pos8, ends, counts, tile_expert, n_valid, tile_rows, n_tiles = _routing_tables(idx, cnt, x.shape[0], wrt.shape[0],
                                                                        EXPERT_ROW_TILE)
    xs = _dispatch(ends, counts, pos8, h, n_rows=n_tiles * EXPERT_ROW_TILE, tm=ROW_TILE, tmb=EXPERT_ROW_TILE)
    y = _experts(tile_expert, n_valid, tile_rows, xs, wg, wu, wd, tmb=EXPERT_ROW_TILE, fc=EXPERT_FF_TILE)
    return _combine(pos8, x, wt, g_out, y, tm=ROW_TILE, final_norm=final_norm)


def _final_norm_body(x_ref, g_ref, o_ref):
    o_ref[...] = _rms(x_ref[...], g_ref[...])


def _final_norm(x, g, *, tm):
    n, d = x.shape
    tile = pl.BlockSpec((tm, d), lambda i: (i, 0))
    return pl.pallas_call(
        _final_norm_body, grid=(n // tm,), in_specs=[tile, _resident(g.shape)], out_specs=tile,
        out_shape=jax.ShapeDtypeStruct((n, d), F32), compiler_params=_params("arbitrary"), name="final_norm",
    )(x, g)


def kernel(x, mem, g_mix, w_in, w_pool_group, pool_scale, w_pool_out, conv_w, w_conv_out, w_mix_out, g_xattn, g_mem, w_xq, w_xk, w_xv, w_xo, g_ffn, w_ff_gate, w_ff_up, w_ff_down, w_router, w_e_gate, w_e_up, w_e_down, g_final):
    b, s, d = x.shape
    depth = g_mix.shape[0]
    bf = lambda w: w.astype(BF16)
    row = lambda v: v.reshape(1, -1)

    kt, v = _memkv(mem, g_mem.reshape(depth, 1, d), bf(w_xk), bf(w_xv))
    xf = x.reshape(b * s, d)
    for l in range(depth):
        xf = _mixer(xf, row(g_mix[l]), bf(w_in[l]), bf(w_pool_group[l]), row(pool_scale[l]), bf(w_pool_out[l]),
                    conv_w[l], bf(w_conv_out[l]), bf(w_mix_out[l]), seq=s, tm=MIXER_ROW_TILE)
        xf = _xattn(xf, row(g_xattn[l]), bf(w_xq[l]), kt, v, bf(w_xo[l]), layer=l, seq=s, tm=MIXER_ROW_TILE)
        last = l == depth - 1
        i = l // 2
        if l % 2 == 0:
            xf = _ffn(xf, row(g_ffn[l]), bf(w_ff_gate[i]), bf(w_ff_up[i]), bf(w_ff_down[i]), tm=ROW_TILE)
            if last:
                xf = _final_norm(xf, row(g_final), tm=ROW_TILE)
        else:
            xf = _moe(xf, row(g_ffn[l]), w_router[i].T, w_e_gate[i], w_e_up[i], w_e_down[i],
                      row(g_final), final_norm=last)
    return xf.reshape(b, s, d)
```

```python
import functools

import jax
import jax.numpy as jnp
from jax import lax
from jax.experimental import pallas as pl
from jax.experimental.pallas import tpu as pltpu

F32 = jnp.float32
BF16 = jnp.bfloat16

EPS = 1e-6
POOL_WINDOWS = (2, 4, 8, 16)
CONV_K = 3
N_XHEADS = 4
TOP_K = 2

LANES = 128
SUBLANES = 8
VMEM_LIMIT_BYTES = 56 * 1024 * 1024
N_DMA_QUEUES = 2
WEIGHT_DMA_QUEUE = 1

POOL_HALO = 16
CONV_HALO = 8

ROW_TILE = 512
MIXER_ROW_TILE = 1024
ROW_CHUNKS = 2
MIXER_ROW_CHUNKS = 1
ROUTER_ROW_CHUNKS = 1
MOE_ROW_TILE = 1024
EXPERT_ROW_TILE = 1024
EXPERT_FF_TILE = 512


def _dot(a, b):
    return jnp.dot(a, b, preferred_element_type=F32)


def _sigmoid(z):
    return 0.5 * jnp.tanh(0.5 * z) + 0.5


def _rms(xf, g):
    ms = jnp.mean(xf * xf, axis=-1, keepdims=True)
    return (xf * lax.rsqrt(ms + EPS)) * g


def _resident(shape):
    nd = len(shape)
    return pl.BlockSpec(shape, lambda *_: (0,) * nd, pipeline_mode=pl.Buffered(1))


def _params(*sem):
    return pltpu.CompilerParams(dimension_semantics=sem, vmem_limit_bytes=VMEM_LIMIT_BYTES)


def _mixer_body(x_ref, g_ref, win_ref, wgrp_ref, ps_ref, wpo_ref, cw_ref, wco_ref, wmo_ref, o_ref,
                pool_buf, conv_buf, *, tm, seq):
    n_groups, _, gw = wgrp_ref.shape
    pw = n_groups * gw
    cwid = cw_ref.shape[1]
    d = x_ref.shape[1]
    o_c, o_b, o_u, o_g = pw, pw + cwid, pw + 2 * cwid, pw + 3 * cwid

    pos0 = lax.rem(pl.program_id(0) * tm, seq)

    @pl.when(pos0 == 0)
    def _():
        pool_buf[:, 0:POOL_HALO, :] = jnp.zeros((n_groups, POOL_HALO, gw), F32)
        conv_buf[:, 0:CONV_HALO, :] = jnp.zeros((conv_buf.shape[0], CONV_HALO, LANES), F32)

    @pl.when(pos0 != 0)
    def _():
        pool_buf[:, 0:POOL_HALO, :] = pool_buf[:, tm:tm + POOL_HALO, :]
        conv_buf[:, 0:CONV_HALO, :] = conv_buf[:, tm:tm + CONV_HALO, :]

    cm = tm // MIXER_ROW_CHUNKS
    for r0 in range(0, tm, cm):
        x = x_ref[r0:r0 + cm, :]
        h = _rms(x, g_ref[...]).astype(BF16)

        u_pool = _dot(h, win_ref[:, 0:o_c])
        ys = []
        for g, w in enumerate(POOL_WINDOWS):
            ug = u_pool[:, g * gw:(g + 1) * gw]
            lo = POOL_HALO + r0
            pool_buf[g, lo:lo + cm, :] = ug
            acc = ug
            for k in range(1, w):
                acc = acc + pool_buf[g, lo - k:lo - k + cm, :]
            if r0 == 0:
                pos1 = lax.broadcasted_iota(jnp.int32, (POOL_HALO, gw), 0) + (pos0 + 1)
                head = acc[0:POOL_HALO] / jnp.minimum(pos1, w).astype(F32)
                mean = jnp.concatenate([head, acc[POOL_HALO:] * (1.0 / w)], axis=0)
            else:
                mean = acc * (1.0 / w)
            ys.append(_dot((mean - ug).astype(BF16), wgrp_ref[g]))
        y = jnp.concatenate(ys, axis=1) * ps_ref[...]
        y_pool = _dot(y.astype(BF16), wpo_ref[...])

        c_gate = _dot(h, win_ref[:, o_c:o_b])
        u_conv = _dot(h, win_ref[:, o_u:o_g])
        zc = c_gate * u_conv
        ycs = []
        for j in range(cwid // LANES):
            sl = slice(j * LANES, (j + 1) * LANES)
            zj = zc[:, sl]
            lo = CONV_HALO + r0
            conv_buf[j, lo:lo + cm, :] = zj
            acc = cw_ref[0:1, sl] * conv_buf[j, lo - (CONV_K - 1):lo - (CONV_K - 1) + cm, :]
            for k in range(1, CONV_K - 1):
                off = lo - (CONV_K - 1) + k
                acc = acc + cw_ref[k:k + 1, sl] * conv_buf[j, off:off + cm, :]
            ycs.append(acc + cw_ref[CONV_K - 1:CONV_K, sl] * zj)
        b_gate = _dot(h, win_ref[:, o_b:o_u])
        y_conv = _dot((b_gate * jnp.concatenate(ycs, axis=1)).astype(BF16), wco_ref[...])

        merged = _sigmoid(_dot(h, win_ref[:, o_g:o_g + d])) * y_pool
        merged = merged + _sigmoid(_dot(h, win_ref[:, o_g + d:o_g + 2 * d])) * y_conv
        o_ref[r0:r0 + cm, :] = x + _dot(merged.astype(BF16), wmo_ref[...])


def _mixer(x, g, win, wgrp, ps, wpo, cw, wco, wmo, *, seq, tm):
    n, d = x.shape
    n_groups, _, gw = wgrp.shape
    assert seq % tm == 0 and n % tm == 0 and gw == LANES and cw.shape[1] % LANES == 0
    assert (tm // MIXER_ROW_CHUNKS) % SUBLANES == 0 and tm // MIXER_ROW_CHUNKS >= POOL_HALO
    assert len(POOL_WINDOWS) == n_groups and max(POOL_WINDOWS) <= POOL_HALO and cw.shape[0] == CONV_K
    assert all(w & (w - 1) == 0 for w in POOL_WINDOWS)
    tile = pl.BlockSpec((tm, d), lambda i: (i, 0))
    return pl.pallas_call(
        functools.partial(_mixer_body, tm=tm, seq=seq),
        grid=(n // tm,),
        in_specs=[tile, _resident(g.shape), _resident(win.shape), _resident(wgrp.shape), _resident(ps.shape),
                  _resident(wpo.shape), _resident(cw.shape), _resident(wco.shape), _resident(wmo.shape)],
        out_specs=tile,
        out_shape=jax.ShapeDtypeStruct((n, d), F32),
        scratch_shapes=[pltpu.VMEM((n_groups, tm + POOL_HALO, gw), F32),
                        pltpu.VMEM((cw.shape[1] // LANES, tm + CONV_HALO, LANES), F32)],
        compiler_params=_params("arbitrary"),
        name="mixer",
    )(x, g, win, wgrp, ps, wpo, cw, wco, wmo)


def _memkv_body(mem_ref, g_ref, wk_ref, wv_ref, kt_ref, v_ref):
    mn = _rms(mem_ref[...], g_ref[...]).astype(BF16)
    kt_ref[...] = _dot(mn, wk_ref[...]).T.astype(BF16)
    v_ref[...] = _dot(mn, wv_ref[...]).astype(BF16)


def _memkv(mem, g_mem, wk, wv):
    b, m, d = mem.shape
    depth = wk.shape[0]
    per_layer = lambda l, i: (l, 0, 0)
    return pl.pallas_call(
        _memkv_body,
        grid=(depth, b),
        in_specs=[pl.BlockSpec((None, m, d), lambda l, i: (i, 0, 0)),
                  pl.BlockSpec((None, 1, d), per_layer),
                  pl.BlockSpec((None, d, d), per_layer),
                  pl.BlockSpec((None, d, d), per_layer)],
        out_specs=[pl.BlockSpec((None, None, d, m), lambda l, i: (l, i, 0, 0)),
                   pl.BlockSpec((None, None, m, d), lambda l, i: (l, i, 0, 0))],
        out_shape=[jax.ShapeDtypeStruct((depth, b, d, m), BF16), jax.ShapeDtypeStruct((depth, b, m, d), BF16)],
        compiler_params=_params("arbitrary", "arbitrary"),
        name="memkv",
    )(mem, g_mem, wk, wv)


def _xattn_body(x_ref, g_ref, wq_ref, kt_ref, v_ref, wo_ref, o_ref):
    tm = x_ref.shape[0]
    for r0 in range(0, tm, tm // ROW_CHUNKS):
        rows = slice(r0, r0 + tm // ROW_CHUNKS)
        x = x_ref[rows, :]
        h = _rms(x, g_ref[...]).astype(BF16)
        q = _dot(h, wq_ref[...])
        dh = q.shape[1] // N_XHEADS
        heads = []
        for hd in range(N_XHEADS):
            sl = slice(hd * dh, (hd + 1) * dh)
            s = _dot(q[:, sl].astype(BF16), kt_ref[sl, :]) * (dh ** -0.5)
            e = jnp.exp(s - jnp.max(s, axis=-1, keepdims=True))
            p = e / jnp.sum(e, axis=-1, keepdims=True)
            heads.append(_dot(p.astype(BF16), v_ref[:, sl]))
        o = jnp.concatenate(heads, axis=1).astype(BF16)
        o_ref[rows, :] = x + _dot(o, wo_ref[...])


def _xattn(x, g, wq, kt, v, wo, *, layer, seq, tm):
    n, d = x.shape
    m = v.shape[2]
    assert seq % tm == 0 and d % N_XHEADS == 0
    tiles_per_seq = seq // tm
    tile = pl.BlockSpec((tm, d), lambda i: (i, 0))
    return pl.pallas_call(
        _xattn_body,
        grid=(n // tm,),
        in_specs=[tile, _resident(g.shape), _resident(wq.shape),
                  pl.BlockSpec((None, None, d, m), lambda i: (layer, i // tiles_per_seq, 0, 0)),
                  pl.BlockSpec((None, None, m, d), lambda i: (layer, i // tiles_per_seq, 0, 0)),
                  _resident(wo.shape)],
        out_specs=tile,
        out_shape=jax.ShapeDtypeStruct((n, d), F32),
        compiler_params=_params("arbitrary"),
        name="xattn",
    )(x, g, wq, kt, v, wo)


def _swiglu(h, wg, wu, wd):
    g = _dot(h, wg)
    return _dot((g * _sigmoid(g) * _dot(h, wu)).astype(BF16), wd)


def _ffn_body(x_ref, g_ref, wg_ref, wu_ref, wd_ref, o_ref):
    x = x_ref[...]
    h = _rms(x, g_ref[...]).astype(BF16)
    o_ref[...] = x + _swiglu(h, wg_ref[...], wu_ref[...], wd_ref[...])


def _ffn(x, g, wg, wu, wd, *, tm):
    n, d = x.shape
    tile = pl.BlockSpec((tm, d), lambda i: (i, 0))
    return pl.pallas_call(
        _ffn_body,
        grid=(n // tm,),
        in_specs=[tile, _resident(g.shape), _resident(wg.shape), _resident(wu.shape), _resident(wd.shape)],
        out_specs=tile,
        out_shape=jax.ShapeDtypeStruct((n, d), F32),
        compiler_params=_params("arbitrary"),
        name="ffn",
    )(x, g, wg, wu, wd)


def _split_bf16(v):
    hi = v.astype(BF16)
    return hi, (v - hi.astype(F32)).astype(BF16)


def _router_body(x_ref, g_ref, w2_ref, h_ref, idx_ref, wt_ref, cnt_ref, base_ref, *, n_exp):
    tm = x_ref.shape[0]

    @pl.when(pl.program_id(0) == 0)
    def _():
        base_ref[...] = jnp.zeros(base_ref.shape, F32)

    cm = tm // ROUTER_ROW_CHUNKS
    lane_c = lax.broadcasted_iota(jnp.int32, (cm, LANES), 1)
    lane_f = lane_c.astype(F32)
    picks = []
    for r0 in range(0, tm, cm):
        h = _rms(x_ref[r0:r0 + cm, :], g_ref[...])
        h_ref[r0:r0 + cm, :] = h
        h_hi, h_lo = _split_bf16(h)
        part = _dot(h_hi, w2_ref[...]) + _dot(h_lo, w2_ref[...])
        logits = jnp.where(lane_c < n_exp, part + pltpu.roll(part, LANES - n_exp, axis=1), -jnp.inf)
        c1 = jnp.max(logits, axis=-1, keepdims=True)
        j1 = jnp.min(jnp.where(logits == c1, lane_f, float(LANES)), axis=-1, keepdims=True).astype(jnp.int32)
        rest = jnp.where(lane_c == j1, -jnp.inf, logits)
        c2 = jnp.max(rest, axis=-1, keepdims=True)
        j2 = jnp.min(jnp.where(rest == c2, lane_f, float(LANES)), axis=-1, keepdims=True).astype(jnp.int32)
        picks.append((c1, j1, c2, j2))
    m1, i1, m2, i2 = (jnp.concatenate(col, axis=0) for col in zip(*picks))
    e2 = jnp.exp(m2 - m1)
    den = 1.0 + e2
    lane = lax.broadcasted_iota(jnp.int32, (tm, LANES), 1)
    wt_ref[...] = jnp.where(lane == 0, 1.0 / den, jnp.where(lane == 1, e2 / den, 0.0))

    oh1, oh2 = lane == i1, lane == i2
    hits = jnp.logical_or(oh1, oh2)
    earlier = lax.broadcasted_iota(jnp.int32, (tm, tm), 1) < lax.broadcasted_iota(jnp.int32, (tm, tm), 0)
    prefix = _dot(earlier.astype(BF16), hits.astype(BF16)) + base_ref[...]
    r1 = jnp.sum(jnp.where(oh1, prefix, 0.0), axis=-1, keepdims=True).astype(jnp.int32)
    r2 = jnp.sum(jnp.where(oh2, prefix, 0.0), axis=-1, keepdims=True).astype(jnp.int32)
    idx = jnp.where(lane == 0, i1, jnp.where(lane == 1, i2, jnp.where(lane == 2, r1, jnp.where(lane == 3, r2, 0))))
    idx_ref[...] = idx.T[0:SUBLANES, :]
    base_ref[...] += jnp.sum(hits.astype(F32), axis=0, keepdims=True)
    cnt_ref[...] = jnp.broadcast_to(base_ref[...], cnt_ref.shape).astype(jnp.int32)


def _router(x, g, w_router, *, tm):
    n, d = x.shape
    n_exp = w_router.shape[1]
    assert 2 * n_exp <= LANES
    w2 = jnp.pad(jnp.concatenate(_split_bf16(w_router), axis=1), ((0, 0), (0, LANES - 2 * n_exp)))
    tile = lambda w: pl.BlockSpec((tm, w), lambda i: (i, 0))
    return pl.pallas_call(
        functools.partial(_router_body, n_exp=n_exp),
        grid=(n // tm,),
        in_specs=[tile(d), _resident(g.shape), _resident(w2.shape)],
        out_specs=[tile(d), pl.BlockSpec((SUBLANES, tm), lambda i: (i, 0)), tile(LANES),
                   pl.BlockSpec((SUBLANES, LANES), lambda i: (0, 0))],
        out_shape=[jax.ShapeDtypeStruct((n, d), F32), jax.ShapeDtypeStruct((n // tm * SUBLANES, tm), jnp.int32),
                   jax.ShapeDtypeStruct((n, LANES), F32), jax.ShapeDtypeStruct((SUBLANES, LANES), jnp.int32)],
        scratch_shapes=[pltpu.VMEM((1, LANES), F32)],
        compiler_params=_params("arbitrary"),
        name="router",
    )(x, g, w2)


def _to_row_tiles(v, dst_ref):
    c = v.shape[1] // LANES
    for j in range(c):
        dst_ref[pl.ds(j, v.shape[0], stride=c), :] = v[:, j * LANES:(j + 1) * LANES]


def _from_row_tiles(src_ref, rows, c):
    return [src_ref[pl.ds(j, rows, stride=c), :] for j in range(c)]


def _tile_copy(src, src_row8, dst, dst_row8, sem):
    return pltpu.make_async_copy(src.at[pl.ds(pl.multiple_of(src_row8, SUBLANES), SUBLANES)],
                                 dst.at[pl.ds(pl.multiple_of(dst_row8, SUBLANES), SUBLANES)], sem)


def _tiles_wait(hbm, vmem, sem):
    pltpu.make_async_copy(hbm.at[pl.ds(0, vmem.shape[0])], vmem, sem).wait()


def _dispatch_body(ends_ref, cnt_ref, pos_ref, h_ref, xs_hbm, zbuf, stage, zsem, sems, *, tmb):
    i, n = pl.program_id(0), pl.num_programs(0)
    tm = h_ref.shape[0]
    slot = lax.rem(i, 2)

    @pl.when(i == 0)
    def _():
        n_exp = ends_ref.shape[0]
        zbuf[...] = jnp.zeros(zbuf.shape, zbuf.dtype)
        for e in range(2 * n_exp):
            first = ends_ref[e] - tmb if e < n_exp else ends_ref[n_exp - 1] + (e - n_exp) * tmb
            live = cnt_ref[e] > 0 if e < n_exp else first * SUBLANES < xs_hbm.shape[0]

            @pl.when(live)
            def _():
                start = pl.multiple_of(first * SUBLANES, tmb * SUBLANES)
                fill = pltpu.make_async_copy(zbuf, xs_hbm.at[pl.ds(start, tmb * SUBLANES)], zsem)
                fill.start()
                fill.wait()

    _to_row_tiles(h_ref[...], stage.at[slot])

    def scatter(g, c):
        for j in range(SUBLANES):
            for k in range(TOP_K):
                _tile_copy(stage.at[slot], (g * SUBLANES + j) * SUBLANES,
                           xs_hbm, pos_ref[0, 0, k * tm + g * SUBLANES + j],
                           sems.at[slot]).start(priority=(j * TOP_K + k) % N_DMA_QUEUES)
        return c
    lax.fori_loop(0, tm // SUBLANES, scatter, 0)

    def drain(s):
        for _ in range(TOP_K):
            _tiles_wait(xs_hbm, stage.at[s], sems.at[s])

    @pl.when(i > 0)
    def _():
        drain(1 - slot)

    @pl.when(i == n - 1)
    def _():
        drain(slot)


def _dispatch(ends, counts, pos8, h, *, n_rows, tm, tmb):
    n, d = h.shape
    assert d == SUBLANES * LANES and tm % SUBLANES == 0
    return pl.pallas_call(
        functools.partial(_dispatch_body, tmb=tmb),
        grid_spec=pltpu.PrefetchScalarGridSpec(
            num_scalar_prefetch=2,
            grid=(n // tm,),
            in_specs=[pl.BlockSpec((1, 1, TOP_K * tm), lambda i, *_: (i, 0, 0), memory_space=pltpu.SMEM),
                      pl.BlockSpec((tm, d), lambda i, *_: (i, 0))],
            out_specs=pl.BlockSpec(memory_space=pl.ANY),
            scratch_shapes=[pltpu.VMEM((tmb * SUBLANES, LANES), F32), pltpu.VMEM((2, tm * SUBLANES, LANES), F32),
                            pltpu.SemaphoreType.DMA(()), pltpu.SemaphoreType.DMA((2,))],
        ),
        out_shape=jax.ShapeDtypeStruct((n_rows * SUBLANES, LANES), F32),
        compiler_params=_params("arbitrary"),
        name="dispatch",
    )(ends, counts, pos8, h)


def _expert_body(te_ref, nv_ref, tr_ref, xs_ref, wg_hbm, wu_hbm, wd_hbm, y_ref, hb, acc, wgb, wub, wdb, sems, *, fc):
    i = pl.program_id(0)
    tmb, d = hb.shape
    nf = wg_hbm.shape[2] // fc
    n_valid = nv_ref[0]

    def weight_copies(tile, f, slot):
        e = te_ref[tile]
        cols = pl.ds(pl.multiple_of(f * fc, fc), fc)
        return (pltpu.make_async_copy(wg_hbm.at[e, :, cols], wgb.at[slot], sems.at[0, slot]),
                pltpu.make_async_copy(wu_hbm.at[e, :, cols], wub.at[slot], sems.at[1, slot]),
                pltpu.make_async_copy(wd_hbm.at[e, cols, :], wdb.at[slot], sems.at[2, slot]))

    @pl.when(i >= n_valid)
    def _():
        y_ref[...] = jnp.zeros(y_ref.shape, F32)

    def run_chunk(f, first, last):
        slot = lax.rem(i * nf + f, 2)
        next_tile = i + 1 if last else i

        @pl.when(next_tile < n_valid)
        def _():
            for cp in weight_copies(next_tile, 0 if last else f + 1, 1 - slot):
                cp.start(priority=WEIGHT_DMA_QUEUE)

        for cp in weight_copies(i, f, slot):
            cp.wait()

        def accumulate(m):
            if first:
                h = jnp.concatenate([col.astype(BF16) for col in _from_row_tiles(xs_ref, m, d // LANES)], axis=1)
                hb[0:m, :] = h
            else:
                h = hb[0:m, :]
            contrib = _swiglu(h, wgb[slot].astype(BF16), wub[slot].astype(BF16), wdb[slot].astype(BF16))
            total = contrib if first else acc[0:m, :] + contrib
            if last:
                _to_row_tiles(total, y_ref)
                if m < tmb:
                    y_ref[m * (d // LANES):, :] = jnp.zeros(((tmb - m) * (d // LANES), LANES), F32)
            else:
                acc[0:m, :] = total

        @pl.when(tr_ref[i] > tmb // 2)
        def _():
            accumulate(tmb)

        @pl.when(tr_ref[i] <= tmb // 2)
        def _():
            accumulate(tmb // 2)

    @pl.when(i < n_valid)
    def _():
        @pl.when(i == 0)
        def _():
            for cp in weight_copies(0, 0, 0):
                cp.start(priority=WEIGHT_DMA_QUEUE)

        run_chunk(0, True, False)

        def middle(f, carry):
            run_chunk(f, False, False)
            return carry
        lax.fori_loop(1, nf - 1, middle, 0)
        run_chunk(nf - 1, False, True)


def _experts(tile_expert, n_valid, tile_rows, xs, wg, wu, wd, *, tmb, fc):
    n_tiles = tile_expert.shape[0]
    _, d, ff = wg.shape
    c = d // LANES
    assert ff % fc == 0 and ff // fc >= 2 and xs.shape == (n_tiles * tmb * c, LANES)
    return pl.pallas_call(
        functools.partial(_expert_body, fc=fc),
        grid_spec=pltpu.PrefetchScalarGridSpec(
            num_scalar_prefetch=3,
            grid=(n_tiles,),
            in_specs=[pl.BlockSpec((tmb * c, LANES), lambda i, te, nv, tr: (jnp.minimum(i, nv[0] - 1), 0)),
                      pl.BlockSpec(memory_space=pl.ANY), pl.BlockSpec(memory_space=pl.ANY),
                      pl.BlockSpec(memory_space=pl.ANY)],
            out_specs=pl.BlockSpec((tmb * c, LANES), lambda i, te, nv, tr: (i, 0)),
            scratch_shapes=[pltpu.VMEM((tmb, d), BF16), pltpu.VMEM((tmb, d), F32),
                            pltpu.VMEM((2, d, fc), wg.dtype), pltpu.VMEM((2, d, fc), wu.dtype),
                            pltpu.VMEM((2, fc, d), wd.dtype), pltpu.SemaphoreType.DMA((3, 2))],
        ),
        out_shape=jax.ShapeDtypeStruct((n_tiles * tmb * c, LANES), F32),
        compiler_params=_params("arbitrary"),
        name="experts",
    )(tile_expert, n_valid, tile_rows, xs, wg, wu, wd)


def _combine_body(pos_ref, pos_next_ref, x_ref, wt_ref, g_ref, y_hbm, o_ref, yrows, sems, *, final_norm):
    i, n = pl.program_id(0), pl.num_programs(0)
    tm, d = x_ref.shape
    slot = lax.rem(i, 2)

    def gather(p_ref, s):
        def body(g, c):
            for j in range(SUBLANES):
                for k in range(TOP_K):
                    _tile_copy(y_hbm, p_ref[0, 0, k * tm + g * SUBLANES + j],
                               yrows.at[s, k], (g * SUBLANES + j) * SUBLANES,
                               sems.at[s]).start(priority=(j * TOP_K + k) % N_DMA_QUEUES)
            return c
        lax.fori_loop(0, tm // SUBLANES, body, 0)

    @pl.when(i == 0)
    def _():
        gather(pos_ref, 0)

    @pl.when(i + 1 < n)
    def _():
        gather(pos_next_ref, 1 - slot)

    for k in range(TOP_K):
        _tiles_wait(y_hbm, yrows.at[slot, k], sems.at[slot])
    ys = [_from_row_tiles(yrows.at[slot, k], tm, d // LANES) for k in range(TOP_K)]
    cols = []
    for j in range(d // LANES):
        moe = wt_ref[:, 0:1] * ys[0][j]
        for k in range(1, TOP_K):
            moe = moe + wt_ref[:, k:k + 1] * ys[k][j]
        cols.append(moe)
    out = x_ref[...] + jnp.concatenate(cols, axis=1)
    o_ref[...] = _rms(out, g_ref[...]) if final_norm else out


def _combine(pos8, x, wt, g, y, *, tm, final_norm):
    n, d = x.shape
    assert d == SUBLANES * LANES and tm % SUBLANES == 0
    last = n // tm - 1
    tile = pl.BlockSpec((tm, d), lambda i: (i, 0))
    return pl.pallas_call(
        functools.partial(_combine_body, final_norm=final_norm),
        grid=(n // tm,),
        in_specs=[pl.BlockSpec((1, 1, TOP_K * tm), lambda i: (i, 0, 0), memory_space=pltpu.SMEM),
                  pl.BlockSpec((1, 1, TOP_K * tm), lambda i: (jnp.minimum(i + 1, last), 0, 0),
                               memory_space=pltpu.SMEM),
                  tile, pl.BlockSpec((tm, LANES), lambda i: (i, 0)), _resident(g.shape),
                  pl.BlockSpec(memory_space=pl.ANY)],
        out_specs=tile,
        out_shape=jax.ShapeDtypeStruct((n, d), F32),
        scratch_shapes=[pltpu.VMEM((2, TOP_K, tm * SUBLANES, LANES), F32), pltpu.SemaphoreType.DMA((2,))],
        compiler_params=_params("arbitrary"),
        name="combine",
    )(pos8, pos8, x, wt, g, y)


def _routing_tables(idx, cnt, n, n_exp, tmb):
    tm = idx.shape[1]
    idx = idx.reshape(n // tm, SUBLANES, tm)
    experts = jnp.arange(n_exp, dtype=jnp.int32)
    counts = cnt[0, :n_exp]
    padded = ((counts + (tmb - 1)) // tmb) * tmb
    ends = jnp.sum(jnp.where(experts[None, :] <= experts[:, None], padded[None, :], 0), axis=1)
    starts = ends - padded
    chosen = idx[:, 0:TOP_K, :]
    first = sum(jnp.where(chosen == e, starts[e], 0) for e in range(n_exp))
    pos8 = ((first + idx[:, TOP_K:2 * TOP_K, :]) * SUBLANES).reshape(n // tm, 1, TOP_K * tm)
    n_tiles = (n * TOP_K) // tmb + n_exp
    tile_start = jnp.arange(n_tiles, dtype=jnp.int32) * tmb
    tile_expert = jnp.minimum(jnp.sum((tile_start[:, None] >= ends[None, :]).astype(jnp.int32), axis=1), n_exp - 1)
    tile_rows = jnp.clip(sum(jnp.where(tile_expert == e, starts[e] + counts[e], 0) for e in range(n_exp)) - tile_start,
                         0, tmb)
    return pos8, ends, counts, tile_expert, ends[n_exp - 1:] // tmb, tile_rows, n_tiles


def _moe(x, g, w_router, wg, wu, wd, g_out, *, final_norm):
    h, idx, wt, cnt = _router(x, g, w_router, tm=MOE_ROW_TILE)
    pos8, ends, counts, tile_expert, n_valid, tile_rows, n_tiles = _routing_tables(idx, cnt, x.shape[0], wg.shape[0],
                                                                        EXPERT_ROW_TILE)
    xs = _dispatch(ends, counts, pos8, h, n_rows=n_tiles * EXPERT_ROW_TILE, tm=MOE_ROW_TILE, tmb=EXPERT_ROW_TILE)
    y = _experts(tile_expert, n_valid, tile_rows, xs, wg, wu, wd, tmb=EXPERT_ROW_TILE, fc=EXPERT_FF_TILE)
    return _combine(pos8, x, wt, g_out, y, tm=MOE_ROW_TILE, final_norm=final_norm)


def _final_norm_body(x_ref, g_ref, o_ref):
    o_ref[...] = _rms(x_ref[...], g_ref[...])


def _final_norm(x, g, *, tm):
    n, d = x.shape
    tile = pl.BlockSpec((tm, d), lambda i: (i, 0))
    return pl.pallas_call(
        _final_norm_body, grid=(n // tm,), in_specs=[tile, _resident(g.shape)], out_specs=tile,
        out_shape=jax.ShapeDtypeStruct((n, d), F32), compiler_params=_params("arbitrary"), name="final_norm",
    )(x, g)


def kernel(x, mem, g_mix, w_in, w_pool_group, pool_scale, w_pool_out, conv_w, w_conv_out, w_mix_out, g_xattn, g_mem, w_xq, w_xk, w_xv, w_xo, g_ffn, w_ff_gate, w_ff_up, w_ff_down, w_router, w_e_gate, w_e_up, w_e_down, g_final):
    b, s, d = x.shape
    depth = g_mix.shape[0]
    bf = lambda w: w.astype(BF16)
    row = lambda v: v.reshape(1, -1)

    kt, v = _memkv(mem, g_mem.reshape(depth, 1, d), bf(w_xk), bf(w_xv))
    xf = x.reshape(b * s, d)
    for l in range(depth):
        xf = _mixer(xf, row(g_mix[l]), bf(w_in[l]), bf(w_pool_group[l]), row(pool_scale[l]), bf(w_pool_out[l]),
                    conv_w[l], bf(w_conv_out[l]), bf(w_mix_out[l]), seq=s, tm=MIXER_ROW_TILE)
        xf = _xattn(xf, row(g_xattn[l]), bf(w_xq[l]), kt, v, bf(w_xo[l]), layer=l, seq=s, tm=MIXER_ROW_TILE)
        last = l == depth - 1
        i = l // 2
        if l % 2 == 0:
            xf = _ffn(xf, row(g_ffn[l]), bf(w_ff_gate[i]), bf(w_ff_up[i]), bf(w_ff_down[i]), tm=ROW_TILE)
            if last:
                xf = _final_norm(xf, row(g_final), tm=ROW_TILE)
        else:
            xf = _moe(xf, row(g_ffn[l]), w_router[i], w_e_gate[i], w_e_up[i], w_e_down[i],
                      row(g_final), final_norm=last)
    return xf.reshape(b, s, d)
```

```python
import functools

import jax
import jax.numpy as jnp
from jax import lax
from jax.experimental import pallas as pl
from jax.experimental.pallas import tpu as pltpu

F32 = jnp.float32
BF16 = jnp.bfloat16

EPS = 1e-6
POOL_WINDOWS = (2, 4, 8, 16)
CONV_K = 3
N_XHEADS = 4
TOP_K = 2

LANES = 128
SUBLANES = 8
VMEM_LIMIT_BYTES = 56 * 1024 * 1024
N_DMA_QUEUES = 2

POOL_HALO = 16
CONV_HALO = 8

ROW_TILE = 512
MIXER_ROW_TILE = 1024
ROW_CHUNKS = 2
MIXER_ROW_CHUNKS = 1
ROUTER_ROW_TILE = 1024
EXPERT_ROW_TILE = 1024
EXPERT_FF_TILE = 512


def _dot(a, b):
    return jnp.dot(a, b, preferred_element_type=F32)


def _sigmoid(z):
    return 0.5 * jnp.tanh(0.5 * z) + 0.5


def _rms(xf, g):
    ms = jnp.mean(xf * xf, axis=-1, keepdims=True)
    return (xf * lax.rsqrt(ms + EPS)) * g


def _resident(shape):
    nd = len(shape)
    return pl.BlockSpec(shape, lambda *_: (0,) * nd, pipeline_mode=pl.Buffered(1))


def _params(*sem):
    return pltpu.CompilerParams(dimension_semantics=sem, vmem_limit_bytes=VMEM_LIMIT_BYTES)


def _mixer_body(x_ref, g_ref, win_ref, wgrp_ref, ps_ref, wpo_ref, cw_ref, wco_ref, wmo_ref, o_ref,
                pool_buf, conv_buf, *, tm, seq):
    n_groups, _, gw = wgrp_ref.shape
    pw = n_groups * gw
    cwid = cw_ref.shape[1]
    d = x_ref.shape[1]
    o_c, o_b, o_u, o_g = pw, pw + cwid, pw + 2 * cwid, pw + 3 * cwid

    pos0 = lax.rem(pl.program_id(0) * tm, seq)

    @pl.when(pos0 == 0)
    def _():
        pool_buf[:, 0:POOL_HALO, :] = jnp.zeros((n_groups, POOL_HALO, gw), F32)
        conv_buf[:, 0:CONV_HALO, :] = jnp.zeros((conv_buf.shape[0], CONV_HALO, LANES), F32)

    @pl.when(pos0 != 0)
    def _():
        pool_buf[:, 0:POOL_HALO, :] = pool_buf[:, tm:tm + POOL_HALO, :]
        conv_buf[:, 0:CONV_HALO, :] = conv_buf[:, tm:tm + CONV_HALO, :]

    cm = tm // MIXER_ROW_CHUNKS
    for r0 in range(0, tm, cm):
        x = x_ref[r0:r0 + cm, :]
        h = _rms(x, g_ref[...]).astype(BF16)

        u_pool = _dot(h, win_ref[:, 0:o_c])
        ys = []
        for g, w in enumerate(POOL_WINDOWS):
            ug = u_pool[:, g * gw:(g + 1) * gw]
            lo = POOL_HALO + r0
            pool_buf[g, lo:lo + cm, :] = ug
            acc = ug
            for k in range(1, w):
                acc = acc + pool_buf[g, lo - k:lo - k + cm, :]
            if r0 == 0:
                pos1 = lax.broadcasted_iota(jnp.int32, (POOL_HALO, gw), 0) + (pos0 + 1)
                head = acc[0:POOL_HALO] / jnp.minimum(pos1, w).astype(F32)
                mean = jnp.concatenate([head, acc[POOL_HALO:] * (1.0 / w)], axis=0)
            else:
                mean = acc * (1.0 / w)
            ys.append(_dot((mean - ug).astype(BF16), wgrp_ref[g]))
        y = jnp.concatenate(ys, axis=1) * ps_ref[...]
        y_pool = _dot(y.astype(BF16), wpo_ref[...])

        c_gate = _dot(h, win_ref[:, o_c:o_b])
        u_conv = _dot(h, win_ref[:, o_u:o_g])
        zc = c_gate * u_conv
        ycs = []
        for j in range(cwid // LANES):
            sl = slice(j * LANES, (j + 1) * LANES)
            zj = zc[:, sl]
            lo = CONV_HALO + r0
            conv_buf[j, lo:lo + cm, :] = zj
            acc = cw_ref[0:1, sl] * conv_buf[j, lo - (CONV_K - 1):lo - (CONV_K - 1) + cm, :]
            for k in range(1, CONV_K - 1):
                off = lo - (CONV_K - 1) + k
                acc = acc + cw_ref[k:k + 1, sl] * conv_buf[j, off:off + cm, :]
            ycs.append(acc + cw_ref[CONV_K - 1:CONV_K, sl] * zj)
        b_gate = _dot(h, win_ref[:, o_b:o_u])
        y_conv = _dot((b_gate * jnp.concatenate(ycs, axis=1)).astype(BF16), wco_ref[...])

        merged = _sigmoid(_dot(h, win_ref[:, o_g:o_g + d])) * y_pool
        merged = merged + _sigmoid(_dot(h, win_ref[:, o_g + d:o_g + 2 * d])) * y_conv
        o_ref[r0:r0 + cm, :] = x + _dot(merged.astype(BF16), wmo_ref[...])


def _mixer(x, g, win, wgrp, ps, wpo, cw, wco, wmo, *, seq, tm):
    n, d = x.shape
    n_groups, _, gw = wgrp.shape
    assert seq % tm == 0 and n % tm == 0 and gw == LANES and cw.shape[1] % LANES == 0
    assert (tm // MIXER_ROW_CHUNKS) % SUBLANES == 0 and tm // MIXER_ROW_CHUNKS >= POOL_HALO
    assert len(POOL_WINDOWS) == n_groups and max(POOL_WINDOWS) <= POOL_HALO and cw.shape[0] == CONV_K
    assert all(w & (w - 1) == 0 for w in POOL_WINDOWS)
    tile = pl.BlockSpec((tm, d), lambda i: (i, 0))
    return pl.pallas_call(
        functools.partial(_mixer_body, tm=tm, seq=seq),
        grid=(n // tm,),
        in_specs=[tile, _resident(g.shape), _resident(win.shape), _resident(wgrp.shape), _resident(ps.shape),
                  _resident(wpo.shape), _resident(cw.shape), _resident(wco.shape), _resident(wmo.shape)],
        out_specs=tile,
        out_shape=jax.ShapeDtypeStruct((n, d), F32),
        scratch_shapes=[pltpu.VMEM((n_groups, tm + POOL_HALO, gw), F32),
                        pltpu.VMEM((cw.shape[1] // LANES, tm + CONV_HALO, LANES), F32)],
        compiler_params=_params("arbitrary"),
        name="mixer",
    )(x, g, win, wgrp, ps, wpo, cw, wco, wmo)


def _memkv_body(mem_ref, g_ref, wk_ref, wv_ref, kt_ref, v_ref):
    mn = _rms(mem_ref[...], g_ref[...]).astype(BF16)
    kt_ref[...] = _dot(mn, wk_ref[...]).T.astype(BF16)
    v_ref[...] = _dot(mn, wv_ref[...]).astype(BF16)


def _memkv(mem, g_mem, wk, wv):
    b, m, d = mem.shape
    depth = wk.shape[0]
    per_layer = lambda l, i: (l, 0, 0)
    return pl.pallas_call(
        _memkv_body,
        grid=(depth, b),
        in_specs=[pl.BlockSpec((None, m, d), lambda l, i: (i, 0, 0)),
                  pl.BlockSpec((None, 1, d), per_layer),
                  pl.BlockSpec((None, d, d), per_layer),
                  pl.BlockSpec((None, d, d), per_layer)],
        out_specs=[pl.BlockSpec((None, None, d, m), lambda l, i: (l, i, 0, 0)),
                   pl.BlockSpec((None, None, m, d), lambda l, i: (l, i, 0, 0))],
        out_shape=[jax.ShapeDtypeStruct((depth, b, d, m), BF16), jax.ShapeDtypeStruct((depth, b, m, d), BF16)],
        compiler_params=_params("arbitrary", "arbitrary"),
        name="memkv",
    )(mem, g_mem, wk, wv)


def _xattn_body(x_ref, g_ref, wq_ref, kt_ref, v_ref, wo_ref, o_ref):
    tm = x_ref.shape[0]
    for r0 in range(0, tm, tm // ROW_CHUNKS):
        rows = slice(r0, r0 + tm // ROW_CHUNKS)
        x = x_ref[rows, :]
        h = _rms(x, g_ref[...]).astype(BF16)
        q = _dot(h, wq_ref[...])
        dh = q.shape[1] // N_XHEADS
        heads = []
        for hd in range(N_XHEADS):
            sl = slice(hd * dh, (hd + 1) * dh)
            s = _dot(q[:, sl].astype(BF16), kt_ref[sl, :]) * (dh ** -0.5)
            e = jnp.exp(s - jnp.max(s, axis=-1, keepdims=True))
            p = e / jnp.sum(e, axis=-1, keepdims=True)
            heads.append(_dot(p.astype(BF16), v_ref[:, sl]))
        o = jnp.concatenate(heads, axis=1).astype(BF16)
        o_ref[rows, :] = x + _dot(o, wo_ref[...])


def _xattn(x, g, wq, kt, v, wo, *, layer, seq, tm):
    n, d = x.shape
    m = v.shape[2]
    assert seq % tm == 0 and d % N_XHEADS == 0
    tiles_per_seq = seq // tm
    tile = pl.BlockSpec((tm, d), lambda i: (i, 0))
    return pl.pallas_call(
        _xattn_body,
        grid=(n // tm,),
        in_specs=[tile, _resident(g.shape), _resident(wq.shape),
                  pl.BlockSpec((None, None, d, m), lambda i: (layer, i // tiles_per_seq, 0, 0)),
                  pl.BlockSpec((None, None, m, d), lambda i: (layer, i // tiles_per_seq, 0, 0)),
                  _resident(wo.shape)],
        out_specs=tile,
        out_shape=jax.ShapeDtypeStruct((n, d), F32),
        compiler_params=_params("arbitrary"),
        name="xattn",
    )(x, g, wq, kt, v, wo)


def _swiglu(h, wg, wu, wd):
    g = _dot(h, wg)
    return _dot((g * _sigmoid(g) * _dot(h, wu)).astype(BF16), wd)


def _ffn_body(x_ref, g_ref, wg_ref, wu_ref, wd_ref, o_ref):
    tm = x_ref.shape[0]
    for r0 in range(0, tm, tm // ROW_CHUNKS):
        rows = slice(r0, r0 + tm // ROW_CHUNKS)
        x = x_ref[rows, :]
        h = _rms(x, g_ref[...]).astype(BF16)
        o_ref[rows, :] = x + _swiglu(h, wg_ref[...], wu_ref[...], wd_ref[...])


def _ffn(x, g, wg, wu, wd, *, tm):
    n, d = x.shape
    tile = pl.BlockSpec((tm, d), lambda i: (i, 0))
    return pl.pallas_call(
        _ffn_body,
        grid=(n // tm,),
        in_specs=[tile, _resident(g.shape), _resident(wg.shape), _resident(wu.shape), _resident(wd.shape)],
        out_specs=tile,
        out_shape=jax.ShapeDtypeStruct((n, d), F32),
        compiler_params=_params("arbitrary"),
        name="ffn",
    )(x, g, wg, wu, wd)


def _split_bf16(v):
    hi = v.astype(BF16)
    return hi, (v - hi.astype(F32)).astype(BF16)


def _router_body(x_ref, g_ref, w2_ref, h_ref, idx_ref, wt_ref, cnt_ref, base_ref, *, n_exp):
    tm = x_ref.shape[0]

    @pl.when(pl.program_id(0) == 0)
    def _():
        base_ref[...] = jnp.zeros(base_ref.shape, F32)

    h = _rms(x_ref[...], g_ref[...])
    h_ref[...] = h
    h_hi, h_lo = _split_bf16(h)
    part = _dot(h_hi, w2_ref[...]) + _dot(h_lo, w2_ref[...])
    lane = lax.broadcasted_iota(jnp.int32, (tm, LANES), 1)
    logits = jnp.where(lane < n_exp, part + pltpu.roll(part, LANES - n_exp, axis=1), -jnp.inf)
    lane_f = lane.astype(F32)
    m1 = jnp.max(logits, axis=-1, keepdims=True)
    i1 = jnp.min(jnp.where(logits == m1, lane_f, float(LANES)), axis=-1, keepdims=True).astype(jnp.int32)
    rest = jnp.where(lane == i1, -jnp.inf, logits)
    m2 = jnp.max(rest, axis=-1, keepdims=True)
    i2 = jnp.min(jnp.where(rest == m2, lane_f, float(LANES)), axis=-1, keepdims=True).astype(jnp.int32)
    e2 = jnp.exp(m2 - m1)
    den = 1.0 + e2
    wt_ref[...] = jnp.where(lane == 0, 1.0 / den, jnp.where(lane == 1, e2 / den, 0.0))

    oh1, oh2 = lane == i1, lane == i2
    hits = jnp.logical_or(oh1, oh2)
    earlier = lax.broadcasted_iota(jnp.int32, (tm, tm), 1) < lax.broadcasted_iota(jnp.int32, (tm, tm), 0)
    prefix = _dot(earlier.astype(BF16), hits.astype(BF16)) + base_ref[...]
    r1 = jnp.sum(jnp.where(oh1, prefix, 0.0), axis=-1, keepdims=True).astype(jnp.int32)
    r2 = jnp.sum(jnp.where(oh2, prefix, 0.0), axis=-1, keepdims=True).astype(jnp.int32)
    idx = jnp.where(lane == 0, i1, jnp.where(lane == 1, i2, jnp.where(lane == 2, r1, jnp.where(lane == 3, r2, 0))))
    idx_ref[...] = idx.T[0:SUBLANES, :]
    base_ref[...] += jnp.sum(hits.astype(F32), axis=0, keepdims=True)
    cnt_ref[...] = jnp.broadcast_to(base_ref[...], cnt_ref.shape).astype(jnp.int32)


def _router(x, g, w_router, *, tm):
    n, d = x.shape
    n_exp = w_router.shape[1]
    assert 2 * n_exp <= LANES
    w2 = jnp.pad(jnp.concatenate(_split_bf16(w_router), axis=1), ((0, 0), (0, LANES - 2 * n_exp)))
    tile = lambda w: pl.BlockSpec((tm, w), lambda i: (i, 0))
    return pl.pallas_call(
        functools.partial(_router_body, n_exp=n_exp),
        grid=(n // tm,),
        in_specs=[tile(d), _resident(g.shape), _resident(w2.shape)],
        out_specs=[tile(d), pl.BlockSpec((SUBLANES, tm), lambda i: (i, 0)), tile(LANES),
                   pl.BlockSpec((SUBLANES, LANES), lambda i: (0, 0))],
        out_shape=[jax.ShapeDtypeStruct((n, d), F32), jax.ShapeDtypeStruct((n // tm * SUBLANES, tm), jnp.int32),
                   jax.ShapeDtypeStruct((n, LANES), F32), jax.ShapeDtypeStruct((SUBLANES, LANES), jnp.int32)],
        scratch_shapes=[pltpu.VMEM((1, LANES), F32)],
        compiler_params=_params("arbitrary"),
        name="router",
    )(x, g, w2)


def _to_row_tiles(v, dst_ref):
    c = v.shape[1] // LANES
    for j in range(c):
        dst_ref[pl.ds(j, v.shape[0], stride=c), :] = v[:, j * LANES:(j + 1) * LANES]


def _from_row_tiles(src_ref, rows, c):
    return [src_ref[pl.ds(j, rows, stride=c), :] for j in range(c)]


def _tile_copy(src, src_row8, dst, dst_row8, sem):
    return pltpu.make_async_copy(src.at[pl.ds(pl.multiple_of(src_row8, SUBLANES), SUBLANES)],
                                 dst.at[pl.ds(pl.multiple_of(dst_row8, SUBLANES), SUBLANES)], sem)


def _tiles_wait(hbm, vmem, sem):
    pltpu.make_async_copy(hbm.at[pl.ds(0, vmem.shape[0])], vmem, sem).wait()


def _dispatch_body(ends_ref, cnt_ref, pos_ref, h_ref, xs_hbm, zbuf, stage, zsem, sems, *, tmb):
    i, n = pl.program_id(0), pl.num_programs(0)
    tm = h_ref.shape[0]
    slot = lax.rem(i, 2)

    @pl.when(i == 0)
    def _():
        n_exp = ends_ref.shape[0]
        zbuf[...] = jnp.zeros(zbuf.shape, zbuf.dtype)
        for e in range(2 * n_exp):
            first = ends_ref[e] - tmb if e < n_exp else ends_ref[n_exp - 1] + (e - n_exp) * tmb
            live = cnt_ref[e] > 0 if e < n_exp else first * SUBLANES < xs_hbm.shape[0]

            @pl.when(live)
            def _():
                start = pl.multiple_of(first * SUBLANES, tmb * SUBLANES)
                fill = pltpu.make_async_copy(zbuf, xs_hbm.at[pl.ds(start, tmb * SUBLANES)], zsem)
                fill.start()
                fill.wait()

    _to_row_tiles(h_ref[...], stage.at[slot])

    def scatter(g, c):
        for j in range(SUBLANES):
            for k in range(TOP_K):
                _tile_copy(stage.at[slot], (g * SUBLANES + j) * SUBLANES,
                           xs_hbm, pos_ref[0, 0, k * tm + g * SUBLANES + j],
                           sems.at[slot]).start(priority=(j * TOP_K + k) % N_DMA_QUEUES)
        return c
    lax.fori_loop(0, tm // SUBLANES, scatter, 0)

    def drain(s):
        for _ in range(TOP_K):
            _tiles_wait(xs_hbm, stage.at[s], sems.at[s])

    @pl.when(i > 0)
    def _():
        drain(1 - slot)

    @pl.when(i == n - 1)
    def _():
        drain(slot)


def _dispatch(ends, counts, pos8, h, *, n_rows, tm, tmb):
    n, d = h.shape
    assert d == SUBLANES * LANES and tm % SUBLANES == 0
    return pl.pallas_call(
        functools.partial(_dispatch_body, tmb=tmb),
        grid_spec=pltpu.PrefetchScalarGridSpec(
            num_scalar_prefetch=2,
            grid=(n // tm,),
            in_specs=[pl.BlockSpec((1, 1, TOP_K * tm), lambda i, *_: (i, 0, 0), memory_space=pltpu.SMEM),
                      pl.BlockSpec((tm, d), lambda i, *_: (i, 0))],
            out_specs=pl.BlockSpec(memory_space=pl.ANY),
            scratch_shapes=[pltpu.VMEM((tmb * SUBLANES, LANES), F32), pltpu.VMEM((2, tm * SUBLANES, LANES), F32),
                            pltpu.SemaphoreType.DMA(()), pltpu.SemaphoreType.DMA((2,))],
        ),
        out_shape=jax.ShapeDtypeStruct((n_rows * SUBLANES, LANES), F32),
        compiler_params=_params("arbitrary"),
        name="dispatch",
    )(ends, counts, pos8, h)


def _expert_body(te_ref, nv_ref, tr_ref, xs_ref, wg_hbm, wu_hbm, wd_hbm, y_ref, hb, acc, wgb, wub, wdb, sems, *, fc):
    i = pl.program_id(0)
    tmb, d = hb.shape
    nf = wg_hbm.shape[2] // fc
    n_valid = nv_ref[0]

    def weight_copies(tile, f, slot):
        e = te_ref[tile]
        cols = pl.ds(pl.multiple_of(f * fc, fc), fc)
        return (pltpu.make_async_copy(wg_hbm.at[e, :, cols], wgb.at[slot], sems.at[0, slot]),
                pltpu.make_async_copy(wu_hbm.at[e, :, cols], wub.at[slot], sems.at[1, slot]),
                pltpu.make_async_copy(wd_hbm.at[e, cols, :], wdb.at[slot], sems.at[2, slot]))

    @pl.when(i >= n_valid)
    def _():
        y_ref[...] = jnp.zeros(y_ref.shape, F32)

    def run_chunk(f, first, last):
        slot = lax.rem(i * nf + f, 2)
        next_tile = i + 1 if last else i

        @pl.when(next_tile < n_valid)
        def _():
            for cp in weight_copies(next_tile, 0 if last else f + 1, 1 - slot):
                cp.start()

        for cp in weight_copies(i, f, slot):
            cp.wait()

        def accumulate(m):
            if first:
                h = jnp.concatenate([col.astype(BF16) for col in _from_row_tiles(xs_ref, m, d // LANES)], axis=1)
                hb[0:m, :] = h
            else:
                h = hb[0:m, :]
            contrib = _swiglu(h, wgb[slot].astype(BF16), wub[slot].astype(BF16), wdb[slot].astype(BF16))
            total = contrib if first else acc[0:m, :] + contrib
            if last:
                _to_row_tiles(total, y_ref)
                if m < tmb:
                    y_ref[m * (d // LANES):, :] = jnp.zeros(((tmb - m) * (d // LANES), LANES), F32)
            else:
                acc[0:m, :] = total

        @pl.when(tr_ref[i] > tmb // 2)
        def _():
            accumulate(tmb)

        @pl.when(tr_ref[i] <= tmb // 2)
        def _():
            accumulate(tmb // 2)

    @pl.when(i < n_valid)
    def _():
        @pl.when(i == 0)
        def _():
            for cp in weight_copies(0, 0, 0):
                cp.start()

        run_chunk(0, True, False)

        def middle(f, carry):
            run_chunk(f, False, False)
            return carry
        lax.fori_loop(1, nf - 1, middle, 0)
        run_chunk(nf - 1, False, True)


def _experts(tile_expert, n_valid, tile_rows, xs, wg, wu, wd, *, tmb, fc):
    n_tiles = tile_expert.shape[0]
    _, d, ff = wg.shape
    c = d // LANES
    assert ff % fc == 0 and ff // fc >= 2 and xs.shape == (n_tiles * tmb * c, LANES)
    return pl.pallas_call(
        functools.partial(_expert_body, fc=fc),
        grid_spec=pltpu.PrefetchScalarGridSpec(
            num_scalar_prefetch=3,
            grid=(n_tiles,),
            in_specs=[pl.BlockSpec((tmb * c, LANES), lambda i, te, nv, tr: (jnp.minimum(i, nv[0] - 1), 0)),
                      pl.BlockSpec(memory_space=pl.ANY), pl.BlockSpec(memory_space=pl.ANY),
                      pl.BlockSpec(memory_space=pl.ANY)],
            out_specs=pl.BlockSpec((tmb * c, LANES), lambda i, te, nv, tr: (i, 0)),
            scratch_shapes=[pltpu.VMEM((tmb, d), BF16), pltpu.VMEM((tmb, d), F32),
                            pltpu.VMEM((2, d, fc), wg.dtype), pltpu.VMEM((2, d, fc), wu.dtype),
                            pltpu.VMEM((2, fc, d), wd.dtype), pltpu.SemaphoreType.DMA((3, 2))],
        ),
        out_shape=jax.ShapeDtypeStruct((n_tiles * tmb * c, LANES), F32),
        compiler_params=_params("arbitrary"),
        name="experts",
    )(tile_expert, n_valid, tile_rows, xs, wg, wu, wd)


def _combine_body(pos_ref, pos_next_ref, x_ref, wt_ref, g_ref, y_hbm, o_ref, yrows, sems, *, final_norm):
    i, n = pl.program_id(0), pl.num_programs(0)
    tm, d = x_ref.shape
    slot = lax.rem(i, 2)

    def gather(p_ref, s):
        def body(g, c):
            for j in range(SUBLANES):
                for k in range(TOP_K):
                    _tile_copy(y_hbm, p_ref[0, 0, k * tm + g * SUBLANES + j],
                               yrows.at[s, k], (g * SUBLANES + j) * SUBLANES,
                               sems.at[s]).start(priority=(j * TOP_K + k) % N_DMA_QUEUES)
            return c
        lax.fori_loop(0, tm // SUBLANES, body, 0)

    @pl.when(i == 0)
    def _():
        gather(pos_ref, 0)

    @pl.when(i + 1 < n)
    def _():
        gather(pos_next_ref, 1 - slot)

    for k in range(TOP_K):
        _tiles_wait(y_hbm, yrows.at[slot, k], sems.at[slot])
    ys = [_from_row_tiles(yrows.at[slot, k], tm, d // LANES) for k in range(TOP_K)]
    cols = []
    for j in range(d // LANES):
        moe = wt_ref[:, 0:1] * ys[0][j]
        for k in range(1, TOP_K):
            moe = moe + wt_ref[:, k:k + 1] * ys[k][j]
        cols.append(moe)
    out = x_ref[...] + jnp.concatenate(cols, axis=1)
    o_ref[...] = _rms(out, g_ref[...]) if final_norm else out


def _combine(pos8, x, wt, g, y, *, tm, final_norm):
    n, d = x.shape
    assert d == SUBLANES * LANES and tm % SUBLANES == 0
    last = n // tm - 1
    tile = pl.BlockSpec((tm, d), lambda i: (i, 0))
    return pl.pallas_call(
        functools.partial(_combine_body, final_norm=final_norm),
        grid=(n // tm,),
        in_specs=[pl.BlockSpec((1, 1, TOP_K * tm), lambda i: (i, 0, 0), memory_space=pltpu.SMEM),
                  pl.BlockSpec((1, 1, TOP_K * tm), lambda i: (jnp.minimum(i + 1, last), 0, 0),
                               memory_space=pltpu.SMEM),
                  tile, pl.BlockSpec((tm, LANES), lambda i: (i, 0)), _resident(g.shape),
                  pl.BlockSpec(memory_space=pl.ANY)],
        out_specs=tile,
        out_shape=jax.ShapeDtypeStruct((n, d), F32),
        scratch_shapes=[pltpu.VMEM((2, TOP_K, tm * SUBLANES, LANES), F32), pltpu.SemaphoreType.DMA((2,))],
        compiler_params=_params("arbitrary"),
        name="combine",
    )(pos8, pos8, x, wt, g, y)


def _routing_tables(idx, cnt, n, n_exp, tmb, tc):
    tm = idx.shape[1]
    idx = idx.reshape(n // tm, SUBLANES, tm)
    experts = jnp.arange(n_exp, dtype=jnp.int32)
    counts = cnt[0, :n_exp]
    padded = ((counts + (tmb - 1)) // tmb) * tmb
    ends = jnp.sum(jnp.where(experts[None, :] <= experts[:, None], padded[None, :], 0), axis=1)
    starts = ends - padded
    chosen = idx[:, 0:TOP_K, :]
    first = sum(jnp.where(chosen == e, starts[e], 0) for e in range(n_exp))
    pos8 = (first + idx[:, TOP_K:2 * TOP_K, :]) * SUBLANES
    pos8 = pos8.reshape(n // tm, TOP_K, tm // tc, tc).transpose(0, 2, 1, 3).reshape(n // tc, 1, TOP_K * tc)
    n_tiles = (n * TOP_K) // tmb + n_exp
    tile_start = jnp.arange(n_tiles, dtype=jnp.int32) * tmb
    tile_expert = jnp.minimum(jnp.sum((tile_start[:, None] >= ends[None, :]).astype(jnp.int32), axis=1), n_exp - 1)
    tile_rows = jnp.clip(sum(jnp.where(tile_expert == e, starts[e] + counts[e], 0) for e in range(n_exp)) - tile_start,
                         0, tmb)
    return pos8, ends, counts, tile_expert, ends[n_exp - 1:] // tmb, tile_rows, n_tiles


def _moe(x, g, w_router, wg, wu, wd, g_out, *, final_norm):
    h, idx, wt, cnt = _router(x, g, w_router, tm=ROUTER_ROW_TILE)
    pos8, ends, counts, tile_expert, n_valid, tile_rows, n_tiles = _routing_tables(
        idx, cnt, x.shape[0], wg.shape[0], EXPERT_ROW_TILE, ROW_TILE)
    xs = _dispatch(ends, counts, pos8, h, n_rows=n_tiles * EXPERT_ROW_TILE, tm=ROW_TILE, tmb=EXPERT_ROW_TILE)
    y = _experts(tile_expert, n_valid, tile_rows, xs, wg, wu, wd, tmb=EXPERT_ROW_TILE, fc=EXPERT_FF_TILE)
    return _combine(pos8, x, wt, g_out, y, tm=ROW_TILE, final_norm=final_norm)


def _final_norm_body(x_ref, g_ref, o_ref):
    o_ref[...] = _rms(x_ref[...], g_ref[...])


def _final_norm(x, g, *, tm):
    n, d = x.shape
    tile = pl.BlockSpec((tm, d), lambda i: (i, 0))
    return pl.pallas_call(
        _final_norm_body, grid=(n // tm,), in_specs=[tile, _resident(g.shape)], out_specs=tile,
        out_shape=jax.ShapeDtypeStruct((n, d), F32), compiler_params=_params("arbitrary"), name="final_norm",
    )(x, g)


def kernel(x, mem, g_mix, w_in, w_pool_group, pool_scale, w_pool_out, conv_w, w_conv_out, w_mix_out, g_xattn, g_mem, w_xq, w_xk, w_xv, w_xo, g_ffn, w_ff_gate, w_ff_up, w_ff_down, w_router, w_e_gate, w_e_up, w_e_down, g_final):
    b, s, d = x.shape
    depth = g_mix.shape[0]
    bf = lambda w: w.astype(BF16)
    row = lambda v: v.reshape(1, -1)

    kt, v = _memkv(mem, g_mem.reshape(depth, 1, d), bf(w_xk), bf(w_xv))
    xf = x.reshape(b * s, d)
    for l in range(depth):
        xf = _mixer(xf, row(g_mix[l]), bf(w_in[l]), bf(w_pool_group[l]), row(pool_scale[l]), bf(w_pool_out[l]),
                    conv_w[l], bf(w_conv_out[l]), bf(w_mix_out[l]), seq=s, tm=MIXER_ROW_TILE)
        xf = _xattn(xf, row(g_xattn[l]), bf(w_xq[l]), kt, v, bf(w_xo[l]), layer=l, seq=s, tm=MIXER_ROW_TILE)
        last = l == depth - 1
        i = l // 2
        if l % 2 == 0:
            xf = _ffn(xf, row(g_ffn[l]), bf(w_ff_gate[i]), bf(w_ff_up[i]), bf(w_ff_down[i]), tm=ROW_TILE)
            if last:
                xf = _final_norm(xf, row(g_final), tm=ROW_TILE)
        else:
            xf = _moe(xf, row(g_ffn[l]), w_router[i], w_e_gate[i], w_e_up[i], w_e_down[i],
                      row(g_final), final_norm=last)
    return xf.reshape(b, s, d)
```

```python
import functools

import jax
import jax.numpy as jnp
from jax import lax
from jax.experimental import pallas as pl
from jax.experimental.pallas import tpu as pltpu

F32 = jnp.float32
BF16 = jnp.bfloat16

EPS = 1e-6
POOL_WINDOWS = (2, 4, 8, 16)
CONV_K = 3
N_XHEADS = 4
TOP_K = 2

LANES = 128
SUBLANES = 8
VMEM_LIMIT_BYTES = 56 * 1024 * 1024
N_DMA_QUEUES = 2

POOL_HALO = 16
CONV_HALO = 8

ROW_TILE = 512
MIXER_ROW_TILE = 1024
ROW_CHUNKS = 2
MIXER_ROW_CHUNKS = 1
ROUTER_ROW_TILE = 1024
EXPERT_ROW_TILE = 1024
EXPERT_FF_TILE = 512
WEIGHT_BUFFERS = 3


def _dot(a, b):
    return jnp.dot(a, b, preferred_element_type=F32)


def _sigmoid(z):
    return 0.5 * jnp.tanh(0.5 * z) + 0.5


def _rms(xf, g):
    ms = jnp.mean(xf * xf, axis=-1, keepdims=True)
    return (xf * lax.rsqrt(ms + EPS)) * g


def _resident(shape):
    nd = len(shape)
    return pl.BlockSpec(shape, lambda *_: (0,) * nd, pipeline_mode=pl.Buffered(1))


def _params(*sem):
    return pltpu.CompilerParams(dimension_semantics=sem, vmem_limit_bytes=VMEM_LIMIT_BYTES)


def _mixer_body(x_ref, g_ref, win_ref, wgrp_ref, ps_ref, wpo_ref, cw_ref, wco_ref, wmo_ref, o_ref,
                pool_buf, conv_buf, *, tm, seq):
    n_groups, _, gw = wgrp_ref.shape
    pw = n_groups * gw
    cwid = cw_ref.shape[1]
    d = x_ref.shape[1]
    o_c, o_b, o_u, o_g = pw, pw + cwid, pw + 2 * cwid, pw + 3 * cwid

    pos0 = lax.rem(pl.program_id(0) * tm, seq)

    @pl.when(pos0 == 0)
    def _():
        pool_buf[:, 0:POOL_HALO, :] = jnp.zeros((n_groups, POOL_HALO, gw), F32)
        conv_buf[:, 0:CONV_HALO, :] = jnp.zeros((conv_buf.shape[0], CONV_HALO, LANES), F32)

    @pl.when(pos0 != 0)
    def _():
        pool_buf[:, 0:POOL_HALO, :] = pool_buf[:, tm:tm + POOL_HALO, :]
        conv_buf[:, 0:CONV_HALO, :] = conv_buf[:, tm:tm + CONV_HALO, :]

    cm = tm // MIXER_ROW_CHUNKS
    for r0 in range(0, tm, cm):
        x = x_ref[r0:r0 + cm, :]
        h = _rms(x, g_ref[...]).astype(BF16)

        u_pool = _dot(h, win_ref[:, 0:o_c])
        ys = []
        for g, w in enumerate(POOL_WINDOWS):
            ug = u_pool[:, g * gw:(g + 1) * gw]
            lo = POOL_HALO + r0
            pool_buf[g, lo:lo + cm, :] = ug
            acc = ug
            for k in range(1, w):
                acc = acc + pool_buf[g, lo - k:lo - k + cm, :]
            if r0 == 0:
                pos1 = lax.broadcasted_iota(jnp.int32, (POOL_HALO, gw), 0) + (pos0 + 1)
                head = acc[0:POOL_HALO] / jnp.minimum(pos1, w).astype(F32)
                mean = jnp.concatenate([head, acc[POOL_HALO:] * (1.0 / w)], axis=0)
            else:
                mean = acc * (1.0 / w)
            ys.append(_dot((mean - ug).astype(BF16), wgrp_ref[g]))
        y = jnp.concatenate(ys, axis=1) * ps_ref[...]
        y_pool = _dot(y.astype(BF16), wpo_ref[...])

        c_gate = _dot(h, win_ref[:, o_c:o_b])
        u_conv = _dot(h, win_ref[:, o_u:o_g])
        zc = c_gate * u_conv
        ycs = []
        for j in range(cwid // LANES):
            sl = slice(j * LANES, (j + 1) * LANES)
            zj = zc[:, sl]
            lo = CONV_HALO + r0
            conv_buf[j, lo:lo + cm, :] = zj
            acc = cw_ref[0:1, sl] * conv_buf[j, lo - (CONV_K - 1):lo - (CONV_K - 1) + cm, :]
            for k in range(1, CONV_K - 1):
                off = lo - (CONV_K - 1) + k
                acc = acc + cw_ref[k:k + 1, sl] * conv_buf[j, off:off + cm, :]
            ycs.append(acc + cw_ref[CONV_K - 1:CONV_K, sl] * zj)
        b_gate = _dot(h, win_ref[:, o_b:o_u])
        y_conv = _dot((b_gate * jnp.concatenate(ycs, axis=1)).astype(BF16), wco_ref[...])

        merged = _sigmoid(_dot(h, win_ref[:, o_g:o_g + d])) * y_pool
        merged = merged + _sigmoid(_dot(h, win_ref[:, o_g + d:o_g + 2 * d])) * y_conv
        o_ref[r0:r0 + cm, :] = x + _dot(merged.astype(BF16), wmo_ref[...])


def _mixer(x, g, win, wgrp, ps, wpo, cw, wco, wmo, *, seq, tm):
    n, d = x.shape
    n_groups, _, gw = wgrp.shape
    assert seq % tm == 0 and n % tm == 0 and gw == LANES and cw.shape[1] % LANES == 0
    assert (tm // MIXER_ROW_CHUNKS) % SUBLANES == 0 and tm // MIXER_ROW_CHUNKS >= POOL_HALO
    assert len(POOL_WINDOWS) == n_groups and max(POOL_WINDOWS) <= POOL_HALO and cw.shape[0] == CONV_K
    assert all(w & (w - 1) == 0 for w in POOL_WINDOWS)
    tile = pl.BlockSpec((tm, d), lambda i: (i, 0))
    return pl.pallas_call(
        functools.partial(_mixer_body, tm=tm, seq=seq),
        grid=(n // tm,),
        in_specs=[tile, _resident(g.shape), _resident(win.shape), _resident(wgrp.shape), _resident(ps.shape),
                  _resident(wpo.shape), _resident(cw.shape), _resident(wco.shape), _resident(wmo.shape)],
        out_specs=tile,
        out_shape=jax.ShapeDtypeStruct((n, d), F32),
        scratch_shapes=[pltpu.VMEM((n_groups, tm + POOL_HALO, gw), F32),
                        pltpu.VMEM((cw.shape[1] // LANES, tm + CONV_HALO, LANES), F32)],
        compiler_params=_params("arbitrary"),
        name="mixer",
    )(x, g, win, wgrp, ps, wpo, cw, wco, wmo)


def _memkv_body(mem_ref, g_ref, wk_ref, wv_ref, kt_ref, v_ref):
    mn = _rms(mem_ref[...], g_ref[...]).astype(BF16)
    kt_ref[...] = _dot(mn, wk_ref[...]).T.astype(BF16)
    v_ref[...] = _dot(mn, wv_ref[...]).astype(BF16)


def _memkv(mem, g_mem, wk, wv):
    b, m, d = mem.shape
    depth = wk.shape[0]
    per_layer = lambda l, i: (l, 0, 0)
    return pl.pallas_call(
        _memkv_body,
        grid=(depth, b),
        in_specs=[pl.BlockSpec((None, m, d), lambda l, i: (i, 0, 0)),
                  pl.BlockSpec((None, 1, d), per_layer),
                  pl.BlockSpec((None, d, d), per_layer),
                  pl.BlockSpec((None, d, d), per_layer)],
        out_specs=[pl.BlockSpec((None, None, d, m), lambda l, i: (l, i, 0, 0)),
                   pl.BlockSpec((None, None, m, d), lambda l, i: (l, i, 0, 0))],
        out_shape=[jax.ShapeDtypeStruct((depth, b, d, m), BF16), jax.ShapeDtypeStruct((depth, b, m, d), BF16)],
        compiler_params=_params("arbitrary", "arbitrary"),
        name="memkv",
    )(mem, g_mem, wk, wv)


def _xattn_body(x_ref, g_ref, wq_ref, kt_ref, v_ref, wo_ref, o_ref):
    tm = x_ref.shape[0]
    for r0 in range(0, tm, tm // ROW_CHUNKS):
        rows = slice(r0, r0 + tm // ROW_CHUNKS)
        x = x_ref[rows, :]
        h = _rms(x, g_ref[...]).astype(BF16)
        q = _dot(h, wq_ref[...])
        dh = q.shape[1] // N_XHEADS
        heads = []
        for hd in range(N_XHEADS):
            sl = slice(hd * dh, (hd + 1) * dh)
            s = _dot(q[:, sl].astype(BF16), kt_ref[sl, :]) * (dh ** -0.5)
            e = jnp.exp(s - jnp.max(s, axis=-1, keepdims=True))
            p = e / jnp.sum(e, axis=-1, keepdims=True)
            heads.append(_dot(p.astype(BF16), v_ref[:, sl]))
        o = jnp.concatenate(heads, axis=1).astype(BF16)
        o_ref[rows, :] = x + _dot(o, wo_ref[...])


def _xattn(x, g, wq, kt, v, wo, *, layer, seq, tm):
    n, d = x.shape
    m = v.shape[2]
    assert seq % tm == 0 and d % N_XHEADS == 0
    tiles_per_seq = seq // tm
    tile = pl.BlockSpec((tm, d), lambda i: (i, 0))
    return pl.pallas_call(
        _xattn_body,
        grid=(n // tm,),
        in_specs=[tile, _resident(g.shape), _resident(wq.shape),
                  pl.BlockSpec((None, None, d, m), lambda i: (layer, i // tiles_per_seq, 0, 0)),
                  pl.BlockSpec((None, None, m, d), lambda i: (layer, i // tiles_per_seq, 0, 0)),
                  _resident(wo.shape)],
        out_specs=tile,
        out_shape=jax.ShapeDtypeStruct((n, d), F32),
        compiler_params=_params("arbitrary"),
        name="xattn",
    )(x, g, wq, kt, v, wo)


def _swiglu(h, wg, wu, wd):
    g = _dot(h, wg)
    return _dot((g * _sigmoid(g) * _dot(h, wu)).astype(BF16), wd)


def _ffn_body(x_ref, g_ref, wg_ref, wu_ref, wd_ref, o_ref):
    tm = x_ref.shape[0]
    for r0 in range(0, tm, tm // ROW_CHUNKS):
        rows = slice(r0, r0 + tm // ROW_CHUNKS)
        x = x_ref[rows, :]
        h = _rms(x, g_ref[...]).astype(BF16)
        o_ref[rows, :] = x + _swiglu(h, wg_ref[...], wu_ref[...], wd_ref[...])


def _ffn(x, g, wg, wu, wd, *, tm):
    n, d = x.shape
    tile = pl.BlockSpec((tm, d), lambda i: (i, 0))
    return pl.pallas_call(
        _ffn_body,
        grid=(n // tm,),
        in_specs=[tile, _resident(g.shape), _resident(wg.shape), _resident(wu.shape), _resident(wd.shape)],
        out_specs=tile,
        out_shape=jax.ShapeDtypeStruct((n, d), F32),
        compiler_params=_params("arbitrary"),
        name="ffn",
    )(x, g, wg, wu, wd)


def _split_bf16(v):
    hi = v.astype(BF16)
    return hi, (v - hi.astype(F32)).astype(BF16)


def _router_body(x_ref, g_ref, w2_ref, h_ref, idx_ref, wt_ref, cnt_ref, base_ref, *, n_exp):
    tm = x_ref.shape[0]

    @pl.when(pl.program_id(0) == 0)
    def _():
        base_ref[...] = jnp.zeros(base_ref.shape, F32)

    h = _rms(x_ref[...], g_ref[...])
    h_ref[...] = h
    h_hi, h_lo = _split_bf16(h)
    part = _dot(h_hi, w2_ref[...]) + _dot(h_lo, w2_ref[...])
    lane = lax.broadcasted_iota(jnp.int32, (tm, LANES), 1)
    logits = jnp.where(lane < n_exp, part + pltpu.roll(part, LANES - n_exp, axis=1), -jnp.inf)
    lane_f = lane.astype(F32)
    m1 = jnp.max(logits, axis=-1, keepdims=True)
    i1 = jnp.min(jnp.where(logits == m1, lane_f, float(LANES)), axis=-1, keepdims=True).astype(jnp.int32)
    rest = jnp.where(lane == i1, -jnp.inf, logits)
    m2 = jnp.max(rest, axis=-1, keepdims=True)
    i2 = jnp.min(jnp.where(rest == m2, lane_f, float(LANES)), axis=-1, keepdims=True).astype(jnp.int32)
    e2 = jnp.exp(m2 - m1)
    den = 1.0 + e2
    wt_ref[...] = jnp.where(lane == 0, 1.0 / den, jnp.where(lane == 1, e2 / den, 0.0))

    oh1, oh2 = lane == i1, lane == i2
    hits = jnp.logical_or(oh1, oh2)
    earlier = lax.broadcasted_iota(jnp.int32, (tm, tm), 1) < lax.broadcasted_iota(jnp.int32, (tm, tm), 0)
    prefix = _dot(earlier.astype(BF16), hits.astype(BF16)) + base_ref[...]
    r1 = jnp.sum(jnp.where(oh1, prefix, 0.0), axis=-1, keepdims=True).astype(jnp.int32)
    r2 = jnp.sum(jnp.where(oh2, prefix, 0.0), axis=-1, keepdims=True).astype(jnp.int32)
    idx = jnp.where(lane == 0, i1, jnp.where(lane == 1, i2, jnp.where(lane == 2, r1, jnp.where(lane == 3, r2, 0))))
    idx_ref[...] = idx.T[0:SUBLANES, :]
    base_ref[...] += jnp.sum(hits.astype(F32), axis=0, keepdims=True)
    cnt_ref[...] = jnp.broadcast_to(base_ref[...], cnt_ref.shape).astype(jnp.int32)


def _router(x, g, w_router, *, tm):
    n, d = x.shape
    n_exp = w_router.shape[1]
    assert 2 * n_exp <= LANES
    w2 = jnp.pad(jnp.concatenate(_split_bf16(w_router), axis=1), ((0, 0), (0, LANES - 2 * n_exp)))
    tile = lambda w: pl.BlockSpec((tm, w), lambda i: (i, 0))
    return pl.pallas_call(
        functools.partial(_router_body, n_exp=n_exp),
        grid=(n // tm,),
        in_specs=[tile(d), _resident(g.shape), _resident(w2.shape)],
        out_specs=[tile(d), pl.BlockSpec((SUBLANES, tm), lambda i: (i, 0)), tile(LANES),
                   pl.BlockSpec((SUBLANES, LANES), lambda i: (0, 0))],
        out_shape=[jax.ShapeDtypeStruct((n, d), F32), jax.ShapeDtypeStruct((n // tm * SUBLANES, tm), jnp.int32),
                   jax.ShapeDtypeStruct((n, LANES), F32), jax.ShapeDtypeStruct((SUBLANES, LANES), jnp.int32)],
        scratch_shapes=[pltpu.VMEM((1, LANES), F32)],
        compiler_params=_params("arbitrary"),
        name="router",
    )(x, g, w2)


def _to_row_tiles(v, dst_ref):
    c = v.shape[1] // LANES
    for j in range(c):
        dst_ref[pl.ds(j, v.shape[0], stride=c), :] = v[:, j * LANES:(j + 1) * LANES]


def _from_row_tiles(src_ref, rows, c):
    return [src_ref[pl.ds(j, rows, stride=c), :] for j in range(c)]


def _tile_copy(src, src_row8, dst, dst_row8, sem):
    return pltpu.make_async_copy(src.at[pl.ds(pl.multiple_of(src_row8, SUBLANES), SUBLANES)],
                                 dst.at[pl.ds(pl.multiple_of(dst_row8, SUBLANES), SUBLANES)], sem)


def _tiles_wait(hbm, vmem, sem):
    pltpu.make_async_copy(hbm.at[pl.ds(0, vmem.shape[0])], vmem, sem).wait()


def _dispatch_body(ends_ref, cnt_ref, pos_ref, h_ref, xs_hbm, zbuf, stage, zsem, sems, *, tmb):
    i, n = pl.program_id(0), pl.num_programs(0)
    tm = h_ref.shape[0]
    slot = lax.rem(i, 2)

    @pl.when(i == 0)
    def _():
        n_exp = ends_ref.shape[0]
        zbuf[...] = jnp.zeros(zbuf.shape, zbuf.dtype)
        for e in range(2 * n_exp):
            first = ends_ref[e] - tmb if e < n_exp else ends_ref[n_exp - 1] + (e - n_exp) * tmb
            live = cnt_ref[e] > 0 if e < n_exp else first * SUBLANES < xs_hbm.shape[0]

            @pl.when(live)
            def _():
                start = pl.multiple_of(first * SUBLANES, tmb * SUBLANES)
                fill = pltpu.make_async_copy(zbuf, xs_hbm.at[pl.ds(start, tmb * SUBLANES)], zsem)
                fill.start()
                fill.wait()

    _to_row_tiles(h_ref[...], stage.at[slot])

    def scatter(g, c):
        for j in range(SUBLANES):
            for k in range(TOP_K):
                _tile_copy(stage.at[slot], (g * SUBLANES + j) * SUBLANES,
                           xs_hbm, pos_ref[0, 0, k * tm + g * SUBLANES + j],
                           sems.at[slot]).start(priority=(j * TOP_K + k) % N_DMA_QUEUES)
        return c
    lax.fori_loop(0, tm // SUBLANES, scatter, 0)

    def drain(s):
        for _ in range(TOP_K):
            _tiles_wait(xs_hbm, stage.at[s], sems.at[s])

    @pl.when(i > 0)
    def _():
        drain(1 - slot)

    @pl.when(i == n - 1)
    def _():
        drain(slot)


def _dispatch(ends, counts, pos8, h, *, n_rows, tm, tmb):
    n, d = h.shape
    assert d == SUBLANES * LANES and tm % SUBLANES == 0
    return pl.pallas_call(
        functools.partial(_dispatch_body, tmb=tmb),
        grid_spec=pltpu.PrefetchScalarGridSpec(
            num_scalar_prefetch=2,
            grid=(n // tm,),
            in_specs=[pl.BlockSpec((1, 1, TOP_K * tm), lambda i, *_: (i, 0, 0), memory_space=pltpu.SMEM),
                      pl.BlockSpec((tm, d), lambda i, *_: (i, 0))],
            out_specs=pl.BlockSpec(memory_space=pl.ANY),
            scratch_shapes=[pltpu.VMEM((tmb * SUBLANES, LANES), F32), pltpu.VMEM((2, tm * SUBLANES, LANES), F32),
                            pltpu.SemaphoreType.DMA(()), pltpu.SemaphoreType.DMA((2,))],
        ),
        out_shape=jax.ShapeDtypeStruct((n_rows * SUBLANES, LANES), F32),
        compiler_params=_params("arbitrary"),
        name="dispatch",
    )(ends, counts, pos8, h)


def _expert_body(te_ref, nv_ref, tr_ref, xs_ref, wg_hbm, wu_hbm, wd_hbm, y_ref, hb, acc, wgb, wub, wdb, sems, *, fc):
    i = pl.program_id(0)
    tmb, d = hb.shape
    nf = wg_hbm.shape[2] // fc
    n_valid = nv_ref[0]

    def weight_copies(tile, f, slot):
        e = te_ref[tile]
        cols = pl.ds(pl.multiple_of(f * fc, fc), fc)
        return (pltpu.make_async_copy(wg_hbm.at[e, :, cols], wgb.at[slot], sems.at[0, slot]),
                pltpu.make_async_copy(wu_hbm.at[e, :, cols], wub.at[slot], sems.at[1, slot]),
                pltpu.make_async_copy(wd_hbm.at[e, cols, :], wdb.at[slot], sems.at[2, slot]))

    @pl.when(i >= n_valid)
    def _():
        y_ref[...] = jnp.zeros(y_ref.shape, F32)

    ahead = WEIGHT_BUFFERS - 1

    def run_chunk(f, first, last):
        q = i * nf + f
        wraps = f + ahead >= nf
        ahead_tile = jnp.where(wraps, i + 1, i)

        @pl.when(ahead_tile < n_valid)
        def _():
            for cp in weight_copies(ahead_tile, jnp.where(wraps, f + ahead - nf, f + ahead),
                                    lax.rem(q + ahead, WEIGHT_BUFFERS)):
                cp.start()

        slot = lax.rem(q, WEIGHT_BUFFERS)
        for cp in weight_copies(i, f, slot):
            cp.wait()

        def accumulate(m):
            if first:
                h = jnp.concatenate([col.astype(BF16) for col in _from_row_tiles(xs_ref, m, d // LANES)], axis=1)
                hb[0:m, :] = h
            else:
                h = hb[0:m, :]
            contrib = _swiglu(h, wgb[slot].astype(BF16), wub[slot].astype(BF16), wdb[slot].astype(BF16))
            total = contrib if first else acc[0:m, :] + contrib
            if last:
                _to_row_tiles(total, y_ref)
                if m < tmb:
                    y_ref[m * (d // LANES):, :] = jnp.zeros(((tmb - m) * (d // LANES), LANES), F32)
            else:
                acc[0:m, :] = total

        @pl.when(tr_ref[i] > tmb // 2)
        def _():
            accumulate(tmb)

        @pl.when(tr_ref[i] <= tmb // 2)
        def _():
            accumulate(tmb // 2)

    @pl.when(i < n_valid)
    def _():
        @pl.when(i == 0)
        def _():
            for f0 in range(ahead):
                for cp in weight_copies(0, f0, f0):
                    cp.start()

        run_chunk(0, True, False)

        def middle(f, carry):
            run_chunk(f, False, False)
            return carry
        lax.fori_loop(1, nf - 1, middle, 0)
        run_chunk(nf - 1, False, True)


def _experts(tile_expert, n_valid, tile_rows, xs, wg, wu, wd, *, tmb, fc):
    n_tiles = tile_expert.shape[0]
    _, d, ff = wg.shape
    c = d // LANES
    assert ff % fc == 0 and ff // fc >= 2 and xs.shape == (n_tiles * tmb * c, LANES)
    return pl.pallas_call(
        functools.partial(_expert_body, fc=fc),
        grid_spec=pltpu.PrefetchScalarGridSpec(
            num_scalar_prefetch=3,
            grid=(n_tiles,),
            in_specs=[pl.BlockSpec((tmb * c, LANES), lambda i, te, nv, tr: (jnp.minimum(i, nv[0] - 1), 0)),
                      pl.BlockSpec(memory_space=pl.ANY), pl.BlockSpec(memory_space=pl.ANY),
                      pl.BlockSpec(memory_space=pl.ANY)],
            out_specs=pl.BlockSpec((tmb * c, LANES), lambda i, te, nv, tr: (i, 0)),
            scratch_shapes=[pltpu.VMEM((tmb, d), BF16), pltpu.VMEM((tmb, d), F32),
                            pltpu.VMEM((WEIGHT_BUFFERS, d, fc), wg.dtype), pltpu.VMEM((WEIGHT_BUFFERS, d, fc), wu.dtype),
                            pltpu.VMEM((WEIGHT_BUFFERS, fc, d), wd.dtype),
                            pltpu.SemaphoreType.DMA((3, WEIGHT_BUFFERS))],
        ),
        out_shape=jax.ShapeDtypeStruct((n_tiles * tmb * c, LANES), F32),
        compiler_params=_params("arbitrary"),
        name="experts",
    )(tile_expert, n_valid, tile_rows, xs, wg, wu, wd)


def _combine_body(pos_ref, pos_next_ref, x_ref, wt_ref, g_ref, y_hbm, o_ref, yrows, sems, *, final_norm):
    i, n = pl.program_id(0), pl.num_programs(0)
    tm, d = x_ref.shape
    slot = lax.rem(i, 2)

    def gather(p_ref, s):
        def body(g, c):
            for j in range(SUBLANES):
                for k in range(TOP_K):
                    _tile_copy(y_hbm, p_ref[0, 0, k * tm + g * SUBLANES + j],
                               yrows.at[s, k], (g * SUBLANES + j) * SUBLANES,
                               sems.at[s]).start(priority=(j * TOP_K + k) % N_DMA_QUEUES)
            return c
        lax.fori_loop(0, tm // SUBLANES, body, 0)

    @pl.when(i == 0)
    def _():
        gather(pos_ref, 0)

    @pl.when(i + 1 < n)
    def _():
        gather(pos_next_ref, 1 - slot)

    for k in range(TOP_K):
        _tiles_wait(y_hbm, yrows.at[slot, k], sems.at[slot])
    ys = [_from_row_tiles(yrows.at[slot, k], tm, d // LANES) for k in range(TOP_K)]
    cols = []
    for j in range(d // LANES):
        moe = wt_ref[:, 0:1] * ys[0][j]
        for k in range(1, TOP_K):
            moe = moe + wt_ref[:, k:k + 1] * ys[k][j]
        cols.append(moe)
    out = x_ref[...] + jnp.concatenate(cols, axis=1)
    o_ref[...] = _rms(out, g_ref[...]) if final_norm else out


def _combine(pos8, x, wt, g, y, *, tm, final_norm):
    n, d = x.shape
    assert d == SUBLANES * LANES and tm % SUBLANES == 0
    last = n // tm - 1
    tile = pl.BlockSpec((tm, d), lambda i: (i, 0))
    return pl.pallas_call(
        functools.partial(_combine_body, final_norm=final_norm),
        grid=(n // tm,),
        in_specs=[pl.BlockSpec((1, 1, TOP_K * tm), lambda i: (i, 0, 0), memory_space=pltpu.SMEM),
                  pl.BlockSpec((1, 1, TOP_K * tm), lambda i: (jnp.minimum(i + 1, last), 0, 0),
                               memory_space=pltpu.SMEM),
                  tile, pl.BlockSpec((tm, LANES), lambda i: (i, 0)), _resident(g.shape),
                  pl.BlockSpec(memory_space=pl.ANY)],
        out_specs=tile,
        out_shape=jax.ShapeDtypeStruct((n, d), F32),
        scratch_shapes=[pltpu.VMEM((2, TOP_K, tm * SUBLANES, LANES), F32), pltpu.SemaphoreType.DMA((2,))],
        compiler_params=_params("arbitrary"),
        name="combine",
    )(pos8, pos8, x, wt, g, y)


def _routing_tables(idx, cnt, n, n_exp, tmb, tc):
    tm = idx.shape[1]
    idx = idx.reshape(n // tm, SUBLANES, tm)
    experts = jnp.arange(n_exp, dtype=jnp.int32)
    counts = cnt[0, :n_exp]
    padded = ((counts + (tmb - 1)) // tmb) * tmb
    ends = jnp.sum(jnp.where(experts[None, :] <= experts[:, None], padded[None, :], 0), axis=1)
    starts = ends - padded
    chosen = idx[:, 0:TOP_K, :]
    first = sum(jnp.where(chosen == e, starts[e], 0) for e in range(n_exp))
    pos8 = (first + idx[:, TOP_K:2 * TOP_K, :]) * SUBLANES
    pos8 = pos8.reshape(n // tm, TOP_K, tm // tc, tc).transpose(0, 2, 1, 3).reshape(n // tc, 1, TOP_K * tc)
    n_tiles = (n * TOP_K) // tmb + n_exp
    tile_start = jnp.arange(n_tiles, dtype=jnp.int32) * tmb
    tile_expert = jnp.minimum(jnp.sum((tile_start[:, None] >= ends[None, :]).astype(jnp.int32), axis=1), n_exp - 1)
    tile_rows = jnp.clip(sum(jnp.where(tile_expert == e, starts[e] + counts[e], 0) for e in range(n_exp)) - tile_start,
                         0, tmb)
    return pos8, ends, counts, tile_expert, ends[n_exp - 1:] // tmb, tile_rows, n_tiles


def _moe(x, g, w_router, wg, wu, wd, g_out, *, final_norm):
    h, idx, wt, cnt = _router(x, g, w_router, tm=ROUTER_ROW_TILE)
    pos8, ends, counts, tile_expert, n_valid, tile_rows, n_tiles = _routing_tables(
        idx, cnt, x.shape[0], wg.shape[0], EXPERT_ROW_TILE, ROW_TILE)
    xs = _dispatch(ends, counts, pos8, h, n_rows=n_tiles * EXPERT_ROW_TILE, tm=ROW_TILE, tmb=EXPERT_ROW_TILE)
    y = _experts(tile_expert, n_valid, tile_rows, xs, wg, wu, wd, tmb=EXPERT_ROW_TILE, fc=EXPERT_FF_TILE)
    return _combine(pos8, x, wt, g_out, y, tm=ROW_TILE, final_norm=final_norm)


def _final_norm_body(x_ref, g_ref, o_ref):
    o_ref[...] = _rms(x_ref[...], g_ref[...])


def _final_norm(x, g, *, tm):
    n, d = x.shape
    tile = pl.BlockSpec((tm, d), lambda i: (i, 0))
    return pl.pallas_call(
        _final_norm_body, grid=(n // tm,), in_specs=[tile, _resident(g.shape)], out_specs=tile,
        out_shape=jax.ShapeDtypeStruct((n, d), F32), compiler_params=_params("arbitrary"), name="final_norm",
    )(x, g)


def kernel(x, mem, g_mix, w_in, w_pool_group, pool_scale, w_pool_out, conv_w, w_conv_out, w_mix_out, g_xattn, g_mem, w_xq, w_xk, w_xv, w_xo, g_ffn, w_ff_gate, w_ff_up, w_ff_down, w_router, w_e_gate, w_e_up, w_e_down, g_final):
    b, s, d = x.shape
    depth = g_mix.shape[0]
    bf = lambda w: w.astype(BF16)
    row = lambda v: v.reshape(1, -1)

    kt, v = _memkv(mem, g_mem.reshape(depth, 1, d), bf(w_xk), bf(w_xv))
    xf = x.reshape(b * s, d)
    for l in range(depth):
        xf = _mixer(xf, row(g_mix[l]), bf(w_in[l]), bf(w_pool_group[l]), row(pool_scale[l]), bf(w_pool_out[l]),
                    conv_w[l], bf(w_conv_out[l]), bf(w_mix_out[l]), seq=s, tm=MIXER_ROW_TILE)
        xf = _xattn(xf, row(g_xattn[l]), bf(w_xq[l]), kt, v, bf(w_xo[l]), layer=l, seq=s, tm=MIXER_ROW_TILE)
        last = l == depth - 1
        i = l // 2
        if l % 2 == 0:
            xf = _ffn(xf, row(g_ffn[l]), bf(w_ff_gate[i]), bf(w_ff_up[i]), bf(w_ff_down[i]), tm=ROW_TILE)
            if last:
                xf = _final_norm(xf, row(g_final), tm=ROW_TILE)
        else:
            xf = _moe(xf, row(g_ffn[l]), w_router[i], w_e_gate[i], w_e_up[i], w_e_down[i],
                      row(g_final), final_norm=last)
    return xf.reshape(b, s, d)
```

```python
import functools

import jax
import jax.numpy as jnp
from jax import lax
from jax.experimental import pallas as pl
from jax.experimental.pallas import tpu as pltpu

F32 = jnp.float32
BF16 = jnp.bfloat16

EPS = 1e-6
POOL_WINDOWS = (2, 4, 8, 16)
CONV_K = 3
N_XHEADS = 4
TOP_K = 2

LANES = 128
SUBLANES = 8
VMEM_LIMIT_BYTES = 56 * 1024 * 1024
N_DMA_QUEUES = 2

POOL_HALO = 16
CONV_HALO = 8

ROW_TILE = 512
MIXER_ROW_TILE = 1024
ROW_CHUNKS = 2
MIXER_ROW_CHUNKS = 1
ROUTER_ROW_TILE = 1024
EXPERT_ROW_TILE = 1024
EXPERT_FF_TILE = 512
WEIGHT_BUFFERS = 4


def _dot(a, b):
    return jnp.dot(a, b, preferred_element_type=F32)


def _sigmoid(z):
    return 0.5 * jnp.tanh(0.5 * z) + 0.5


def _rms(xf, g):
    ms = jnp.mean(xf * xf, axis=-1, keepdims=True)
    return (xf * lax.rsqrt(ms + EPS)) * g


def _resident(shape):
    nd = len(shape)
    return pl.BlockSpec(shape, lambda *_: (0,) * nd, pipeline_mode=pl.Buffered(1))


def _params(*sem):
    return pltpu.CompilerParams(dimension_semantics=sem, vmem_limit_bytes=VMEM_LIMIT_BYTES)


def _mixer_body(x_ref, g_ref, win_ref, wgrp_ref, ps_ref, wpo_ref, cw_ref, wco_ref, wmo_ref, o_ref,
                pool_buf, conv_buf, *, tm, seq):
    n_groups, _, gw = wgrp_ref.shape
    pw = n_groups * gw
    cwid = cw_ref.shape[1]
    d = x_ref.shape[1]
    o_c, o_b, o_u, o_g = pw, pw + cwid, pw + 2 * cwid, pw + 3 * cwid

    pos0 = lax.rem(pl.program_id(0) * tm, seq)

    @pl.when(pos0 == 0)
    def _():
        pool_buf[:, 0:POOL_HALO, :] = jnp.zeros((n_groups, POOL_HALO, gw), F32)
        conv_buf[:, 0:CONV_HALO, :] = jnp.zeros((conv_buf.shape[0], CONV_HALO, LANES), F32)

    @pl.when(pos0 != 0)
    def _():
        pool_buf[:, 0:POOL_HALO, :] = pool_buf[:, tm:tm + POOL_HALO, :]
        conv_buf[:, 0:CONV_HALO, :] = conv_buf[:, tm:tm + CONV_HALO, :]

    cm = tm // MIXER_ROW_CHUNKS
    for r0 in range(0, tm, cm):
        x = x_ref[r0:r0 + cm, :]
        h = _rms(x, g_ref[...]).astype(BF16)

        u_pool = _dot(h, win_ref[:, 0:o_c])
        ys = []
        for g, w in enumerate(POOL_WINDOWS):
            ug = u_pool[:, g * gw:(g + 1) * gw]
            lo = POOL_HALO + r0
            pool_buf[g, lo:lo + cm, :] = ug
            acc = ug
            for k in range(1, w):
                acc = acc + pool_buf[g, lo - k:lo - k + cm, :]
            if r0 == 0:
                pos1 = lax.broadcasted_iota(jnp.int32, (POOL_HALO, gw), 0) + (pos0 + 1)
                head = acc[0:POOL_HALO] / jnp.minimum(pos1, w).astype(F32)
                mean = jnp.concatenate([head, acc[POOL_HALO:] * (1.0 / w)], axis=0)
            else:
                mean = acc * (1.0 / w)
            ys.append(_dot((mean - ug).astype(BF16), wgrp_ref[g]))
        y = jnp.concatenate(ys, axis=1) * ps_ref[...]
        y_pool = _dot(y.astype(BF16), wpo_ref[...])

        c_gate = _dot(h, win_ref[:, o_c:o_b])
        u_conv = _dot(h, win_ref[:, o_u:o_g])
        zc = c_gate * u_conv
        ycs = []
        for j in range(cwid // LANES):
            sl = slice(j * LANES, (j + 1) * LANES)
            zj = zc[:, sl]
            lo = CONV_HALO + r0
            conv_buf[j, lo:lo + cm, :] = zj
            acc = cw_ref[0:1, sl] * conv_buf[j, lo - (CONV_K - 1):lo - (CONV_K - 1) + cm, :]
            for k in range(1, CONV_K - 1):
                off = lo - (CONV_K - 1) + k
                acc = acc + cw_ref[k:k + 1, sl] * conv_buf[j, off:off + cm, :]
            ycs.append(acc + cw_ref[CONV_K - 1:CONV_K, sl] * zj)
        b_gate = _dot(h, win_ref[:, o_b:o_u])
        y_conv = _dot((b_gate * jnp.concatenate(ycs, axis=1)).astype(BF16), wco_ref[...])

        merged = _sigmoid(_dot(h, win_ref[:, o_g:o_g + d])) * y_pool
        merged = merged + _sigmoid(_dot(h, win_ref[:, o_g + d:o_g + 2 * d])) * y_conv
        o_ref[r0:r0 + cm, :] = x + _dot(merged.astype(BF16), wmo_ref[...])


def _mixer(x, g, win, wgrp, ps, wpo, cw, wco, wmo, *, seq, tm):
    n, d = x.shape
    n_groups, _, gw = wgrp.shape
    assert seq % tm == 0 and n % tm == 0 and gw == LANES and cw.shape[1] % LANES == 0
    assert (tm // MIXER_ROW_CHUNKS) % SUBLANES == 0 and tm // MIXER_ROW_CHUNKS >= POOL_HALO
    assert len(POOL_WINDOWS) == n_groups and max(POOL_WINDOWS) <= POOL_HALO and cw.shape[0] == CONV_K
    assert all(w & (w - 1) == 0 for w in POOL_WINDOWS)
    tile = pl.BlockSpec((tm, d), lambda i: (i, 0))
    return pl.pallas_call(
        functools.partial(_mixer_body, tm=tm, seq=seq),
        grid=(n // tm,),
        in_specs=[tile, _resident(g.shape), _resident(win.shape), _resident(wgrp.shape), _resident(ps.shape),
                  _resident(wpo.shape), _resident(cw.shape), _resident(wco.shape), _resident(wmo.shape)],
        out_specs=tile,
        out_shape=jax.ShapeDtypeStruct((n, d), F32),
        scratch_shapes=[pltpu.VMEM((n_groups, tm + POOL_HALO, gw), F32),
                        pltpu.VMEM((cw.shape[1] // LANES, tm + CONV_HALO, LANES), F32)],
        compiler_params=_params("arbitrary"),
        name="mixer",
    )(x, g, win, wgrp, ps, wpo, cw, wco, wmo)


def _memkv_body(mem_ref, g_ref, wk_ref, wv_ref, kt_ref, v_ref):
    mn = _rms(mem_ref[...], g_ref[...]).astype(BF16)
    kt_ref[...] = _dot(mn, wk_ref[...]).T.astype(BF16)
    v_ref[...] = _dot(mn, wv_ref[...]).astype(BF16)


def _memkv(mem, g_mem, wk, wv):
    b, m, d = mem.shape
    depth = wk.shape[0]
    per_layer = lambda l, i: (l, 0, 0)
    return pl.pallas_call(
        _memkv_body,
        grid=(depth, b),
        in_specs=[pl.BlockSpec((None, m, d), lambda l, i: (i, 0, 0)),
                  pl.BlockSpec((None, 1, d), per_layer),
                  pl.BlockSpec((None, d, d), per_layer),
                  pl.BlockSpec((None, d, d), per_layer)],
        out_specs=[pl.BlockSpec((None, None, d, m), lambda l, i: (l, i, 0, 0)),
                   pl.BlockSpec((None, None, m, d), lambda l, i: (l, i, 0, 0))],
        out_shape=[jax.ShapeDtypeStruct((depth, b, d, m), BF16), jax.ShapeDtypeStruct((depth, b, m, d), BF16)],
        compiler_params=_params("arbitrary", "arbitrary"),
        name="memkv",
    )(mem, g_mem, wk, wv)


def _xattn_body(x_ref, g_ref, wq_ref, kt_ref, v_ref, wo_ref, o_ref):
    tm = x_ref.shape[0]
    for r0 in range(0, tm, tm // ROW_CHUNKS):
        rows = slice(r0, r0 + tm // ROW_CHUNKS)
        x = x_ref[rows, :]
        h = _rms(x, g_ref[...]).astype(BF16)
        q = _dot(h, wq_ref[...])
        dh = q.shape[1] // N_XHEADS
        heads = []
        for hd in range(N_XHEADS):
            sl = slice(hd * dh, (hd + 1) * dh)
            s = _dot(q[:, sl].astype(BF16), kt_ref[sl, :]) * (dh ** -0.5)
            e = jnp.exp(s - jnp.max(s, axis=-1, keepdims=True))
            p = e / jnp.sum(e, axis=-1, keepdims=True)
            heads.append(_dot(p.astype(BF16), v_ref[:, sl]))
        o = jnp.concatenate(heads, axis=1).astype(BF16)
        o_ref[rows, :] = x + _dot(o, wo_ref[...])


def _xattn(x, g, wq, kt, v, wo, *, layer, seq, tm):
    n, d = x.shape
    m = v.shape[2]
    assert seq % tm == 0 and d % N_XHEADS == 0
    tiles_per_seq = seq // tm
    tile = pl.BlockSpec((tm, d), lambda i: (i, 0))
    return pl.pallas_call(
        _xattn_body,
        grid=(n // tm,),
        in_specs=[tile, _resident(g.shape), _resident(wq.shape),
                  pl.BlockSpec((None, None, d, m), lambda i: (layer, i // tiles_per_seq, 0, 0)),
                  pl.BlockSpec((None, None, m, d), lambda i: (layer, i // tiles_per_seq, 0, 0)),
                  _resident(wo.shape)],
        out_specs=tile,
        out_shape=jax.ShapeDtypeStruct((n, d), F32),
        compiler_params=_params("arbitrary"),
        name="xattn",
    )(x, g, wq, kt, v, wo)


def _swiglu(h, wg, wu, wd):
    g = _dot(h, wg)
    return _dot((g * _sigmoid(g) * _dot(h, wu)).astype(BF16), wd)


def _ffn_body(x_ref, g_ref, wg_ref, wu_ref, wd_ref, o_ref):
    tm = x_ref.shape[0]
    for r0 in range(0, tm, tm // ROW_CHUNKS):
        rows = slice(r0, r0 + tm // ROW_CHUNKS)
        x = x_ref[rows, :]
        h = _rms(x, g_ref[...]).astype(BF16)
        o_ref[rows, :] = x + _swiglu(h, wg_ref[...], wu_ref[...], wd_ref[...])


def _ffn(x, g, wg, wu, wd, *, tm):
    n, d = x.shape
    tile = pl.BlockSpec((tm, d), lambda i: (i, 0))
    return pl.pallas_call(
        _ffn_body,
        grid=(n // tm,),
        in_specs=[tile, _resident(g.shape), _resident(wg.shape), _resident(wu.shape), _resident(wd.shape)],
        out_specs=tile,
        out_shape=jax.ShapeDtypeStruct((n, d), F32),
        compiler_params=_params("arbitrary"),
        name="ffn",
    )(x, g, wg, wu, wd)


def _split_bf16(v):
    hi = v.astype(BF16)
    return hi, (v - hi.astype(F32)).astype(BF16)


def _router_body(x_ref, g_ref, w2_ref, h_ref, idx_ref, wt_ref, cnt_ref, base_ref, *, n_exp):
    tm = x_ref.shape[0]

    @pl.when(pl.program_id(0) == 0)
    def _():
        base_ref[...] = jnp.zeros(base_ref.shape, F32)

    h = _rms(x_ref[...], g_ref[...])
    h_ref[...] = h
    h_hi, h_lo = _split_bf16(h)
    part = _dot(h_hi, w2_ref[...]) + _dot(h_lo, w2_ref[...])
    lane = lax.broadcasted_iota(jnp.int32, (tm, LANES), 1)
    logits = jnp.where(lane < n_exp, part + pltpu.roll(part, LANES - n_exp, axis=1), -jnp.inf)
    lane_f = lane.astype(F32)
    m1 = jnp.max(logits, axis=-1, keepdims=True)
    i1 = jnp.min(jnp.where(logits == m1, lane_f, float(LANES)), axis=-1, keepdims=True).astype(jnp.int32)
    rest = jnp.where(lane == i1, -jnp.inf, logits)
    m2 = jnp.max(rest, axis=-1, keepdims=True)
    i2 = jnp.min(jnp.where(rest == m2, lane_f, float(LANES)), axis=-1, keepdims=True).astype(jnp.int32)
    e2 = jnp.exp(m2 - m1)
    den = 1.0 + e2
    wt_ref[...] = jnp.where(lane == 0, 1.0 / den, jnp.where(lane == 1, e2 / den, 0.0))

    oh1, oh2 = lane == i1, lane == i2
    hits = jnp.logical_or(oh1, oh2)
    earlier = lax.broadcasted_iota(jnp.int32, (tm, tm), 1) < lax.broadcasted_iota(jnp.int32, (tm, tm), 0)
    prefix = _dot(earlier.astype(BF16), hits.astype(BF16)) + base_ref[...]
    r1 = jnp.sum(jnp.where(oh1, prefix, 0.0), axis=-1, keepdims=True).astype(jnp.int32)
    r2 = jnp.sum(jnp.where(oh2, prefix, 0.0), axis=-1, keepdims=True).astype(jnp.int32)
    idx = jnp.where(lane == 0, i1, jnp.where(lane == 1, i2, jnp.where(lane == 2, r1, jnp.where(lane == 3, r2, 0))))
    idx_ref[...] = idx.T[0:SUBLANES, :]
    base_ref[...] += jnp.sum(hits.astype(F32), axis=0, keepdims=True)
    cnt_ref[...] = jnp.broadcast_to(base_ref[...], cnt_ref.shape).astype(jnp.int32)


def _router(x, g, w_router, *, tm):
    n, d = x.shape
    n_exp = w_router.shape[1]
    assert 2 * n_exp <= LANES
    w2 = jnp.pad(jnp.concatenate(_split_bf16(w_router), axis=1), ((0, 0), (0, LANES - 2 * n_exp)))
    tile = lambda w: pl.BlockSpec((tm, w), lambda i: (i, 0))
    return pl.pallas_call(
        functools.partial(_router_body, n_exp=n_exp),
        grid=(n // tm,),
        in_specs=[tile(d), _resident(g.shape), _resident(w2.shape)],
        out_specs=[tile(d), pl.BlockSpec((SUBLANES, tm), lambda i: (i, 0)), tile(LANES),
                   pl.BlockSpec((SUBLANES, LANES), lambda i: (0, 0))],
        out_shape=[jax.ShapeDtypeStruct((n, d), F32), jax.ShapeDtypeStruct((n // tm * SUBLANES, tm), jnp.int32),
                   jax.ShapeDtypeStruct((n, LANES), F32), jax.ShapeDtypeStruct((SUBLANES, LANES), jnp.int32)],
        scratch_shapes=[pltpu.VMEM((1, LANES), F32)],
        compiler_params=_params("arbitrary"),
        name="router",
    )(x, g, w2)


def _to_row_tiles(v, dst_ref):
    c = v.shape[1] // LANES
    for j in range(c):
        dst_ref[pl.ds(j, v.shape[0], stride=c), :] = v[:, j * LANES:(j + 1) * LANES]


def _from_row_tiles(src_ref, rows, c):
    return [src_ref[pl.ds(j, rows, stride=c), :] for j in range(c)]


def _tile_copy(src, src_row8, dst, dst_row8, sem):
    return pltpu.make_async_copy(src.at[pl.ds(pl.multiple_of(src_row8, SUBLANES), SUBLANES)],
                                 dst.at[pl.ds(pl.multiple_of(dst_row8, SUBLANES), SUBLANES)], sem)


def _tiles_wait(hbm, vmem, sem):
    pltpu.make_async_copy(hbm.at[pl.ds(0, vmem.shape[0])], vmem, sem).wait()


def _dispatch_body(ends_ref, cnt_ref, pos_ref, h_ref, xs_hbm, zbuf, stage, zsem, sems, *, tmb):
    i, n = pl.program_id(0), pl.num_programs(0)
    tm = h_ref.shape[0]
    slot = lax.rem(i, 2)

    @pl.when(i == 0)
    def _():
        n_exp = ends_ref.shape[0]
        zbuf[...] = jnp.zeros(zbuf.shape, zbuf.dtype)
        for e in range(2 * n_exp):
            first = ends_ref[e] - tmb if e < n_exp else ends_ref[n_exp - 1] + (e - n_exp) * tmb
            live = cnt_ref[e] > 0 if e < n_exp else first * SUBLANES < xs_hbm.shape[0]

            @pl.when(live)
            def _():
                start = pl.multiple_of(first * SUBLANES, tmb * SUBLANES)
                fill = pltpu.make_async_copy(zbuf, xs_hbm.at[pl.ds(start, tmb * SUBLANES)], zsem)
                fill.start()
                fill.wait()

    _to_row_tiles(h_ref[...], stage.at[slot])

    def scatter(g, c):
        for j in range(SUBLANES):
            for k in range(TOP_K):
                _tile_copy(stage.at[slot], (g * SUBLANES + j) * SUBLANES,
                           xs_hbm, pos_ref[0, 0, k * tm + g * SUBLANES + j],
                           sems.at[slot]).start(priority=(j * TOP_K + k) % N_DMA_QUEUES)
        return c
    lax.fori_loop(0, tm // SUBLANES, scatter, 0)

    def drain(s):
        for _ in range(TOP_K):
            _tiles_wait(xs_hbm, stage.at[s], sems.at[s])

    @pl.when(i > 0)
    def _():
        drain(1 - slot)

    @pl.when(i == n - 1)
    def _():
        drain(slot)


def _dispatch(ends, counts, pos8, h, *, n_rows, tm, tmb):
    n, d = h.shape
    assert d == SUBLANES * LANES and tm % SUBLANES == 0
    return pl.pallas_call(
        functools.partial(_dispatch_body, tmb=tmb),
        grid_spec=pltpu.PrefetchScalarGridSpec(
            num_scalar_prefetch=2,
            grid=(n // tm,),
            in_specs=[pl.BlockSpec((1, 1, TOP_K * tm), lambda i, *_: (i, 0, 0), memory_space=pltpu.SMEM),
                      pl.BlockSpec((tm, d), lambda i, *_: (i, 0))],
            out_specs=pl.BlockSpec(memory_space=pl.ANY),
            scratch_shapes=[pltpu.VMEM((tmb * SUBLANES, LANES), F32), pltpu.VMEM((2, tm * SUBLANES, LANES), F32),
                            pltpu.SemaphoreType.DMA(()), pltpu.SemaphoreType.DMA((2,))],
        ),
        out_shape=jax.ShapeDtypeStruct((n_rows * SUBLANES, LANES), F32),
        compiler_params=_params("arbitrary"),
        name="dispatch",
    )(ends, counts, pos8, h)


def _expert_body(te_ref, nv_ref, tr_ref, xs_ref, wg_hbm, wu_hbm, wd_hbm, y_ref, hb, acc, wgb, wub, wdb, sems, *, fc):
    i = pl.program_id(0)
    tmb, d = hb.shape
    nf = wg_hbm.shape[2] // fc
    n_valid = nv_ref[0]

    def weight_copies(tile, f, slot):
        e = te_ref[tile]
        cols = pl.ds(pl.multiple_of(f * fc, fc), fc)
        return (pltpu.make_async_copy(wg_hbm.at[e, :, cols], wgb.at[slot], sems.at[0, slot]),
                pltpu.make_async_copy(wu_hbm.at[e, :, cols], wub.at[slot], sems.at[1, slot]),
                pltpu.make_async_copy(wd_hbm.at[e, cols, :], wdb.at[slot], sems.at[2, slot]))

    @pl.when(i >= n_valid)
    def _():
        y_ref[...] = jnp.zeros(y_ref.shape, F32)

    ahead = WEIGHT_BUFFERS - 1

    def run_chunk(f, first, last):
        q = i * nf + f
        wraps = f + ahead >= nf
        ahead_tile = jnp.where(wraps, i + 1, i)

        @pl.when(ahead_tile < n_valid)
        def _():
            for cp in weight_copies(ahead_tile, jnp.where(wraps, f + ahead - nf, f + ahead),
                                    lax.rem(q + ahead, WEIGHT_BUFFERS)):
                cp.start()

        slot = lax.rem(q, WEIGHT_BUFFERS)
        for cp in weight_copies(i, f, slot):
            cp.wait()

        def accumulate(m):
            if first:
                h = jnp.concatenate([col.astype(BF16) for col in _from_row_tiles(xs_ref, m, d // LANES)], axis=1)
                hb[0:m, :] = h
            else:
                h = hb[0:m, :]
            contrib = _swiglu(h, wgb[slot].astype(BF16), wub[slot].astype(BF16), wdb[slot].astype(BF16))
            total = contrib if first else acc[0:m, :] + contrib
            if last:
                _to_row_tiles(total, y_ref)
                if m < tmb:
                    y_ref[m * (d // LANES):, :] = jnp.zeros(((tmb - m) * (d // LANES), LANES), F32)
            else:
                acc[0:m, :] = total

        @pl.when(tr_ref[i] > tmb // 2)
        def _():
            accumulate(tmb)

        @pl.when(tr_ref[i] <= tmb // 2)
        def _():
            accumulate(tmb // 2)

    @pl.when(i < n_valid)
    def _():
        @pl.when(i == 0)
        def _():
            for f0 in range(ahead):
                for cp in weight_copies(0, f0, f0):
                    cp.start()

        run_chunk(0, True, False)

        def middle(f, carry):
            run_chunk(f, False, False)
            return carry
        lax.fori_loop(1, nf - 1, middle, 0)
        run_chunk(nf - 1, False, True)


def _experts(tile_expert, n_valid, tile_rows, xs, wg, wu, wd, *, tmb, fc):
    n_tiles = tile_expert.shape[0]
    _, d, ff = wg.shape
    c = d // LANES
    assert ff % fc == 0 and ff // fc >= 2 and xs.shape == (n_tiles * tmb * c, LANES)
    return pl.pallas_call(
        functools.partial(_expert_body, fc=fc),
        grid_spec=pltpu.PrefetchScalarGridSpec(
            num_scalar_prefetch=3,
            grid=(n_tiles,),
            in_specs=[pl.BlockSpec((tmb * c, LANES), lambda i, te, nv, tr: (jnp.minimum(i, nv[0] - 1), 0)),
                      pl.BlockSpec(memory_space=pl.ANY), pl.BlockSpec(memory_space=pl.ANY),
                      pl.BlockSpec(memory_space=pl.ANY)],
            out_specs=pl.BlockSpec((tmb * c, LANES), lambda i, te, nv, tr: (i, 0)),
            scratch_shapes=[pltpu.VMEM((tmb, d), BF16), pltpu.VMEM((tmb, d), F32),
                            pltpu.VMEM((WEIGHT_BUFFERS, d, fc), wg.dtype), pltpu.VMEM((WEIGHT_BUFFERS, d, fc), wu.dtype),
                            pltpu.VMEM((WEIGHT_BUFFERS, fc, d), wd.dtype),
                            pltpu.SemaphoreType.DMA((3, WEIGHT_BUFFERS))],
        ),
        out_shape=jax.ShapeDtypeStruct((n_tiles * tmb * c, LANES), F32),
        compiler_params=_params("arbitrary"),
        name="experts",
    )(tile_expert, n_valid, tile_rows, xs, wg, wu, wd)


def _combine_body(pos_ref, pos_next_ref, x_ref, wt_ref, g_ref, y_hbm, o_ref, yrows, sems, *, final_norm):
    i, n = pl.program_id(0), pl.num_programs(0)
    tm, d = x_ref.shape
    slot = lax.rem(i, 2)

    def gather(p_ref, s):
        def body(g, c):
            for j in range(SUBLANES):
                for k in range(TOP_K):
                    _tile_copy(y_hbm, p_ref[0, 0, k * tm + g * SUBLANES + j],
                               yrows.at[s, k], (g * SUBLANES + j) * SUBLANES,
                               sems.at[s]).start(priority=(j * TOP_K + k) % N_DMA_QUEUES)
            return c
        lax.fori_loop(0, tm // SUBLANES, body, 0)

    @pl.when(i == 0)
    def _():
        gather(pos_ref, 0)

    @pl.when(i + 1 < n)
    def _():
        gather(pos_next_ref, 1 - slot)

    for k in range(TOP_K):
        _tiles_wait(y_hbm, yrows.at[slot, k], sems.at[slot])
    ys = [_from_row_tiles(yrows.at[slot, k], tm, d // LANES) for k in range(TOP_K)]
    cols = []
    for j in range(d // LANES):
        moe = wt_ref[:, 0:1] * ys[0][j]
        for k in range(1, TOP_K):
            moe = moe + wt_ref[:, k:k + 1] * ys[k][j]
        cols.append(moe)
    out = x_ref[...] + jnp.concatenate(cols, axis=1)
    o_ref[...] = _rms(out, g_ref[...]) if final_norm else out


def _combine(pos8, x, wt, g, y, *, tm, final_norm):
    n, d = x.shape
    assert d == SUBLANES * LANES and tm % SUBLANES == 0
    last = n // tm - 1
    tile = pl.BlockSpec((tm, d), lambda i: (i, 0))
    return pl.pallas_call(
        functools.partial(_combine_body, final_norm=final_norm),
        grid=(n // tm,),
        in_specs=[pl.BlockSpec((1, 1, TOP_K * tm), lambda i: (i, 0, 0), memory_space=pltpu.SMEM),
                  pl.BlockSpec((1, 1, TOP_K * tm), lambda i: (jnp.minimum(i + 1, last), 0, 0),
                               memory_space=pltpu.SMEM),
                  tile, pl.BlockSpec((tm, LANES), lambda i: (i, 0)), _resident(g.shape),
                  pl.BlockSpec(memory_space=pl.ANY)],
        out_specs=tile,
        out_shape=jax.ShapeDtypeStruct((n, d), F32),
        scratch_shapes=[pltpu.VMEM((2, TOP_K, tm * SUBLANES, LANES), F32), pltpu.SemaphoreType.DMA((2,))],
        compiler_params=_params("arbitrary"),
        name="combine",
    )(pos8, pos8, x, wt, g, y)


def _routing_tables(idx, cnt, n, n_exp, tmb, tc):
    tm = idx.shape[1]
    idx = idx.reshape(n // tm, SUBLANES, tm)
    experts = jnp.arange(n_exp, dtype=jnp.int32)
    counts = cnt[0, :n_exp]
    padded = ((counts + (tmb - 1)) // tmb) * tmb
    ends = jnp.sum(jnp.where(experts[None, :] <= experts[:, None], padded[None, :], 0), axis=1)
    starts = ends - padded
    chosen = idx[:, 0:TOP_K, :]
    first = sum(jnp.where(chosen == e, starts[e], 0) for e in range(n_exp))
    pos8 = (first + idx[:, TOP_K:2 * TOP_K, :]) * SUBLANES
    pos8 = pos8.reshape(n // tm, TOP_K, tm // tc, tc).transpose(0, 2, 1, 3).reshape(n // tc, 1, TOP_K * tc)
    n_tiles = (n * TOP_K) // tmb + n_exp
    tile_start = jnp.arange(n_tiles, dtype=jnp.int32) * tmb
    tile_expert = jnp.minimum(jnp.sum((tile_start[:, None] >= ends[None, :]).astype(jnp.int32), axis=1), n_exp - 1)
    tile_rows = jnp.clip(sum(jnp.where(tile_expert == e, starts[e] + counts[e], 0) for e in range(n_exp)) - tile_start,
                         0, tmb)
    return pos8, ends, counts, tile_expert, ends[n_exp - 1:] // tmb, tile_rows, n_tiles


def _moe(x, g, w_router, wg, wu, wd, g_out, *, final_norm):
    h, idx, wt, cnt = _router(x, g, w_router, tm=ROUTER_ROW_TILE)
    pos8, ends, counts, tile_expert, n_valid, tile_rows, n_tiles = _routing_tables(
        idx, cnt, x.shape[0], wg.shape[0], EXPERT_ROW_TILE, ROW_TILE)
    xs = _dispatch(ends, counts, pos8, h, n_rows=n_tiles * EXPERT_ROW_TILE, tm=ROW_TILE, tmb=EXPERT_ROW_TILE)
    y = _experts(tile_expert, n_valid, tile_rows, xs, wg, wu, wd, tmb=EXPERT_ROW_TILE, fc=EXPERT_FF_TILE)
    return _combine(pos8, x, wt, g_out, y, tm=ROW_TILE, final_norm=final_norm)


def _final_norm_body(x_ref, g_ref, o_ref):
    o_ref[...] = _rms(x_ref[...], g_ref[...])


def _final_norm(x, g, *, tm):
    n, d = x.shape
    tile = pl.BlockSpec((tm, d), lambda i: (i, 0))
    return pl.pallas_call(
        _final_norm_body, grid=(n // tm,), in_specs=[tile, _resident(g.shape)], out_specs=tile,
        out_shape=jax.ShapeDtypeStruct((n, d), F32), compiler_params=_params("arbitrary"), name="final_norm",
    )(x, g)


def kernel(x, mem, g_mix, w_in, w_pool_group, pool_scale, w_pool_out, conv_w, w_conv_out, w_mix_out, g_xattn, g_mem, w_xq, w_xk, w_xv, w_xo, g_ffn, w_ff_gate, w_ff_up, w_ff_down, w_router, w_e_gate, w_e_up, w_e_down, g_final):
    b, s, d = x.shape
    depth = g_mix.shape[0]
    bf = lambda w: w.astype(BF16)
    row = lambda v: v.reshape(1, -1)

    kt, v = _memkv(mem, g_mem.reshape(depth, 1, d), bf(w_xk), bf(w_xv))
    xf = x.reshape(b * s, d)
    for l in range(depth):
        xf = _mixer(xf, row(g_mix[l]), bf(w_in[l]), bf(w_pool_group[l]), row(pool_scale[l]), bf(w_pool_out[l]),
                    conv_w[l], bf(w_conv_out[l]), bf(w_mix_out[l]), seq=s, tm=MIXER_ROW_TILE)
        xf = _xattn(xf, row(g_xattn[l]), bf(w_xq[l]), kt, v, bf(w_xo[l]), layer=l, seq=s, tm=MIXER_ROW_TILE)
        last = l == depth - 1
        i = l // 2
        if l % 2 == 0:
            xf = _ffn(xf, row(g_ffn[l]), bf(w_ff_gate[i]), bf(w_ff_up[i]), bf(w_ff_down[i]), tm=ROW_TILE)
            if last:
                xf = _final_norm(xf, row(g_final), tm=ROW_TILE)
        else:
            xf = _moe(xf, row(g_ffn[l]), w_router[i], w_e_gate[i], w_e_up[i], w_e_down[i],
                      row(g_final), final_norm=last)
    return xf.reshape(b, s, d)
```

```python
import functools

import jax
import jax.numpy as jnp
from jax import lax
from jax.experimental import pallas as pl
from jax.experimental.pallas import tpu as pltpu

F32 = jnp.float32
BF16 = jnp.bfloat16

EPS = 1e-6
POOL_WINDOWS = (2, 4, 8, 16)
CONV_K = 3
N_XHEADS = 4
TOP_K = 2

LANES = 128
SUBLANES = 8
VMEM_LIMIT_BYTES = 56 * 1024 * 1024
N_DMA_QUEUES = 2

POOL_HALO = 16
CONV_HALO = 8

ROW_TILE = 512
MIXER_ROW_TILE = 1024
ROW_CHUNKS = 2
MIXER_ROW_CHUNKS = 1
ROUTER_ROW_TILE = 1024
EXPERT_ROW_TILE = 1024
EXPERT_FF_TILE = 512
WEIGHT_BUFFERS = 4


def _dot(a, b):
    return jnp.dot(a, b, preferred_element_type=F32)


def _sigmoid(z):
    return 0.5 * jnp.tanh(0.5 * z) + 0.5


def _rms(xf, g):
    ms = jnp.mean(xf * xf, axis=-1, keepdims=True)
    return (xf * lax.rsqrt(ms + EPS)) * g


def _resident(shape):
    nd = len(shape)
    return pl.BlockSpec(shape, lambda *_: (0,) * nd, pipeline_mode=pl.Buffered(1))


def _params(*sem):
    return pltpu.CompilerParams(dimension_semantics=sem, vmem_limit_bytes=VMEM_LIMIT_BYTES)


def _mixer_body(x_ref, g_ref, win_ref, wgrp_ref, ps_ref, wpo_ref, cw_ref, wco_ref, wmo_ref, o_ref,
                pool_buf, conv_buf, *, tm, seq):
    n_groups, _, gw = wgrp_ref.shape
    pw = n_groups * gw
    cwid = cw_ref.shape[1]
    d = x_ref.shape[1]
    o_c, o_b, o_u, o_g = pw, pw + cwid, pw + 2 * cwid, pw + 3 * cwid

    pos0 = lax.rem(pl.program_id(0) * tm, seq)

    @pl.when(pos0 == 0)
    def _():
        pool_buf[:, 0:POOL_HALO, :] = jnp.zeros((n_groups, POOL_HALO, gw), F32)
        conv_buf[:, 0:CONV_HALO, :] = jnp.zeros((conv_buf.shape[0], CONV_HALO, LANES), F32)

    @pl.when(pos0 != 0)
    def _():
        pool_buf[:, 0:POOL_HALO, :] = pool_buf[:, tm:tm + POOL_HALO, :]
        conv_buf[:, 0:CONV_HALO, :] = conv_buf[:, tm:tm + CONV_HALO, :]

    cm = tm // MIXER_ROW_CHUNKS
    for r0 in range(0, tm, cm):
        x = x_ref[r0:r0 + cm, :]
        h = _rms(x, g_ref[...]).astype(BF16)

        u_pool = _dot(h, win_ref[:, 0:o_c])
        ys = []
        for g, w in enumerate(POOL_WINDOWS):
            ug = u_pool[:, g * gw:(g + 1) * gw]
            lo = POOL_HALO + r0
            pool_buf[g, lo:lo + cm, :] = ug
            acc = ug
            for k in range(1, w):
                acc = acc + pool_buf[g, lo - k:lo - k + cm, :]
            if r0 == 0:
                pos1 = lax.broadcasted_iota(jnp.int32, (POOL_HALO, gw), 0) + (pos0 + 1)
                head = acc[0:POOL_HALO] / jnp.minimum(pos1, w).astype(F32)
                mean = jnp.concatenate([head, acc[POOL_HALO:] * (1.0 / w)], axis=0)
            else:
                mean = acc * (1.0 / w)
            ys.append(_dot((mean - ug).astype(BF16), wgrp_ref[g]))
        y = jnp.concatenate(ys, axis=1) * ps_ref[...]
        y_pool = _dot(y.astype(BF16), wpo_ref[...])

        c_gate = _dot(h, win_ref[:, o_c:o_b])
        u_conv = _dot(h, win_ref[:, o_u:o_g])
        zc = c_gate * u_conv
        ycs = []
        for j in range(cwid // LANES):
            sl = slice(j * LANES, (j + 1) * LANES)
            zj = zc[:, sl]
            lo = CONV_HALO + r0
            conv_buf[j, lo:lo + cm, :] = zj
            acc = cw_ref[0:1, sl] * conv_buf[j, lo - (CONV_K - 1):lo - (CONV_K - 1) + cm, :]
            for k in range(1, CONV_K - 1):
                off = lo - (CONV_K - 1) + k
                acc = acc + cw_ref[k:k + 1, sl] * conv_buf[j, off:off + cm, :]
            ycs.append(acc + cw_ref[CONV_K - 1:CONV_K, sl] * zj)
        b_gate = _dot(h, win_ref[:, o_b:o_u])
        y_conv = _dot((b_gate * jnp.concatenate(ycs, axis=1)).astype(BF16), wco_ref[...])

        merged = _sigmoid(_dot(h, win_ref[:, o_g:o_g + d])) * y_pool
        merged = merged + _sigmoid(_dot(h, win_ref[:, o_g + d:o_g + 2 * d])) * y_conv
        o_ref[r0:r0 + cm, :] = x + _dot(merged.astype(BF16), wmo_ref[...])


def _mixer(x, g, win, wgrp, ps, wpo, cw, wco, wmo, *, seq, tm):
    n, d = x.shape
    n_groups, _, gw = wgrp.shape
    assert seq % tm == 0 and n % tm == 0 and gw == LANES and cw.shape[1] % LANES == 0
    assert (tm // MIXER_ROW_CHUNKS) % SUBLANES == 0 and tm // MIXER_ROW_CHUNKS >= POOL_HALO
    assert len(POOL_WINDOWS) == n_groups and max(POOL_WINDOWS) <= POOL_HALO and cw.shape[0] == CONV_K
    assert all(w & (w - 1) == 0 for w in POOL_WINDOWS)
    tile = pl.BlockSpec((tm, d), lambda i: (i, 0))
    return pl.pallas_call(
        functools.partial(_mixer_body, tm=tm, seq=seq),
        grid=(n // tm,),
        in_specs=[tile, _resident(g.shape), _resident(win.shape), _resident(wgrp.shape), _resident(ps.shape),
                  _resident(wpo.shape), _resident(cw.shape), _resident(wco.shape), _resident(wmo.shape)],
        out_specs=tile,
        out_shape=jax.ShapeDtypeStruct((n, d), F32),
        scratch_shapes=[pltpu.VMEM((n_groups, tm + POOL_HALO, gw), F32),
                        pltpu.VMEM((cw.shape[1] // LANES, tm + CONV_HALO, LANES), F32)],
        compiler_params=_params("arbitrary"),
        name="mixer",
    )(x, g, win, wgrp, ps, wpo, cw, wco, wmo)


def _memkv_body(mem_ref, g_ref, wk_ref, wv_ref, kt_ref, v_ref):
    mn = _rms(mem_ref[...], g_ref[...]).astype(BF16)
    kt_ref[...] = _dot(mn, wk_ref[...]).T.astype(BF16)
    v_ref[...] = _dot(mn, wv_ref[...]).astype(BF16)


def _memkv(mem, g_mem, wk, wv):
    b, m, d = mem.shape
    depth = wk.shape[0]
    per_layer = lambda l, i: (l, 0, 0)
    return pl.pallas_call(
        _memkv_body,
        grid=(depth, b),
        in_specs=[pl.BlockSpec((None, m, d), lambda l, i: (i, 0, 0)),
                  pl.BlockSpec((None, 1, d), per_layer),
                  pl.BlockSpec((None, d, d), per_layer),
                  pl.BlockSpec((None, d, d), per_layer)],
        out_specs=[pl.BlockSpec((None, None, d, m), lambda l, i: (l, i, 0, 0)),
                   pl.BlockSpec((None, None, m, d), lambda l, i: (l, i, 0, 0))],
        out_shape=[jax.ShapeDtypeStruct((depth, b, d, m), BF16), jax.ShapeDtypeStruct((depth, b, m, d), BF16)],
        compiler_params=_params("arbitrary", "arbitrary"),
        name="memkv",
    )(mem, g_mem, wk, wv)


def _xattn_body(x_ref, g_ref, wq_ref, kt_ref, v_ref, wo_ref, o_ref):
    tm = x_ref.shape[0]
    for r0 in range(0, tm, tm // ROW_CHUNKS):
        rows = slice(r0, r0 + tm // ROW_CHUNKS)
        x = x_ref[rows, :]
        h = _rms(x, g_ref[...]).astype(BF16)
        q = _dot(h, wq_ref[...])
        dh = q.shape[1] // N_XHEADS
        heads = []
        for hd in range(N_XHEADS):
            sl = slice(hd * dh, (hd + 1) * dh)
            s = _dot(q[:, sl].astype(BF16), kt_ref[sl, :]) * (dh ** -0.5)
            e = jnp.exp(s - jnp.max(s, axis=-1, keepdims=True))
            p = e / jnp.sum(e, axis=-1, keepdims=True)
            heads.append(_dot(p.astype(BF16), v_ref[:, sl]))
        o = jnp.concatenate(heads, axis=1).astype(BF16)
        o_ref[rows, :] = x + _dot(o, wo_ref[...])


def _xattn(x, g, wq, kt, v, wo, *, layer, seq, tm):
    n, d = x.shape
    m = v.shape[2]
    assert seq % tm == 0 and d % N_XHEADS == 0
    tiles_per_seq = seq // tm
    tile = pl.BlockSpec((tm, d), lambda i: (i, 0))
    return pl.pallas_call(
        _xattn_body,
        grid=(n // tm,),
        in_specs=[tile, _resident(g.shape), _resident(wq.shape),
                  pl.BlockSpec((None, None, d, m), lambda i: (layer, i // tiles_per_seq, 0, 0)),
                  pl.BlockSpec((None, None, m, d), lambda i: (layer, i // tiles_per_seq, 0, 0)),
                  _resident(wo.shape)],
        out_specs=tile,
        out_shape=jax.ShapeDtypeStruct((n, d), F32),
        compiler_params=_params("arbitrary"),
        name="xattn",
    )(x, g, wq, kt, v, wo)


def _swiglu(h, wg, wu, wd):
    g = _dot(h, wg)
    return _dot((g * _sigmoid(g) * _dot(h, wu)).astype(BF16), wd)


def _ffn_body(x_ref, g_ref, wg_ref, wu_ref, wd_ref, o_ref):
    tm = x_ref.shape[0]
    for r0 in range(0, tm, tm // ROW_CHUNKS):
        rows = slice(r0, r0 + tm // ROW_CHUNKS)
        x = x_ref[rows, :]
        h = _rms(x, g_ref[...]).astype(BF16)
        o_ref[rows, :] = x + _swiglu(h, wg_ref[...], wu_ref[...], wd_ref[...])


def _ffn(x, g, wg, wu, wd, *, tm):
    n, d = x.shape
    tile = pl.BlockSpec((tm, d), lambda i: (i, 0))
    return pl.pallas_call(
        _ffn_body,
        grid=(n // tm,),
        in_specs=[tile, _resident(g.shape), _resident(wg.shape), _resident(wu.shape), _resident(wd.shape)],
        out_specs=tile,
        out_shape=jax.ShapeDtypeStruct((n, d), F32),
        compiler_params=_params("arbitrary"),
        name="ffn",
    )(x, g, wg, wu, wd)


def _split_bf16(v):
    hi = v.astype(BF16)
    return hi, (v - hi.astype(F32)).astype(BF16)


def _router_body(x_ref, g_ref, w2_ref, h_ref, idx_ref, wt_ref, cnt_ref, base_ref, *, n_exp):
    tm = x_ref.shape[0]

    @pl.when(pl.program_id(0) == 0)
    def _():
        base_ref[...] = jnp.zeros(base_ref.shape, F32)

    h = _rms(x_ref[...], g_ref[...])
    h_ref[...] = h
    h_hi, h_lo = _split_bf16(h)
    part = _dot(h_hi, w2_ref[...]) + _dot(h_lo, w2_ref[...])
    lane = lax.broadcasted_iota(jnp.int32, (tm, LANES), 1)
    logits = jnp.where(lane < n_exp, part + pltpu.roll(part, LANES - n_exp, axis=1), -jnp.inf)
    lane_f = lane.astype(F32)
    m1 = jnp.max(logits, axis=-1, keepdims=True)
    i1 = jnp.min(jnp.where(logits == m1, lane_f, float(LANES)), axis=-1, keepdims=True).astype(jnp.int32)
    rest = jnp.where(lane == i1, -jnp.inf, logits)
    m2 = jnp.max(rest, axis=-1, keepdims=True)
    i2 = jnp.min(jnp.where(rest == m2, lane_f, float(LANES)), axis=-1, keepdims=True).astype(jnp.int32)
    e2 = jnp.exp(m2 - m1)
    den = 1.0 + e2
    wt_ref[...] = jnp.where(lane == 0, 1.0 / den, jnp.where(lane == 1, e2 / den, 0.0))

    oh1, oh2 = lane == i1, lane == i2
    hits = jnp.logical_or(oh1, oh2)
    earlier = lax.broadcasted_iota(jnp.int32, (tm, tm), 1) < lax.broadcasted_iota(jnp.int32, (tm, tm), 0)
    prefix = _dot(earlier.astype(BF16), hits.astype(BF16)) + base_ref[...]
    r1 = jnp.sum(jnp.where(oh1, prefix, 0.0), axis=-1, keepdims=True).astype(jnp.int32)
    r2 = jnp.sum(jnp.where(oh2, prefix, 0.0), axis=-1, keepdims=True).astype(jnp.int32)
    idx = jnp.where(lane == 0, i1, jnp.where(lane == 1, i2, jnp.where(lane == 2, r1, jnp.where(lane == 3, r2, 0))))
    idx_ref[...] = idx.T[0:SUBLANES, :]
    base_ref[...] += jnp.sum(hits.astype(F32), axis=0, keepdims=True)
    cnt_ref[...] = jnp.broadcast_to(base_ref[...], cnt_ref.shape).astype(jnp.int32)


def _router(x, g, w_router, *, tm):
    n, d = x.shape
    n_exp = w_router.shape[1]
    assert 2 * n_exp <= LANES
    w2 = jnp.pad(jnp.concatenate(_split_bf16(w_router), axis=1), ((0, 0), (0, LANES - 2 * n_exp)))
    tile = lambda w: pl.BlockSpec((tm, w), lambda i: (i, 0))
    return pl.pallas_call(
        functools.partial(_router_body, n_exp=n_exp),
        grid=(n // tm,),
        in_specs=[tile(d), _resident(g.shape), _resident(w2.shape)],
        out_specs=[tile(d), pl.BlockSpec((SUBLANES, tm), lambda i: (i, 0)), tile(LANES),
                   pl.BlockSpec((SUBLANES, LANES), lambda i: (0, 0))],
        out_shape=[jax.ShapeDtypeStruct((n, d), F32), jax.ShapeDtypeStruct((n // tm * SUBLANES, tm), jnp.int32),
                   jax.ShapeDtypeStruct((n, LANES), F32), jax.ShapeDtypeStruct((SUBLANES, LANES), jnp.int32)],
        scratch_shapes=[pltpu.VMEM((1, LANES), F32)],
        compiler_params=_params("arbitrary"),
        name="router",
    )(x, g, w2)


def _to_row_tiles(v, dst_ref):
    c = v.shape[1] // LANES
    for j in range(c):
        dst_ref[pl.ds(j, v.shape[0], stride=c), :] = v[:, j * LANES:(j + 1) * LANES]


def _from_row_tiles(src_ref, rows, c):
    return [src_ref[pl.ds(j, rows, stride=c), :] for j in range(c)]


def _tile_copy(src, src_row8, dst, dst_row8, sem):
    return pltpu.make_async_copy(src.at[pl.ds(pl.multiple_of(src_row8, SUBLANES), SUBLANES)],
                                 dst.at[pl.ds(pl.multiple_of(dst_row8, SUBLANES), SUBLANES)], sem)


def _tiles_wait(hbm, vmem, sem):
    pltpu.make_async_copy(hbm.at[pl.ds(0, vmem.shape[0])], vmem, sem).wait()


def _dispatch_body(ends_ref, cnt_ref, pos_ref, h_ref, xs_hbm, zbuf, stage, zsem, sems, *, tmb):
    i, n = pl.program_id(0), pl.num_programs(0)
    tm = h_ref.shape[0]
    slot = lax.rem(i, 2)

    @pl.when(i == 0)
    def _():
        n_exp = ends_ref.shape[0]
        zbuf[...] = jnp.zeros(zbuf.shape, zbuf.dtype)
        for e in range(2 * n_exp):
            first = ends_ref[e] - tmb if e < n_exp else ends_ref[n_exp - 1] + (e - n_exp) * tmb
            live = cnt_ref[e] > 0 if e < n_exp else first * SUBLANES < xs_hbm.shape[0]

            @pl.when(live)
            def _():
                start = pl.multiple_of(first * SUBLANES, tmb * SUBLANES)
                fill = pltpu.make_async_copy(zbuf, xs_hbm.at[pl.ds(start, tmb * SUBLANES)], zsem)
                fill.start()
                fill.wait()

    _to_row_tiles(h_ref[...], stage.at[slot])

    def scatter(g, c):
        for j in range(SUBLANES):
            for k in range(TOP_K):
                _tile_copy(stage.at[slot], (g * SUBLANES + j) * SUBLANES,
                           xs_hbm, pos_ref[0, 0, k * tm + g * SUBLANES + j],
                           sems.at[slot]).start(priority=(j * TOP_K + k) % N_DMA_QUEUES)
        return c
    lax.fori_loop(0, tm // SUBLANES, scatter, 0)

    def drain(s):
        for _ in range(TOP_K):
            _tiles_wait(xs_hbm, stage.at[s], sems.at[s])

    @pl.when(i > 0)
    def _():
        drain(1 - slot)

    @pl.when(i == n - 1)
    def _():
        drain(slot)


def _dispatch(ends, counts, pos8, h, *, n_rows, tm, tmb):
    n, d = h.shape
    assert d == SUBLANES * LANES and tm % SUBLANES == 0
    return pl.pallas_call(
        functools.partial(_dispatch_body, tmb=tmb),
        grid_spec=pltpu.PrefetchScalarGridSpec(
            num_scalar_prefetch=2,
            grid=(n // tm,),
            in_specs=[pl.BlockSpec((1, 1, TOP_K * tm), lambda i, *_: (i, 0, 0), memory_space=pltpu.SMEM),
                      pl.BlockSpec((tm, d), lambda i, *_: (i, 0))],
            out_specs=pl.BlockSpec(memory_space=pl.ANY),
            scratch_shapes=[pltpu.VMEM((tmb * SUBLANES, LANES), F32), pltpu.VMEM((2, tm * SUBLANES, LANES), F32),
                            pltpu.SemaphoreType.DMA(()), pltpu.SemaphoreType.DMA((2,))],
        ),
        out_shape=jax.ShapeDtypeStruct((n_rows * SUBLANES, LANES), F32),
        compiler_params=_params("arbitrary"),
        name="dispatch",
    )(ends, counts, pos8, h)


def _expert_body(te_ref, nv_ref, tr_ref, xs_ref, wg_hbm, wu_hbm, wd_hbm, y_ref, hb, acc, wgb, wub, wdb, sems, *, fc):
    i = pl.program_id(0)
    tmb, d = hb.shape
    nf = wg_hbm.shape[2] // fc
    n_valid = nv_ref[0]

    def weight_copies(tile, f, slot):
        e = te_ref[tile]
        cols = pl.ds(pl.multiple_of(f * fc, fc), fc)
        return (pltpu.make_async_copy(wg_hbm.at[e, :, cols], wgb.at[slot], sems.at[0, slot]),
                pltpu.make_async_copy(wu_hbm.at[e, :, cols], wub.at[slot], sems.at[1, slot]),
                pltpu.make_async_copy(wd_hbm.at[e, cols, :], wdb.at[slot], sems.at[2, slot]))

    @pl.when(i >= n_valid)
    def _():
        y_ref[...] = jnp.zeros(y_ref.shape, F32)

    ahead = WEIGHT_BUFFERS - 1

    def run_chunk(f, first, last):
        q = i * nf + f
        wraps = f + ahead >= nf
        ahead_tile = jnp.where(wraps, i + 1, i)

        @pl.when(ahead_tile < n_valid)
        def _():
            for cp in weight_copies(ahead_tile, jnp.where(wraps, f + ahead - nf, f + ahead),
                                    lax.rem(q + ahead, WEIGHT_BUFFERS)):
                cp.start()

        slot = lax.rem(q, WEIGHT_BUFFERS)
        for cp in weight_copies(i, f, slot):
            cp.wait()

        def accumulate(m):
            if first:
                h = jnp.concatenate([col.astype(BF16) for col in _from_row_tiles(xs_ref, m, d // LANES)], axis=1)
                hb[0:m, :] = h
            else:
                h = hb[0:m, :]
            contrib = _swiglu(h, wgb[slot].astype(BF16), wub[slot].astype(BF16), wdb[slot].astype(BF16))
            total = contrib if first else acc[0:m, :] + contrib
            if last:
                _to_row_tiles(total, y_ref)
                if m < tmb:
                    y_ref[m * (d // LANES):, :] = jnp.zeros(((tmb - m) * (d // LANES), LANES), F32)
            else:
                acc[0:m, :] = total

        @pl.when(tr_ref[i] > tmb // 2)
        def _():
            accumulate(tmb)

        @pl.when(tr_ref[i] <= tmb // 2)
        def _():
            accumulate(tmb // 2)

    @pl.when(i < n_valid)
    def _():
        @pl.when(i == 0)
        def _():
            for f0 in range(ahead):
                for cp in weight_copies(0, f0, f0):
                    cp.start()

        run_chunk(0, True, False)

        def middle(f, carry):
            run_chunk(f, False, False)
            return carry
        lax.fori_loop(1, nf - 1, middle, 0)
        run_chunk(nf - 1, False, True)


def _experts(tile_expert, n_valid, tile_rows, xs, wg, wu, wd, *, tmb, fc):
    n_tiles = tile_expert.shape[0]
    _, d, ff = wg.shape
    c = d // LANES
    assert ff % fc == 0 and ff // fc >= 2 and xs.shape == (n_tiles * tmb * c, LANES)
    return pl.pallas_call(
        functools.partial(_expert_body, fc=fc),
        grid_spec=pltpu.PrefetchScalarGridSpec(
            num_scalar_prefetch=3,
            grid=(n_tiles,),
            in_specs=[pl.BlockSpec((tmb * c, LANES), lambda i, te, nv, tr: (jnp.minimum(i, nv[0] - 1), 0)),
                      pl.BlockSpec(memory_space=pl.ANY), pl.BlockSpec(memory_space=pl.ANY),
                      pl.BlockSpec(memory_space=pl.ANY)],
            out_specs=pl.BlockSpec((tmb * c, LANES), lambda i, te, nv, tr: (i, 0)),
            scratch_shapes=[pltpu.VMEM((tmb, d), BF16), pltpu.VMEM((tmb, d), F32),
                            pltpu.VMEM((WEIGHT_BUFFERS, d, fc), wg.dtype), pltpu.VMEM((WEIGHT_BUFFERS, d, fc), wu.dtype),
                            pltpu.VMEM((WEIGHT_BUFFERS, fc, d), wd.dtype),
                            pltpu.SemaphoreType.DMA((3, WEIGHT_BUFFERS))],
        ),
        out_shape=jax.ShapeDtypeStruct((n_tiles * tmb * c, LANES), F32),
        compiler_params=_params("arbitrary"),
        name="experts",
    )(tile_expert, n_valid, tile_rows, xs, wg, wu, wd)


def _combine_body(pos_ref, pos_next_ref, x_ref, wt_ref, g_ref, y_hbm, o_ref, yrows, sems, *, final_norm):
    i, n = pl.program_id(0), pl.num_programs(0)
    tm, d = x_ref.shape
    slot = lax.rem(i, 2)

    def gather_group(p_ref, s, g):
        for j in range(SUBLANES):
            for k in range(TOP_K):
                _tile_copy(y_hbm, p_ref[0, 0, k * tm + g * SUBLANES + j],
                           yrows.at[s, k], (g * SUBLANES + j) * SUBLANES,
                           sems.at[s]).start(priority=(j * TOP_K + k) % N_DMA_QUEUES)

    def combine():
        ys = [_from_row_tiles(yrows.at[slot, k], tm, d // LANES) for k in range(TOP_K)]
        cols = []
        for j in range(d // LANES):
            moe = wt_ref[:, 0:1] * ys[0][j]
            for k in range(1, TOP_K):
                moe = moe + wt_ref[:, k:k + 1] * ys[k][j]
            cols.append(moe)
        out = x_ref[...] + jnp.concatenate(cols, axis=1)
        o_ref[...] = _rms(out, g_ref[...]) if final_norm else out

    @pl.when(i == 0)
    def _():
        def first(g, c):
            gather_group(pos_ref, 0, g)
            return c
        lax.fori_loop(0, tm // SUBLANES, first, 0)

    for k in range(TOP_K):
        _tiles_wait(y_hbm, yrows.at[slot, k], sems.at[slot])

    @pl.when(i + 1 < n)
    def _():
        for g in range(tm // SUBLANES):
            gather_group(pos_next_ref, 1 - slot, g)
        combine()

    @pl.when(i + 1 == n)
    def _():
        combine()


def _combine(pos8, x, wt, g, y, *, tm, final_norm):
    n, d = x.shape
    assert d == SUBLANES * LANES and tm % SUBLANES == 0
    last = n // tm - 1
    tile = pl.BlockSpec((tm, d), lambda i: (i, 0))
    return pl.pallas_call(
        functools.partial(_combine_body, final_norm=final_norm),
        grid=(n // tm,),
        in_specs=[pl.BlockSpec((1, 1, TOP_K * tm), lambda i: (i, 0, 0), memory_space=pltpu.SMEM),
                  pl.BlockSpec((1, 1, TOP_K * tm), lambda i: (jnp.minimum(i + 1, last), 0, 0),
                               memory_space=pltpu.SMEM),
                  tile, pl.BlockSpec((tm, LANES), lambda i: (i, 0)), _resident(g.shape),
                  pl.BlockSpec(memory_space=pl.ANY)],
        out_specs=tile,
        out_shape=jax.ShapeDtypeStruct((n, d), F32),
        scratch_shapes=[pltpu.VMEM((2, TOP_K, tm * SUBLANES, LANES), F32), pltpu.SemaphoreType.DMA((2,))],
        compiler_params=_params("arbitrary"),
        name="combine",
    )(pos8, pos8, x, wt, g, y)


def _routing_tables(idx, cnt, n, n_exp, tmb, tc):
    tm = idx.shape[1]
    idx = idx.reshape(n // tm, SUBLANES, tm)
    experts = jnp.arange(n_exp, dtype=jnp.int32)
    counts = cnt[0, :n_exp]
    padded = ((counts + (tmb - 1)) // tmb) * tmb
    ends = jnp.sum(jnp.where(experts[None, :] <= experts[:, None], padded[None, :], 0), axis=1)
    starts = ends - padded
    chosen = idx[:, 0:TOP_K, :]
    first = sum(jnp.where(chosen == e, starts[e], 0) for e in range(n_exp))
    pos8 = (first + idx[:, TOP_K:2 * TOP_K, :]) * SUBLANES
    pos8 = pos8.reshape(n // tm, TOP_K, tm // tc, tc).transpose(0, 2, 1, 3).reshape(n // tc, 1, TOP_K * tc)
    n_tiles = (n * TOP_K) // tmb + n_exp
    tile_start = jnp.arange(n_tiles, dtype=jnp.int32) * tmb
    tile_expert = jnp.minimum(jnp.sum((tile_start[:, None] >= ends[None, :]).astype(jnp.int32), axis=1), n_exp - 1)
    tile_rows = jnp.clip(sum(jnp.where(tile_expert == e, starts[e] + counts[e], 0) for e in range(n_exp)) - tile_start,
                         0, tmb)
    return pos8, ends, counts, tile_expert, ends[n_exp - 1:] // tmb, tile_rows, n_tiles


def _moe(x, g, w_router, wg, wu, wd, g_out, *, final_norm):
    h, idx, wt, cnt = _router(x, g, w_router, tm=ROUTER_ROW_TILE)
    pos8, ends, counts, tile_expert, n_valid, tile_rows, n_tiles = _routing_tables(
        idx, cnt, x.shape[0], wg.shape[0], EXPERT_ROW_TILE, ROW_TILE)
    xs = _dispatch(ends, counts, pos8, h, n_rows=n_tiles * EXPERT_ROW_TILE, tm=ROW_TILE, tmb=EXPERT_ROW_TILE)
    y = _experts(tile_expert, n_valid, tile_rows, xs, wg, wu, wd, tmb=EXPERT_ROW_TILE, fc=EXPERT_FF_TILE)
    return _combine(pos8, x, wt, g_out, y, tm=ROW_TILE, final_norm=final_norm)


def _final_norm_body(x_ref, g_ref, o_ref):
    o_ref[...] = _rms(x_ref[...], g_ref[...])


def _final_norm(x, g, *, tm):
    n, d = x.shape
    tile = pl.BlockSpec((tm, d), lambda i: (i, 0))
    return pl.pallas_call(
        _final_norm_body, grid=(n // tm,), in_specs=[tile, _resident(g.shape)], out_specs=tile,
        out_shape=jax.ShapeDtypeStruct((n, d), F32), compiler_params=_params("arbitrary"), name="final_norm",
    )(x, g)


def kernel(x, mem, g_mix, w_in, w_pool_group, pool_scale, w_pool_out, conv_w, w_conv_out, w_mix_out, g_xattn, g_mem, w_xq, w_xk, w_xv, w_xo, g_ffn, w_ff_gate, w_ff_up, w_ff_down, w_router, w_e_gate, w_e_up, w_e_down, g_final):
    b, s, d = x.shape
    depth = g_mix.shape[0]
    bf = lambda w: w.astype(BF16)
    row = lambda v: v.reshape(1, -1)

    kt, v = _memkv(mem, g_mem.reshape(depth, 1, d), bf(w_xk), bf(w_xv))
    xf = x.reshape(b * s, d)
    for l in range(depth):
        xf = _mixer(xf, row(g_mix[l]), bf(w_in[l]), bf(w_pool_group[l]), row(pool_scale[l]), bf(w_pool_out[l]),
                    conv_w[l], bf(w_conv_out[l]), bf(w_mix_out[l]), seq=s, tm=MIXER_ROW_TILE)
        xf = _xattn(xf, row(g_xattn[l]), bf(w_xq[l]), kt, v, bf(w_xo[l]), layer=l, seq=s, tm=MIXER_ROW_TILE)
        last = l == depth - 1
        i = l // 2
        if l % 2 == 0:
            xf = _ffn(xf, row(g_ffn[l]), bf(w_ff_gate[i]), bf(w_ff_up[i]), bf(w_ff_down[i]), tm=ROW_TILE)
            if last:
                xf = _final_norm(xf, row(g_final), tm=ROW_TILE)
        else:
            xf = _moe(xf, row(g_ffn[l]), w_router[i], w_e_gate[i], w_e_up[i], w_e_down[i],
                      row(g_final), final_norm=last)
    return xf.reshape(b, s, d)
```

```python
import functools

import jax
import jax.numpy as jnp
from jax import lax
from jax.experimental import pallas as pl
from jax.experimental.pallas import tpu as pltpu

F32 = jnp.float32
BF16 = jnp.bfloat16

EPS = 1e-6
POOL_WINDOWS = (2, 4, 8, 16)
CONV_K = 3
N_XHEADS = 4
TOP_K = 2

LANES = 128
SUBLANES = 8
VMEM_LIMIT_BYTES = 56 * 1024 * 1024
N_DMA_QUEUES = 2

POOL_HALO = 16
CONV_HALO = 8

ROW_TILE = 512
MIXER_ROW_TILE = 1024
ROW_CHUNKS = 2
MIXER_ROW_CHUNKS = 1
ROUTER_ROW_TILE = 1024
EXPERT_ROW_TILE = 1024
EXPERT_FF_TILE = 512
WEIGHT_BUFFERS = 4
EXPERT_ROW_FRACTIONS = (1, 2, 4)


def _dot(a, b):
    return jnp.dot(a, b, preferred_element_type=F32)


def _sigmoid(z):
    return 0.5 * jnp.tanh(0.5 * z) + 0.5


def _rms(xf, g):
    ms = jnp.mean(xf * xf, axis=-1, keepdims=True)
    return (xf * lax.rsqrt(ms + EPS)) * g


def _resident(shape):
    nd = len(shape)
    return pl.BlockSpec(shape, lambda *_: (0,) * nd, pipeline_mode=pl.Buffered(1))


def _params(*sem):
    return pltpu.CompilerParams(dimension_semantics=sem, vmem_limit_bytes=VMEM_LIMIT_BYTES)


def _mixer_body(x_ref, g_ref, win_ref, wgrp_ref, ps_ref, wpo_ref, cw_ref, wco_ref, wmo_ref, o_ref,
                pool_buf, conv_buf, *, tm, seq):
    n_groups, _, gw = wgrp_ref.shape
    pw = n_groups * gw
    cwid = cw_ref.shape[1]
    d = x_ref.shape[1]
    o_c, o_b, o_u, o_g = pw, pw + cwid, pw + 2 * cwid, pw + 3 * cwid

    pos0 = lax.rem(pl.program_id(0) * tm, seq)

    @pl.when(pos0 == 0)
    def _():
        pool_buf[:, 0:POOL_HALO, :] = jnp.zeros((n_groups, POOL_HALO, gw), F32)
        conv_buf[:, 0:CONV_HALO, :] = jnp.zeros((conv_buf.shape[0], CONV_HALO, LANES), F32)

    @pl.when(pos0 != 0)
    def _():
        pool_buf[:, 0:POOL_HALO, :] = pool_buf[:, tm:tm + POOL_HALO, :]
        conv_buf[:, 0:CONV_HALO, :] = conv_buf[:, tm:tm + CONV_HALO, :]

    cm = tm // MIXER_ROW_CHUNKS
    for r0 in range(0, tm, cm):
        x = x_ref[r0:r0 + cm, :]
        h = _rms(x, g_ref[...]).astype(BF16)

        u_pool = _dot(h, win_ref[:, 0:o_c])
        ys = []
        for g, w in enumerate(POOL_WINDOWS):
            ug = u_pool[:, g * gw:(g + 1) * gw]
            lo = POOL_HALO + r0
            pool_buf[g, lo:lo + cm, :] = ug
            acc = ug
            for k in range(1, w):
                acc = acc + pool_buf[g, lo - k:lo - k + cm, :]
            if r0 == 0:
                pos1 = lax.broadcasted_iota(jnp.int32, (POOL_HALO, gw), 0) + (pos0 + 1)
                head = acc[0:POOL_HALO] / jnp.minimum(pos1, w).astype(F32)
                mean = jnp.concatenate([head, acc[POOL_HALO:] * (1.0 / w)], axis=0)
            else:
                mean = acc * (1.0 / w)
            ys.append(_dot((mean - ug).astype(BF16), wgrp_ref[g]))
        y = jnp.concatenate(ys, axis=1) * ps_ref[...]
        y_pool = _dot(y.astype(BF16), wpo_ref[...])

        c_gate = _dot(h, win_ref[:, o_c:o_b])
        u_conv = _dot(h, win_ref[:, o_u:o_g])
        zc = c_gate * u_conv
        ycs = []
        for j in range(cwid // LANES):
            sl = slice(j * LANES, (j + 1) * LANES)
            zj = zc[:, sl]
            lo = CONV_HALO + r0
            conv_buf[j, lo:lo + cm, :] = zj
            acc = cw_ref[0:1, sl] * conv_buf[j, lo - (CONV_K - 1):lo - (CONV_K - 1) + cm, :]
            for k in range(1, CONV_K - 1):
                off = lo - (CONV_K - 1) + k
                acc = acc + cw_ref[k:k + 1, sl] * conv_buf[j, off:off + cm, :]
            ycs.append(acc + cw_ref[CONV_K - 1:CONV_K, sl] * zj)
        b_gate = _dot(h, win_ref[:, o_b:o_u])
        y_conv = _dot((b_gate * jnp.concatenate(ycs, axis=1)).astype(BF16), wco_ref[...])

        merged = _sigmoid(_dot(h, win_ref[:, o_g:o_g + d])) * y_pool
        merged = merged + _sigmoid(_dot(h, win_ref[:, o_g + d:o_g + 2 * d])) * y_conv
        o_ref[r0:r0 + cm, :] = x + _dot(merged.astype(BF16), wmo_ref[...])


def _mixer(x, g, win, wgrp, ps, wpo, cw, wco, wmo, *, seq, tm):
    n, d = x.shape
    n_groups, _, gw = wgrp.shape
    assert seq % tm == 0 and n % tm == 0 and gw == LANES and cw.shape[1] % LANES == 0
    assert (tm // MIXER_ROW_CHUNKS) % SUBLANES == 0 and tm // MIXER_ROW_CHUNKS >= POOL_HALO
    assert len(POOL_WINDOWS) == n_groups and max(POOL_WINDOWS) <= POOL_HALO and cw.shape[0] == CONV_K
    assert all(w & (w - 1) == 0 for w in POOL_WINDOWS)
    tile = pl.BlockSpec((tm, d), lambda i: (i, 0))
    return pl.pallas_call(
        functools.partial(_mixer_body, tm=tm, seq=seq),
        grid=(n // tm,),
        in_specs=[tile, _resident(g.shape), _resident(win.shape), _resident(wgrp.shape), _resident(ps.shape),
                  _resident(wpo.shape), _resident(cw.shape), _resident(wco.shape), _resident(wmo.shape)],
        out_specs=tile,
        out_shape=jax.ShapeDtypeStruct((n, d), F32),
        scratch_shapes=[pltpu.VMEM((n_groups, tm + POOL_HALO, gw), F32),
                        pltpu.VMEM((cw.shape[1] // LANES, tm + CONV_HALO, LANES), F32)],
        compiler_params=_params("arbitrary"),
        name="mixer",
    )(x, g, win, wgrp, ps, wpo, cw, wco, wmo)


def _memkv_body(mem_ref, g_ref, wk_ref, wv_ref, kt_ref, v_ref):
    mn = _rms(mem_ref[...], g_ref[...]).astype(BF16)
    kt_ref[...] = _dot(mn, wk_ref[...]).T.astype(BF16)
    v_ref[...] = _dot(mn, wv_ref[...]).astype(BF16)


def _memkv(mem, g_mem, wk, wv):
    b, m, d = mem.shape
    depth = wk.shape[0]
    per_layer = lambda l, i: (l, 0, 0)
    return pl.pallas_call(
        _memkv_body,
        grid=(depth, b),
        in_specs=[pl.BlockSpec((None, m, d), lambda l, i: (i, 0, 0)),
                  pl.BlockSpec((None, 1, d), per_layer),
                  pl.BlockSpec((None, d, d), per_layer),
                  pl.BlockSpec((None, d, d), per_layer)],
        out_specs=[pl.BlockSpec((None, None, d, m), lambda l, i: (l, i, 0, 0)),
                   pl.BlockSpec((None, None, m, d), lambda l, i: (l, i, 0, 0))],
        out_shape=[jax.ShapeDtypeStruct((depth, b, d, m), BF16), jax.ShapeDtypeStruct((depth, b, m, d), BF16)],
        compiler_params=_params("arbitrary", "arbitrary"),
        name="memkv",
    )(mem, g_mem, wk, wv)


def _xattn_body(x_ref, g_ref, wq_ref, kt_ref, v_ref, wo_ref, o_ref):
    tm = x_ref.shape[0]
    for r0 in range(0, tm, tm // ROW_CHUNKS):
        rows = slice(r0, r0 + tm // ROW_CHUNKS)
        x = x_ref[rows, :]
        h = _rms(x, g_ref[...]).astype(BF16)
        q = _dot(h, wq_ref[...])
        dh = q.shape[1] // N_XHEADS
        heads = []
        for hd in range(N_XHEADS):
            sl = slice(hd * dh, (hd + 1) * dh)
            s = _dot(q[:, sl].astype(BF16), kt_ref[sl, :]) * (dh ** -0.5)
            e = jnp.exp(s - jnp.max(s, axis=-1, keepdims=True))
            p = e / jnp.sum(e, axis=-1, keepdims=True)
            heads.append(_dot(p.astype(BF16), v_ref[:, sl]))
        o = jnp.concatenate(heads, axis=1).astype(BF16)
        o_ref[rows, :] = x + _dot(o, wo_ref[...])


def _xattn(x, g, wq, kt, v, wo, *, layer, seq, tm):
    n, d = x.shape
    m = v.shape[2]
    assert seq % tm == 0 and d % N_XHEADS == 0
    tiles_per_seq = seq // tm
    tile = pl.BlockSpec((tm, d), lambda i: (i, 0))
    return pl.pallas_call(
        _xattn_body,
        grid=(n // tm,),
        in_specs=[tile, _resident(g.shape), _resident(wq.shape),
                  pl.BlockSpec((None, None, d, m), lambda i: (layer, i // tiles_per_seq, 0, 0)),
                  pl.BlockSpec((None, None, m, d), lambda i: (layer, i // tiles_per_seq, 0, 0)),
                  _resident(wo.shape)],
        out_specs=tile,
        out_shape=jax.ShapeDtypeStruct((n, d), F32),
        compiler_params=_params("arbitrary"),
        name="xattn",
    )(x, g, wq, kt, v, wo)


def _swiglu(h, wg, wu, wd):
    g = _dot(h, wg)
    return _dot((g * _sigmoid(g) * _dot(h, wu)).astype(BF16), wd)


def _ffn_body(x_ref, g_ref, wg_ref, wu_ref, wd_ref, o_ref):
    tm = x_ref.shape[0]
    for r0 in range(0, tm, tm // ROW_CHUNKS):
        rows = slice(r0, r0 + tm // ROW_CHUNKS)
        x = x_ref[rows, :]
        h = _rms(x, g_ref[...]).astype(BF16)
        o_ref[rows, :] = x + _swiglu(h, wg_ref[...], wu_ref[...], wd_ref[...])


def _ffn(x, g, wg, wu, wd, *, tm):
    n, d = x.shape
    tile = pl.BlockSpec((tm, d), lambda i: (i, 0))
    return pl.pallas_call(
        _ffn_body,
        grid=(n // tm,),
        in_specs=[tile, _resident(g.shape), _resident(wg.shape), _resident(wu.shape), _resident(wd.shape)],
        out_specs=tile,
        out_shape=jax.ShapeDtypeStruct((n, d), F32),
        compiler_params=_params("arbitrary"),
        name="ffn",
    )(x, g, wg, wu, wd)


def _split_bf16(v):
    hi = v.astype(BF16)
    return hi, (v - hi.astype(F32)).astype(BF16)


def _router_body(x_ref, g_ref, w2_ref, h_ref, idx_ref, wt_ref, cnt_ref, base_ref, *, n_exp):
    tm = x_ref.shape[0]

    @pl.when(pl.program_id(0) == 0)
    def _():
        base_ref[...] = jnp.zeros(base_ref.shape, F32)

    h = _rms(x_ref[...], g_ref[...])
    h_ref[...] = h
    h_hi, h_lo = _split_bf16(h)
    part = _dot(h_hi, w2_ref[...]) + _dot(h_lo, w2_ref[...])
    lane = lax.broadcasted_iota(jnp.int32, (tm, LANES), 1)
    logits = jnp.where(lane < n_exp, part + pltpu.roll(part, LANES - n_exp, axis=1), -jnp.inf)
    lane_f = lane.astype(F32)
    m1 = jnp.max(logits, axis=-1, keepdims=True)
    i1 = jnp.min(jnp.where(logits == m1, lane_f, float(LANES)), axis=-1, keepdims=True).astype(jnp.int32)
    rest = jnp.where(lane == i1, -jnp.inf, logits)
    m2 = jnp.max(rest, axis=-1, keepdims=True)
    i2 = jnp.min(jnp.where(rest == m2, lane_f, float(LANES)), axis=-1, keepdims=True).astype(jnp.int32)
    e2 = jnp.exp(m2 - m1)
    den = 1.0 + e2
    wt_ref[...] = jnp.where(lane == 0, 1.0 / den, jnp.where(lane == 1, e2 / den, 0.0))

    oh1, oh2 = lane == i1, lane == i2
    hits = jnp.logical_or(oh1, oh2)
    earlier = lax.broadcasted_iota(jnp.int32, (tm, tm), 1) < lax.broadcasted_iota(jnp.int32, (tm, tm), 0)
    prefix = _dot(earlier.astype(BF16), hits.astype(BF16)) + base_ref[...]
    r1 = jnp.sum(jnp.where(oh1, prefix, 0.0), axis=-1, keepdims=True).astype(jnp.int32)
    r2 = jnp.sum(jnp.where(oh2, prefix, 0.0), axis=-1, keepdims=True).astype(jnp.int32)
    idx = jnp.where(lane == 0, i1, jnp.where(lane == 1, i2, jnp.where(lane == 2, r1, jnp.where(lane == 3, r2, 0))))
    idx_ref[...] = idx.T[0:SUBLANES, :]
    base_ref[...] += jnp.sum(hits.astype(F32), axis=0, keepdims=True)
    cnt_ref[...] = jnp.broadcast_to(base_ref[...], cnt_ref.shape).astype(jnp.int32)


def _router(x, g, w_router, *, tm):
    n, d = x.shape
    n_exp = w_router.shape[1]
    assert 2 * n_exp <= LANES
    w2 = jnp.pad(jnp.concatenate(_split_bf16(w_router), axis=1), ((0, 0), (0, LANES - 2 * n_exp)))
    tile = lambda w: pl.BlockSpec((tm, w), lambda i: (i, 0))
    return pl.pallas_call(
        functools.partial(_router_body, n_exp=n_exp),
        grid=(n // tm,),
        in_specs=[tile(d), _resident(g.shape), _resident(w2.shape)],
        out_specs=[tile(d), pl.BlockSpec((SUBLANES, tm), lambda i: (i, 0)), tile(LANES),
                   pl.BlockSpec((SUBLANES, LANES), lambda i: (0, 0))],
        out_shape=[jax.ShapeDtypeStruct((n, d), F32), jax.ShapeDtypeStruct((n // tm * SUBLANES, tm), jnp.int32),
                   jax.ShapeDtypeStruct((n, LANES), F32), jax.ShapeDtypeStruct((SUBLANES, LANES), jnp.int32)],
        scratch_shapes=[pltpu.VMEM((1, LANES), F32)],
        compiler_params=_params("arbitrary"),
        name="router",
    )(x, g, w2)


def _to_row_tiles(v, dst_ref):
    c = v.shape[1] // LANES
    for j in range(c):
        dst_ref[pl.ds(j, v.shape[0], stride=c), :] = v[:, j * LANES:(j + 1) * LANES]


def _from_row_tiles(src_ref, rows, c):
    return [src_ref[pl.ds(j, rows, stride=c), :] for j in range(c)]


def _tile_copy(src, src_row8, dst, dst_row8, sem):
    return pltpu.make_async_copy(src.at[pl.ds(pl.multiple_of(src_row8, SUBLANES), SUBLANES)],
                                 dst.at[pl.ds(pl.multiple_of(dst_row8, SUBLANES), SUBLANES)], sem)


def _tiles_wait(hbm, vmem, sem):
    pltpu.make_async_copy(hbm.at[pl.ds(0, vmem.shape[0])], vmem, sem).wait()


def _dispatch_body(ends_ref, cnt_ref, pos_ref, h_ref, xs_hbm, zbuf, stage, zsem, sems, *, tmb):
    i, n = pl.program_id(0), pl.num_programs(0)
    tm = h_ref.shape[0]
    slot = lax.rem(i, 2)

    @pl.when(i == 0)
    def _():
        n_exp = ends_ref.shape[0]
        zbuf[...] = jnp.zeros(zbuf.shape, zbuf.dtype)
        for e in range(2 * n_exp):
            first = ends_ref[e] - tmb if e < n_exp else ends_ref[n_exp - 1] + (e - n_exp) * tmb
            live = cnt_ref[e] > 0 if e < n_exp else first * SUBLANES < xs_hbm.shape[0]

            @pl.when(live)
            def _():
                start = pl.multiple_of(first * SUBLANES, tmb * SUBLANES)
                fill = pltpu.make_async_copy(zbuf, xs_hbm.at[pl.ds(start, tmb * SUBLANES)], zsem)
                fill.start()
                fill.wait()

    _to_row_tiles(h_ref[...], stage.at[slot])

    def scatter(g, c):
        for j in range(SUBLANES):
            for k in range(TOP_K):
                _tile_copy(stage.at[slot], (g * SUBLANES + j) * SUBLANES,
                           xs_hbm, pos_ref[0, 0, k * tm + g * SUBLANES + j],
                           sems.at[slot]).start(priority=(j * TOP_K + k) % N_DMA_QUEUES)
        return c
    lax.fori_loop(0, tm // SUBLANES, scatter, 0)

    def drain(s):
        for _ in range(TOP_K):
            _tiles_wait(xs_hbm, stage.at[s], sems.at[s])

    @pl.when(i > 0)
    def _():
        drain(1 - slot)

    @pl.when(i == n - 1)
    def _():
        drain(slot)


def _dispatch(ends, counts, pos8, h, *, n_rows, tm, tmb):
    n, d = h.shape
    assert d == SUBLANES * LANES and tm % SUBLANES == 0
    return pl.pallas_call(
        functools.partial(_dispatch_body, tmb=tmb),
        grid_spec=pltpu.PrefetchScalarGridSpec(
            num_scalar_prefetch=2,
            grid=(n // tm,),
            in_specs=[pl.BlockSpec((1, 1, TOP_K * tm), lambda i, *_: (i, 0, 0), memory_space=pltpu.SMEM),
                      pl.BlockSpec((tm, d), lambda i, *_: (i, 0))],
            out_specs=pl.BlockSpec(memory_space=pl.ANY),
            scratch_shapes=[pltpu.VMEM((tmb * SUBLANES, LANES), F32), pltpu.VMEM((2, tm * SUBLANES, LANES), F32),
                            pltpu.SemaphoreType.DMA(()), pltpu.SemaphoreType.DMA((2,))],
        ),
        out_shape=jax.ShapeDtypeStruct((n_rows * SUBLANES, LANES), F32),
        compiler_params=_params("arbitrary"),
        name="dispatch",
    )(ends, counts, pos8, h)


def _expert_body(te_ref, nv_ref, tr_ref, xs_ref, wg_hbm, wu_hbm, wd_hbm, y_ref, hb, acc, wgb, wub, wdb, sems, *, fc):
    i = pl.program_id(0)
    tmb, d = hb.shape
    nf = wg_hbm.shape[2] // fc
    n_valid = nv_ref[0]

    def weight_copies(tile, f, slot):
        e = te_ref[tile]
        cols = pl.ds(pl.multiple_of(f * fc, fc), fc)
        return (pltpu.make_async_copy(wg_hbm.at[e, :, cols], wgb.at[slot], sems.at[0, slot]),
                pltpu.make_async_copy(wu_hbm.at[e, :, cols], wub.at[slot], sems.at[1, slot]),
                pltpu.make_async_copy(wd_hbm.at[e, cols, :], wdb.at[slot], sems.at[2, slot]))

    @pl.when(i >= n_valid)
    def _():
        y_ref[...] = jnp.zeros(y_ref.shape, F32)

    ahead = WEIGHT_BUFFERS - 1

    def run_chunk(f, first, last):
        q = i * nf + f
        wraps = f + ahead >= nf
        ahead_tile = jnp.where(wraps, i + 1, i)

        @pl.when(ahead_tile < n_valid)
        def _():
            for cp in weight_copies(ahead_tile, jnp.where(wraps, f + ahead - nf, f + ahead),
                                    lax.rem(q + ahead, WEIGHT_BUFFERS)):
                cp.start()

        slot = lax.rem(q, WEIGHT_BUFFERS)
        for cp in weight_copies(i, f, slot):
            cp.wait()

        def accumulate(m):
            if first:
                h = jnp.concatenate([col.astype(BF16) for col in _from_row_tiles(xs_ref, m, d // LANES)], axis=1)
                hb[0:m, :] = h
            else:
                h = hb[0:m, :]
            contrib = _swiglu(h, wgb[slot].astype(BF16), wub[slot].astype(BF16), wdb[slot].astype(BF16))
            total = contrib if first else acc[0:m, :] + contrib
            if last:
                _to_row_tiles(total, y_ref)
                if m < tmb:
                    y_ref[m * (d // LANES):, :] = jnp.zeros(((tmb - m) * (d // LANES), LANES), F32)
            else:
                acc[0:m, :] = total

        lo = 0
        for m in sorted(tmb // k for k in EXPERT_ROW_FRACTIONS):
            @pl.when(jnp.logical_and(tr_ref[i] > lo, tr_ref[i] <= m))
            def _():
                accumulate(m)
            lo = m

    @pl.when(i < n_valid)
    def _():
        @pl.when(i == 0)
        def _():
            for f0 in range(ahead):
                for cp in weight_copies(0, f0, f0):
                    cp.start()

        run_chunk(0, True, False)

        def middle(f, carry):
            run_chunk(f, False, False)
            return carry
        lax.fori_loop(1, nf - 1, middle, 0)
        run_chunk(nf - 1, False, True)


def _experts(tile_expert, n_valid, tile_rows, xs, wg, wu, wd, *, tmb, fc):
    n_tiles = tile_expert.shape[0]
    _, d, ff = wg.shape
    c = d // LANES
    assert ff % fc == 0 and ff // fc >= 2 and xs.shape == (n_tiles * tmb * c, LANES)
    return pl.pallas_call(
        functools.partial(_expert_body, fc=fc),
        grid_spec=pltpu.PrefetchScalarGridSpec(
            num_scalar_prefetch=3,
            grid=(n_tiles,),
            in_specs=[pl.BlockSpec((tmb * c, LANES), lambda i, te, nv, tr: (jnp.minimum(i, nv[0] - 1), 0)),
                      pl.BlockSpec(memory_space=pl.ANY), pl.BlockSpec(memory_space=pl.ANY),
                      pl.BlockSpec(memory_space=pl.ANY)],
            out_specs=pl.BlockSpec((tmb * c, LANES), lambda i, te, nv, tr: (i, 0)),
            scratch_shapes=[pltpu.VMEM((tmb, d), BF16), pltpu.VMEM((tmb, d), F32),
                            pltpu.VMEM((WEIGHT_BUFFERS, d, fc), wg.dtype), pltpu.VMEM((WEIGHT_BUFFERS, d, fc), wu.dtype),
                            pltpu.VMEM((WEIGHT_BUFFERS, fc, d), wd.dtype),
                            pltpu.SemaphoreType.DMA((3, WEIGHT_BUFFERS))],
        ),
        out_shape=jax.ShapeDtypeStruct((n_tiles * tmb * c, LANES), F32),
        compiler_params=_params("arbitrary"),
        name="experts",
    )(tile_expert, n_valid, tile_rows, xs, wg, wu, wd)


def _combine_body(pos_ref, pos_next_ref, x_ref, wt_ref, g_ref, y_hbm, o_ref, yrows, sems, *, final_norm):
    i, n = pl.program_id(0), pl.num_programs(0)
    tm, d = x_ref.shape
    slot = lax.rem(i, 2)

    def gather_group(p_ref, s, g):
        for j in range(SUBLANES):
            for k in range(TOP_K):
                _tile_copy(y_hbm, p_ref[0, 0, k * tm + g * SUBLANES + j],
                           yrows.at[s, k], (g * SUBLANES + j) * SUBLANES,
                           sems.at[s]).start(priority=(j * TOP_K + k) % N_DMA_QUEUES)

    def combine():
        ys = [_from_row_tiles(yrows.at[slot, k], tm, d // LANES) for k in range(TOP_K)]
        cols = []
        for j in range(d // LANES):
            moe = wt_ref[:, 0:1] * ys[0][j]
            for k in range(1, TOP_K):
                moe = moe + wt_ref[:, k:k + 1] * ys[k][j]
            cols.append(moe)
        out = x_ref[...] + jnp.concatenate(cols, axis=1)
        o_ref[...] = _rms(out, g_ref[...]) if final_norm else out

    @pl.when(i == 0)
    def _():
        def first(g, c):
            gather_group(pos_ref, 0, g)
            return c
        lax.fori_loop(0, tm // SUBLANES, first, 0)

    for k in range(TOP_K):
        _tiles_wait(y_hbm, yrows.at[slot, k], sems.at[slot])

    @pl.when(i + 1 < n)
    def _():
        for g in range(tm // SUBLANES):
            gather_group(pos_next_ref, 1 - slot, g)
        combine()

    @pl.when(i + 1 == n)
    def _():
        combine()


def _combine(pos8, x, wt, g, y, *, tm, final_norm):
    n, d = x.shape
    assert d == SUBLANES * LANES and tm % SUBLANES == 0
    last = n // tm - 1
    tile = pl.BlockSpec((tm, d), lambda i: (i, 0))
    return pl.pallas_call(
        functools.partial(_combine_body, final_norm=final_norm),
        grid=(n // tm,),
        in_specs=[pl.BlockSpec((1, 1, TOP_K * tm), lambda i: (i, 0, 0), memory_space=pltpu.SMEM),
                  pl.BlockSpec((1, 1, TOP_K * tm), lambda i: (jnp.minimum(i + 1, last), 0, 0),
                               memory_space=pltpu.SMEM),
                  tile, pl.BlockSpec((tm, LANES), lambda i: (i, 0)), _resident(g.shape),
                  pl.BlockSpec(memory_space=pl.ANY)],
        out_specs=tile,
        out_shape=jax.ShapeDtypeStruct((n, d), F32),
        scratch_shapes=[pltpu.VMEM((2, TOP_K, tm * SUBLANES, LANES), F32), pltpu.SemaphoreType.DMA((2,))],
        compiler_params=_params("arbitrary"),
        name="combine",
    )(pos8, pos8, x, wt, g, y)


def _routing_tables(idx, cnt, n, n_exp, tmb, tc):
    tm = idx.shape[1]
    idx = idx.reshape(n // tm, SUBLANES, tm)
    experts = jnp.arange(n_exp, dtype=jnp.int32)
    counts = cnt[0, :n_exp]
    padded = ((counts + (tmb - 1)) // tmb) * tmb
    ends = jnp.sum(jnp.where(experts[None, :] <= experts[:, None], padded[None, :], 0), axis=1)
    starts = ends - padded
    chosen = idx[:, 0:TOP_K, :]
    first = sum(jnp.where(chosen == e, starts[e], 0) for e in range(n_exp))
    pos8 = (first + idx[:, TOP_K:2 * TOP_K, :]) * SUBLANES
    pos8 = pos8.reshape(n // tm, TOP_K, tm // tc, tc).transpose(0, 2, 1, 3).reshape(n // tc, 1, TOP_K * tc)
    n_tiles = (n * TOP_K) // tmb + n_exp
    tile_start = jnp.arange(n_tiles, dtype=jnp.int32) * tmb
    tile_expert = jnp.minimum(jnp.sum((tile_start[:, None] >= ends[None, :]).astype(jnp.int32), axis=1), n_exp - 1)
    tile_rows = jnp.clip(sum(jnp.where(tile_expert == e, starts[e] + counts[e], 0) for e in range(n_exp)) - tile_start,
                         0, tmb)
    return pos8, ends, counts, tile_expert, ends[n_exp - 1:] // tmb, tile_rows, n_tiles


def _moe(x, g, w_router, wg, wu, wd, g_out, *, final_norm):
    h, idx, wt, cnt = _router(x, g, w_router, tm=ROUTER_ROW_TILE)
    pos8, ends, counts, tile_expert, n_valid, tile_rows, n_tiles = _routing_tables(
        idx, cnt, x.shape[0], wg.shape[0], EXPERT_ROW_TILE, ROW_TILE)
    xs = _dispatch(ends, counts, pos8, h, n_rows=n_tiles * EXPERT_ROW_TILE, tm=ROW_TILE, tmb=EXPERT_ROW_TILE)
    y = _experts(tile_expert, n_valid, tile_rows, xs, wg, wu, wd, tmb=EXPERT_ROW_TILE, fc=EXPERT_FF_TILE)
    return _combine(pos8, x, wt, g_out, y, tm=ROW_TILE, final_norm=final_norm)


def _final_norm_body(x_ref, g_ref, o_ref):
    o_ref[...] = _rms(x_ref[...], g_ref[...])


def _final_norm(x, g, *, tm):
    n, d = x.shape
    tile = pl.BlockSpec((tm, d), lambda i: (i, 0))
    return pl.pallas_call(
        _final_norm_body, grid=(n // tm,), in_specs=[tile, _resident(g.shape)], out_specs=tile,
        out_shape=jax.ShapeDtypeStruct((n, d), F32), compiler_params=_params("arbitrary"), name="final_norm",
    )(x, g)


def kernel(x, mem, g_mix, w_in, w_pool_group, pool_scale, w_pool_out, conv_w, w_conv_out, w_mix_out, g_xattn, g_mem, w_xq, w_xk, w_xv, w_xo, g_ffn, w_ff_gate, w_ff_up, w_ff_down, w_router, w_e_gate, w_e_up, w_e_down, g_final):
    b, s, d = x.shape
    depth = g_mix.shape[0]
    bf = lambda w: w.astype(BF16)
    row = lambda v: v.reshape(1, -1)

    kt, v = _memkv(mem, g_mem.reshape(depth, 1, d), bf(w_xk), bf(w_xv))
    xf = x.reshape(b * s, d)
    for l in range(depth):
        xf = _mixer(xf, row(g_mix[l]), bf(w_in[l]), bf(w_pool_group[l]), row(pool_scale[l]), bf(w_pool_out[l]),
                    conv_w[l], bf(w_conv_out[l]), bf(w_mix_out[l]), seq=s, tm=MIXER_ROW_TILE)
        xf = _xattn(xf, row(g_xattn[l]), bf(w_xq[l]), kt, v, bf(w_xo[l]), layer=l, seq=s, tm=MIXER_ROW_TILE)
        last = l == depth - 1
        i = l // 2
        if l % 2 == 0:
            xf = _ffn(xf, row(g_ffn[l]), bf(w_ff_gate[i]), bf(w_ff_up[i]), bf(w_ff_down[i]), tm=ROW_TILE)
            if last:
                xf = _final_norm(xf, row(g_final), tm=ROW_TILE)
        else:
            xf = _moe(xf, row(g_ffn[l]), w_router[i], w_e_gate[i], w_e_up[i], w_e_down[i],
                      row(g_final), final_norm=last)
    return xf.reshape(b, s, d)
```

```python
import functools

import jax
import jax.numpy as jnp
from jax import lax
from jax.experimental import pallas as pl
from jax.experimental.pallas import tpu as pltpu

F32 = jnp.float32
BF16 = jnp.bfloat16

EPS = 1e-6
POOL_WINDOWS = (2, 4, 8, 16)
CONV_K = 3
N_XHEADS = 4
TOP_K = 2

LANES = 128
SUBLANES = 8
VMEM_LIMIT_BYTES = 56 * 1024 * 1024
N_DMA_QUEUES = 2

POOL_HALO = 16
CONV_HALO = 8

ROW_TILE = 512
MIXER_ROW_TILE = 1024
ROW_CHUNKS = 2
MIXER_ROW_CHUNKS = 1
ROUTER_ROW_TILE = 1024
EXPERT_ROW_TILE = 1024
EXPERT_FF_TILE = 512
WEIGHT_BUFFERS = 4
EXPERT_ROW_FRACTIONS = (1, 2, 4)


def _dot(a, b):
    return jnp.dot(a, b, preferred_element_type=F32)


def _sigmoid(z):
    return 0.5 * jnp.tanh(0.5 * z) + 0.5


def _rms(xf, g):
    ms = jnp.mean(xf * xf, axis=-1, keepdims=True)
    return (xf * lax.rsqrt(ms + EPS)) * g


def _resident(shape):
    nd = len(shape)
    return pl.BlockSpec(shape, lambda *_: (0,) * nd, pipeline_mode=pl.Buffered(1))


def _params(*sem):
    return pltpu.CompilerParams(dimension_semantics=sem, vmem_limit_bytes=VMEM_LIMIT_BYTES)


def _mixer_body(x_ref, g_ref, win_ref, wgrp_ref, ps_ref, wpo_ref, cw_ref, wco_ref, wmo_ref, o_ref,
                pool_buf, conv_buf, *, tm, seq):
    n_groups, _, gw = wgrp_ref.shape
    pw = n_groups * gw
    cwid = cw_ref.shape[1]
    d = x_ref.shape[1]
    o_c, o_b, o_u, o_g = pw, pw + cwid, pw + 2 * cwid, pw + 3 * cwid

    pos0 = lax.rem(pl.program_id(0) * tm, seq)

    @pl.when(pos0 == 0)
    def _():
        pool_buf[:, 0:POOL_HALO, :] = jnp.zeros((n_groups, POOL_HALO, gw), F32)
        conv_buf[:, 0:CONV_HALO, :] = jnp.zeros((conv_buf.shape[0], CONV_HALO, LANES), F32)

    @pl.when(pos0 != 0)
    def _():
        pool_buf[:, 0:POOL_HALO, :] = pool_buf[:, tm:tm + POOL_HALO, :]
        conv_buf[:, 0:CONV_HALO, :] = conv_buf[:, tm:tm + CONV_HALO, :]

    cm = tm // MIXER_ROW_CHUNKS
    for r0 in range(0, tm, cm):
        x = x_ref[r0:r0 + cm, :]
        h = _rms(x, g_ref[...]).astype(BF16)

        u_pool = _dot(h, win_ref[:, 0:o_c])
        c_gate = _dot(h, win_ref[:, o_c:o_b])
        u_conv = _dot(h, win_ref[:, o_u:o_g])
        b_gate = _dot(h, win_ref[:, o_b:o_u])
        gate_pool = _dot(h, win_ref[:, o_g:o_g + d])
        gate_conv = _dot(h, win_ref[:, o_g + d:o_g + 2 * d])
        ys = []
        for g, w in enumerate(POOL_WINDOWS):
            ug = u_pool[:, g * gw:(g + 1) * gw]
            lo = POOL_HALO + r0
            pool_buf[g, lo:lo + cm, :] = ug
            acc = ug
            for k in range(1, w):
                acc = acc + pool_buf[g, lo - k:lo - k + cm, :]
            if r0 == 0:
                pos1 = lax.broadcasted_iota(jnp.int32, (POOL_HALO, gw), 0) + (pos0 + 1)
                head = acc[0:POOL_HALO] / jnp.minimum(pos1, w).astype(F32)
                mean = jnp.concatenate([head, acc[POOL_HALO:] * (1.0 / w)], axis=0)
            else:
                mean = acc * (1.0 / w)
            ys.append(_dot((mean - ug).astype(BF16), wgrp_ref[g]))
        y = jnp.concatenate(ys, axis=1) * ps_ref[...]
        y_pool = _dot(y.astype(BF16), wpo_ref[...])

        zc = c_gate * u_conv
        ycs = []
        for j in range(cwid // LANES):
            sl = slice(j * LANES, (j + 1) * LANES)
            zj = zc[:, sl]
            lo = CONV_HALO + r0
            conv_buf[j, lo:lo + cm, :] = zj
            acc = cw_ref[0:1, sl] * conv_buf[j, lo - (CONV_K - 1):lo - (CONV_K - 1) + cm, :]
            for k in range(1, CONV_K - 1):
                off = lo - (CONV_K - 1) + k
                acc = acc + cw_ref[k:k + 1, sl] * conv_buf[j, off:off + cm, :]
            ycs.append(acc + cw_ref[CONV_K - 1:CONV_K, sl] * zj)
        y_conv = _dot((b_gate * jnp.concatenate(ycs, axis=1)).astype(BF16), wco_ref[...])

        merged = _sigmoid(gate_pool) * y_pool + _sigmoid(gate_conv) * y_conv
        o_ref[r0:r0 + cm, :] = x + _dot(merged.astype(BF16), wmo_ref[...])


def _mixer(x, g, win, wgrp, ps, wpo, cw, wco, wmo, *, seq, tm):
    n, d = x.shape
    n_groups, _, gw = wgrp.shape
    assert seq % tm == 0 and n % tm == 0 and gw == LANES and cw.shape[1] % LANES == 0
    assert (tm // MIXER_ROW_CHUNKS) % SUBLANES == 0 and tm // MIXER_ROW_CHUNKS >= POOL_HALO
    assert len(POOL_WINDOWS) == n_groups and max(POOL_WINDOWS) <= POOL_HALO and cw.shape[0] == CONV_K
    assert all(w & (w - 1) == 0 for w in POOL_WINDOWS)
    tile = pl.BlockSpec((tm, d), lambda i: (i, 0))
    return pl.pallas_call(
        functools.partial(_mixer_body, tm=tm, seq=seq),
        grid=(n // tm,),
        in_specs=[tile, _resident(g.shape), _resident(win.shape), _resident(wgrp.shape), _resident(ps.shape),
                  _resident(wpo.shape), _resident(cw.shape), _resident(wco.shape), _resident(wmo.shape)],
        out_specs=tile,
        out_shape=jax.ShapeDtypeStruct((n, d), F32),
        scratch_shapes=[pltpu.VMEM((n_groups, tm + POOL_HALO, gw), F32),
                        pltpu.VMEM((cw.shape[1] // LANES, tm + CONV_HALO, LANES), F32)],
        compiler_params=_params("arbitrary"),
        name="mixer",
    )(x, g, win, wgrp, ps, wpo, cw, wco, wmo)


def _memkv_body(mem_ref, g_ref, wk_ref, wv_ref, kt_ref, v_ref):
    mn = _rms(mem_ref[...], g_ref[...]).astype(BF16)
    kt_ref[...] = _dot(mn, wk_ref[...]).T.astype(BF16)
    v_ref[...] = _dot(mn, wv_ref[...]).astype(BF16)


def _memkv(mem, g_mem, wk, wv):
    b, m, d = mem.shape
    depth = wk.shape[0]
    per_layer = lambda l, i: (l, 0, 0)
    return pl.pallas_call(
        _memkv_body,
        grid=(depth, b),
        in_specs=[pl.BlockSpec((None, m, d), lambda l, i: (i, 0, 0)),
                  pl.BlockSpec((None, 1, d), per_layer),
                  pl.BlockSpec((None, d, d), per_layer),
                  pl.BlockSpec((None, d, d), per_layer)],
        out_specs=[pl.BlockSpec((None, None, d, m), lambda l, i: (l, i, 0, 0)),
                   pl.BlockSpec((None, None, m, d), lambda l, i: (l, i, 0, 0))],
        out_shape=[jax.ShapeDtypeStruct((depth, b, d, m), BF16), jax.ShapeDtypeStruct((depth, b, m, d), BF16)],
        compiler_params=_params("arbitrary", "arbitrary"),
        name="memkv",
    )(mem, g_mem, wk, wv)


def _xattn_body(x_ref, g_ref, wq_ref, kt_ref, v_ref, wo_ref, o_ref):
    tm = x_ref.shape[0]
    for r0 in range(0, tm, tm // ROW_CHUNKS):
        rows = slice(r0, r0 + tm // ROW_CHUNKS)
        x = x_ref[rows, :]
        h = _rms(x, g_ref[...]).astype(BF16)
        q = _dot(h, wq_ref[...])
        dh = q.shape[1] // N_XHEADS
        sls = [slice(hd * dh, (hd + 1) * dh) for hd in range(N_XHEADS)]
        scores = [_dot(q[:, sl].astype(BF16), kt_ref[sl, :]) * (dh ** -0.5) for sl in sls]
        probs = []
        for s in scores:
            e = jnp.exp(s - jnp.max(s, axis=-1, keepdims=True))
            probs.append((e / jnp.sum(e, axis=-1, keepdims=True)).astype(BF16))
        heads = [_dot(p, v_ref[:, sl]) for p, sl in zip(probs, sls)]
        o = jnp.concatenate(heads, axis=1).astype(BF16)
        o_ref[rows, :] = x + _dot(o, wo_ref[...])


def _xattn(x, g, wq, kt, v, wo, *, layer, seq, tm):
    n, d = x.shape
    m = v.shape[2]
    assert seq % tm == 0 and d % N_XHEADS == 0
    tiles_per_seq = seq // tm
    tile = pl.BlockSpec((tm, d), lambda i: (i, 0))
    return pl.pallas_call(
        _xattn_body,
        grid=(n // tm,),
        in_specs=[tile, _resident(g.shape), _resident(wq.shape),
                  pl.BlockSpec((None, None, d, m), lambda i: (layer, i // tiles_per_seq, 0, 0)),
                  pl.BlockSpec((None, None, m, d), lambda i: (layer, i // tiles_per_seq, 0, 0)),
                  _resident(wo.shape)],
        out_specs=tile,
        out_shape=jax.ShapeDtypeStruct((n, d), F32),
        compiler_params=_params("arbitrary"),
        name="xattn",
    )(x, g, wq, kt, v, wo)


def _swiglu(h, wg, wu, wd):
    g = _dot(h, wg)
    return _dot((g * _sigmoid(g) * _dot(h, wu)).astype(BF16), wd)


def _ffn_body(x_ref, g_ref, wg_ref, wu_ref, wd_ref, o_ref):
    tm = x_ref.shape[0]
    for r0 in range(0, tm, tm // ROW_CHUNKS):
        rows = slice(r0, r0 + tm // ROW_CHUNKS)
        x = x_ref[rows, :]
        h = _rms(x, g_ref[...]).astype(BF16)
        o_ref[rows, :] = x + _swiglu(h, wg_ref[...], wu_ref[...], wd_ref[...])


def _ffn(x, g, wg, wu, wd, *, tm):
    n, d = x.shape
    tile = pl.BlockSpec((tm, d), lambda i: (i, 0))
    return pl.pallas_call(
        _ffn_body,
        grid=(n // tm,),
        in_specs=[tile, _resident(g.shape), _resident(wg.shape), _resident(wu.shape), _resident(wd.shape)],
        out_specs=tile,
        out_shape=jax.ShapeDtypeStruct((n, d), F32),
        compiler_params=_params("arbitrary"),
        name="ffn",
    )(x, g, wg, wu, wd)


def _split_bf16(v):
    hi = v.astype(BF16)
    return hi, (v - hi.astype(F32)).astype(BF16)


def _router_body(x_ref, g_ref, w2_ref, h_ref, idx_ref, wt_ref, cnt_ref, base_ref, *, n_exp):
    tm = x_ref.shape[0]

    @pl.when(pl.program_id(0) == 0)
    def _():
        base_ref[...] = jnp.zeros(base_ref.shape, F32)

    h = _rms(x_ref[...], g_ref[...])
    h_ref[...] = h
    h_hi, h_lo = _split_bf16(h)
    part = _dot(h_hi, w2_ref[...]) + _dot(h_lo, w2_ref[...])
    lane = lax.broadcasted_iota(jnp.int32, (tm, LANES), 1)
    logits = jnp.where(lane < n_exp, part + pltpu.roll(part, LANES - n_exp, axis=1), -jnp.inf)
    lane_f = lane.astype(F32)
    m1 = jnp.max(logits, axis=-1, keepdims=True)
    i1 = jnp.min(jnp.where(logits == m1, lane_f, float(LANES)), axis=-1, keepdims=True).astype(jnp.int32)
    rest = jnp.where(lane == i1, -jnp.inf, logits)
    m2 = jnp.max(rest, axis=-1, keepdims=True)
    i2 = jnp.min(jnp.where(rest == m2, lane_f, float(LANES)), axis=-1, keepdims=True).astype(jnp.int32)
    e2 = jnp.exp(m2 - m1)
    den = 1.0 + e2
    wt_ref[...] = jnp.where(lane == 0, 1.0 / den, jnp.where(lane == 1, e2 / den, 0.0))

    oh1, oh2 = lane == i1, lane == i2
    hits = jnp.logical_or(oh1, oh2)
    earlier = lax.broadcasted_iota(jnp.int32, (tm, tm), 1) < lax.broadcasted_iota(jnp.int32, (tm, tm), 0)
    prefix = _dot(earlier.astype(BF16), hits.astype(BF16)) + base_ref[...]
    r1 = jnp.sum(jnp.where(oh1, prefix, 0.0), axis=-1, keepdims=True).astype(jnp.int32)
    r2 = jnp.sum(jnp.where(oh2, prefix, 0.0), axis=-1, keepdims=True).astype(jnp.int32)
    idx = jnp.where(lane == 0, i1, jnp.where(lane == 1, i2, jnp.where(lane == 2, r1, jnp.where(lane == 3, r2, 0))))
    idx_ref[...] = idx.T[0:SUBLANES, :]
    base_ref[...] += jnp.sum(hits.astype(F32), axis=0, keepdims=True)
    cnt_ref[...] = jnp.broadcast_to(base_ref[...], cnt_ref.shape).astype(jnp.int32)


def _router(x, g, w_router, *, tm):
    n, d = x.shape
    n_exp = w_router.shape[1]
    assert 2 * n_exp <= LANES
    w2 = jnp.pad(jnp.concatenate(_split_bf16(w_router), axis=1), ((0, 0), (0, LANES - 2 * n_exp)))
    tile = lambda w: pl.BlockSpec((tm, w), lambda i: (i, 0))
    return pl.pallas_call(
        functools.partial(_router_body, n_exp=n_exp),
        grid=(n // tm,),
        in_specs=[tile(d), _resident(g.shape), _resident(w2.shape)],
        out_specs=[tile(d), pl.BlockSpec((SUBLANES, tm), lambda i: (i, 0)), tile(LANES),
                   pl.BlockSpec((SUBLANES, LANES), lambda i: (0, 0))],
        out_shape=[jax.ShapeDtypeStruct((n, d), F32), jax.ShapeDtypeStruct((n // tm * SUBLANES, tm), jnp.int32),
                   jax.ShapeDtypeStruct((n, LANES), F32), jax.ShapeDtypeStruct((SUBLANES, LANES), jnp.int32)],
        scratch_shapes=[pltpu.VMEM((1, LANES), F32)],
        compiler_params=_params("arbitrary"),
        name="router",
    )(x, g, w2)


def _to_row_tiles(v, dst_ref):
    c = v.shape[1] // LANES
    for j in range(c):
        dst_ref[pl.ds(j, v.shape[0], stride=c), :] = v[:, j * LANES:(j + 1) * LANES]


def _from_row_tiles(src_ref, rows, c):
    return [src_ref[pl.ds(j, rows, stride=c), :] for j in range(c)]


def _tile_copy(src, src_row8, dst, dst_row8, sem):
    return pltpu.make_async_copy(src.at[pl.ds(pl.multiple_of(src_row8, SUBLANES), SUBLANES)],
                                 dst.at[pl.ds(pl.multiple_of(dst_row8, SUBLANES), SUBLANES)], sem)


def _tiles_wait(hbm, vmem, sem):
    pltpu.make_async_copy(hbm.at[pl.ds(0, vmem.shape[0])], vmem, sem).wait()


def _dispatch_body(ends_ref, cnt_ref, pos_ref, h_ref, xs_hbm, zbuf, stage, zsem, sems, *, tmb):
    i, n = pl.program_id(0), pl.num_programs(0)
    tm = h_ref.shape[0]
    slot = lax.rem(i, 2)

    @pl.when(i == 0)
    def _():
        n_exp = ends_ref.shape[0]
        zbuf[...] = jnp.zeros(zbuf.shape, zbuf.dtype)
        for e in range(2 * n_exp):
            first = ends_ref[e] - tmb if e < n_exp else ends_ref[n_exp - 1] + (e - n_exp) * tmb
            live = cnt_ref[e] > 0 if e < n_exp else first * SUBLANES < xs_hbm.shape[0]

            @pl.when(live)
            def _():
                start = pl.multiple_of(first * SUBLANES, tmb * SUBLANES)
                fill = pltpu.make_async_copy(zbuf, xs_hbm.at[pl.ds(start, tmb * SUBLANES)], zsem)
                fill.start()
                fill.wait()

    _to_row_tiles(h_ref[...], stage.at[slot])

    def scatter(g, c):
        for j in range(SUBLANES):
            for k in range(TOP_K):
                _tile_copy(stage.at[slot], (g * SUBLANES + j) * SUBLANES,
                           xs_hbm, pos_ref[0, 0, k * tm + g * SUBLANES + j],
                           sems.at[slot]).start(priority=(j * TOP_K + k) % N_DMA_QUEUES)
        return c
    lax.fori_loop(0, tm // SUBLANES, scatter, 0)

    def drain(s):
        for _ in range(TOP_K):
            _tiles_wait(xs_hbm, stage.at[s], sems.at[s])

    @pl.when(i > 0)
    def _():
        drain(1 - slot)

    @pl.when(i == n - 1)
    def _():
        drain(slot)


def _dispatch(ends, counts, pos8, h, *, n_rows, tm, tmb):
    n, d = h.shape
    assert d == SUBLANES * LANES and tm % SUBLANES == 0
    return pl.pallas_call(
        functools.partial(_dispatch_body, tmb=tmb),
        grid_spec=pltpu.PrefetchScalarGridSpec(
            num_scalar_prefetch=2,
            grid=(n // tm,),
            in_specs=[pl.BlockSpec((1, 1, TOP_K * tm), lambda i, *_: (i, 0, 0), memory_space=pltpu.SMEM),
                      pl.BlockSpec((tm, d), lambda i, *_: (i, 0))],
            out_specs=pl.BlockSpec(memory_space=pl.ANY),
            scratch_shapes=[pltpu.VMEM((tmb * SUBLANES, LANES), F32), pltpu.VMEM((2, tm * SUBLANES, LANES), F32),
                            pltpu.SemaphoreType.DMA(()), pltpu.SemaphoreType.DMA((2,))],
        ),
        out_shape=jax.ShapeDtypeStruct((n_rows * SUBLANES, LANES), F32),
        compiler_params=_params("arbitrary"),
        name="dispatch",
    )(ends, counts, pos8, h)


def _expert_body(te_ref, nv_ref, tr_ref, xs_ref, wg_hbm, wu_hbm, wd_hbm, y_ref, hb, acc, wgb, wub, wdb, sems, *, fc):
    i = pl.program_id(0)
    tmb, d = hb.shape
    nf = wg_hbm.shape[2] // fc
    n_valid = nv_ref[0]

    def weight_copies(tile, f, slot):
        e = te_ref[tile]
        cols = pl.ds(pl.multiple_of(f * fc, fc), fc)
        return (pltpu.make_async_copy(wg_hbm.at[e, :, cols], wgb.at[slot], sems.at[0, slot]),
                pltpu.make_async_copy(wu_hbm.at[e, :, cols], wub.at[slot], sems.at[1, slot]),
                pltpu.make_async_copy(wd_hbm.at[e, cols, :], wdb.at[slot], sems.at[2, slot]))

    @pl.when(i >= n_valid)
    def _():
        y_ref[...] = jnp.zeros(y_ref.shape, F32)

    ahead = WEIGHT_BUFFERS - 1

    def run_chunk(f, first, last):
        q = i * nf + f
        wraps = f + ahead >= nf
        ahead_tile = jnp.where(wraps, i + 1, i)

        @pl.when(ahead_tile < n_valid)
        def _():
            for cp in weight_copies(ahead_tile, jnp.where(wraps, f + ahead - nf, f + ahead),
                                    lax.rem(q + ahead, WEIGHT_BUFFERS)):
                cp.start()

        slot = lax.rem(q, WEIGHT_BUFFERS)
        for cp in weight_copies(i, f, slot):
            cp.wait()

        def accumulate(m):
            if first:
                h = jnp.concatenate([col.astype(BF16) for col in _from_row_tiles(xs_ref, m, d // LANES)], axis=1)
                hb[0:m, :] = h
            else:
                h = hb[0:m, :]
            contrib = _swiglu(h, wgb[slot].astype(BF16), wub[slot].astype(BF16), wdb[slot].astype(BF16))
            total = contrib if first else acc[0:m, :] + contrib
            if last:
                _to_row_tiles(total, y_ref)
                if m < tmb:
                    y_ref[m * (d // LANES):, :] = jnp.zeros(((tmb - m) * (d // LANES), LANES), F32)
            else:
                acc[0:m, :] = total

        lo = 0
        for m in sorted(tmb // k for k in EXPERT_ROW_FRACTIONS):
            @pl.when(jnp.logical_and(tr_ref[i] > lo, tr_ref[i] <= m))
            def _():
                accumulate(m)
            lo = m

    @pl.when(i < n_valid)
    def _():
        @pl.when(i == 0)
        def _():
            for f0 in range(ahead):
                for cp in weight_copies(0, f0, f0):
                    cp.start()

        run_chunk(0, True, False)

        def middle(f, carry):
            run_chunk(f, False, False)
            return carry
        lax.fori_loop(1, nf - 1, middle, 0)
        run_chunk(nf - 1, False, True)


def _experts(tile_expert, n_valid, tile_rows, xs, wg, wu, wd, *, tmb, fc):
    n_tiles = tile_expert.shape[0]
    _, d, ff = wg.shape
    c = d // LANES
    assert ff % fc == 0 and ff // fc >= 2 and xs.shape == (n_tiles * tmb * c, LANES)
    return pl.pallas_call(
        functools.partial(_expert_body, fc=fc),
        grid_spec=pltpu.PrefetchScalarGridSpec(
            num_scalar_prefetch=3,
            grid=(n_tiles,),
            in_specs=[pl.BlockSpec((tmb * c, LANES), lambda i, te, nv, tr: (jnp.minimum(i, nv[0] - 1), 0)),
                      pl.BlockSpec(memory_space=pl.ANY), pl.BlockSpec(memory_space=pl.ANY),
                      pl.BlockSpec(memory_space=pl.ANY)],
            out_specs=pl.BlockSpec((tmb * c, LANES), lambda i, te, nv, tr: (i, 0)),
            scratch_shapes=[pltpu.VMEM((tmb, d), BF16), pltpu.VMEM((tmb, d), F32),
                            pltpu.VMEM((WEIGHT_BUFFERS, d, fc), wg.dtype), pltpu.VMEM((WEIGHT_BUFFERS, d, fc), wu.dtype),
                            pltpu.VMEM((WEIGHT_BUFFERS, fc, d), wd.dtype),
                            pltpu.SemaphoreType.DMA((3, WEIGHT_BUFFERS))],
        ),
        out_shape=jax.ShapeDtypeStruct((n_tiles * tmb * c, LANES), F32),
        compiler_params=_params("arbitrary"),
        name="experts",
    )(tile_expert, n_valid, tile_rows, xs, wg, wu, wd)


def _combine_body(pos_ref, pos_next_ref, x_ref, wt_ref, g_ref, y_hbm, o_ref, yrows, sems, *, final_norm):
    i, n = pl.program_id(0), pl.num_programs(0)
    tm, d = x_ref.shape
    slot = lax.rem(i, 2)

    def gather_group(p_ref, s, g):
        for j in range(SUBLANES):
            for k in range(TOP_K):
                _tile_copy(y_hbm, p_ref[0, 0, k * tm + g * SUBLANES + j],
                           yrows.at[s, k], (g * SUBLANES + j) * SUBLANES,
                           sems.at[s]).start(priority=(j * TOP_K + k) % N_DMA_QUEUES)

    def combine():
        ys = [_from_row_tiles(yrows.at[slot, k], tm, d // LANES) for k in range(TOP_K)]
        cols = []
        for j in range(d // LANES):
            moe = wt_ref[:, 0:1] * ys[0][j]
            for k in range(1, TOP_K):
                moe = moe + wt_ref[:, k:k + 1] * ys[k][j]
            cols.append(moe)
        out = x_ref[...] + jnp.concatenate(cols, axis=1)
        o_ref[...] = _rms(out, g_ref[...]) if final_norm else out

    @pl.when(i == 0)
    def _():
        def first(g, c):
            gather_group(pos_ref, 0, g)
            return c
        lax.fori_loop(0, tm // SUBLANES, first, 0)

    for k in range(TOP_K):
        _tiles_wait(y_hbm, yrows.at[slot, k], sems.at[slot])

    @pl.when(i + 1 < n)
    def _():
        for g in range(tm // SUBLANES):
            gather_group(pos_next_ref, 1 - slot, g)
        combine()

    @pl.when(i + 1 == n)
    def _():
        combine()


def _combine(pos8, x, wt, g, y, *, tm, final_norm):
    n, d = x.shape
    assert d == SUBLANES * LANES and tm % SUBLANES == 0
    last = n // tm - 1
    tile = pl.BlockSpec((tm, d), lambda i: (i, 0))
    return pl.pallas_call(
        functools.partial(_combine_body, final_norm=final_norm),
        grid=(n // tm,),
        in_specs=[pl.BlockSpec((1, 1, TOP_K * tm), lambda i: (i, 0, 0), memory_space=pltpu.SMEM),
                  pl.BlockSpec((1, 1, TOP_K * tm), lambda i: (jnp.minimum(i + 1, last), 0, 0),
                               memory_space=pltpu.SMEM),
                  tile, pl.BlockSpec((tm, LANES), lambda i: (i, 0)), _resident(g.shape),
                  pl.BlockSpec(memory_space=pl.ANY)],
        out_specs=tile,
        out_shape=jax.ShapeDtypeStruct((n, d), F32),
        scratch_shapes=[pltpu.VMEM((2, TOP_K, tm * SUBLANES, LANES), F32), pltpu.SemaphoreType.DMA((2,))],
        compiler_params=_params("arbitrary"),
        name="combine",
    )(pos8, pos8, x, wt, g, y)


def _routing_tables(idx, cnt, n, n_exp, tmb, tc):
    tm = idx.shape[1]
    idx = idx.reshape(n // tm, SUBLANES, tm)
    experts = jnp.arange(n_exp, dtype=jnp.int32)
    counts = cnt[0, :n_exp]
    padded = ((counts + (tmb - 1)) // tmb) * tmb
    ends = jnp.sum(jnp.where(experts[None, :] <= experts[:, None], padded[None, :], 0), axis=1)
    starts = ends - padded
    chosen = idx[:, 0:TOP_K, :]
    first = sum(jnp.where(chosen == e, starts[e], 0) for e in range(n_exp))
    pos8 = (first + idx[:, TOP_K:2 * TOP_K, :]) * SUBLANES
    pos8 = pos8.reshape(n // tm, TOP_K, tm // tc, tc).transpose(0, 2, 1, 3).reshape(n // tc, 1, TOP_K * tc)
    n_tiles = (n * TOP_K) // tmb + n_exp
    tile_start = jnp.arange(n_tiles, dtype=jnp.int32) * tmb
    tile_expert = jnp.minimum(jnp.sum((tile_start[:, None] >= ends[None, :]).astype(jnp.int32), axis=1), n_exp - 1)
    tile_rows = jnp.clip(sum(jnp.where(tile_expert == e, starts[e] + counts[e], 0) for e in range(n_exp)) - tile_start,
                         0, tmb)
    return pos8, ends, counts, tile_expert, ends[n_exp - 1:] // tmb, tile_rows, n_tiles


def _moe(x, g, w_router, wg, wu, wd, g_out, *, final_norm):
    h, idx, wt, cnt = _router(x, g, w_router, tm=ROUTER_ROW_TILE)
    pos8, ends, counts, tile_expert, n_valid, tile_rows, n_tiles = _routing_tables(
        idx, cnt, x.shape[0], wg.shape[0], EXPERT_ROW_TILE, ROW_TILE)
    xs = _dispatch(ends, counts, pos8, h, n_rows=n_tiles * EXPERT_ROW_TILE, tm=ROW_TILE, tmb=EXPERT_ROW_TILE)
    y = _experts(tile_expert, n_valid, tile_rows, xs, wg, wu, wd, tmb=EXPERT_ROW_TILE, fc=EXPERT_FF_TILE)
    return _combine(pos8, x, wt, g_out, y, tm=ROW_TILE, final_norm=final_norm)


def _final_norm_body(x_ref, g_ref, o_ref):
    o_ref[...] = _rms(x_ref[...], g_ref[...])


def _final_norm(x, g, *, tm):
    n, d = x.shape
    tile = pl.BlockSpec((tm, d), lambda i: (i, 0))
    return pl.pallas_call(
        _final_norm_body, grid=(n // tm,), in_specs=[tile, _resident(g.shape)], out_specs=tile,
        out_shape=jax.ShapeDtypeStruct((n, d), F32), compiler_params=_params("arbitrary"), name="final_norm",
    )(x, g)


def kernel(x, mem, g_mix, w_in, w_pool_group, pool_scale, w_pool_out, conv_w, w_conv_out, w_mix_out, g_xattn, g_mem, w_xq, w_xk, w_xv, w_xo, g_ffn, w_ff_gate, w_ff_up, w_ff_down, w_router, w_e_gate, w_e_up, w_e_down, g_final):
    b, s, d = x.shape
    depth = g_mix.shape[0]
    bf = lambda w: w.astype(BF16)
    row = lambda v: v.reshape(1, -1)

    kt, v = _memkv(mem, g_mem.reshape(depth, 1, d), bf(w_xk), bf(w_xv))
    xf = x.reshape(b * s, d)
    for l in range(depth):
        xf = _mixer(xf, row(g_mix[l]), bf(w_in[l]), bf(w_pool_group[l]), row(pool_scale[l]), bf(w_pool_out[l]),
                    conv_w[l], bf(w_conv_out[l]), bf(w_mix_out[l]), seq=s, tm=MIXER_ROW_TILE)
        xf = _xattn(xf, row(g_xattn[l]), bf(w_xq[l]), kt, v, bf(w_xo[l]), layer=l, seq=s, tm=MIXER_ROW_TILE)
        last = l == depth - 1
        i = l // 2
        if l % 2 == 0:
            xf = _ffn(xf, row(g_ffn[l]), bf(w_ff_gate[i]), bf(w_ff_up[i]), bf(w_ff_down[i]), tm=ROW_TILE)
            if last:
                xf = _final_norm(xf, row(g_final), tm=ROW_TILE)
        else:
            xf = _moe(xf, row(g_ffn[l]), w_router[i], w_e_gate[i], w_e_up[i], w_e_down[i],
                      row(g_final), final_norm=last)
    return xf.reshape(b, s, d)
```

```python
import functools

import jax
import jax.numpy as jnp
from jax import lax
from jax.experimental import pallas as pl
from jax.experimental.pallas import tpu as pltpu

F32 = jnp.float32
BF16 = jnp.bfloat16

EPS = 1e-6
POOL_WINDOWS = (2, 4, 8, 16)
CONV_K = 3
N_XHEADS = 4
TOP_K = 2

LANES = 128
SUBLANES = 8
VMEM_LIMIT_BYTES = 56 * 1024 * 1024
N_DMA_QUEUES = 2

POOL_HALO = 16
CONV_HALO = 8

ROW_TILE = 512
MIXER_ROW_TILE = 1024
ROW_CHUNKS = 2
MIXER_ROW_CHUNKS = 1
ROUTER_ROW_TILE = 1024
EXPERT_ROW_TILE = 1024
EXPERT_FF_TILE = 512
WEIGHT_BUFFERS = 4
EXPERT_ROW_FRACTIONS = (1, 2, 4)


def _dot(a, b):
    return jnp.dot(a, b, preferred_element_type=F32)


def _sigmoid(z):
    return 0.5 * jnp.tanh(0.5 * z) + 0.5


def _rms(xf, g):
    ms = jnp.mean(xf * xf, axis=-1, keepdims=True)
    return (xf * lax.rsqrt(ms + EPS)) * g


def _resident(shape):
    nd = len(shape)
    return pl.BlockSpec(shape, lambda *_: (0,) * nd, pipeline_mode=pl.Buffered(1))


def _params(*sem):
    return pltpu.CompilerParams(dimension_semantics=sem, vmem_limit_bytes=VMEM_LIMIT_BYTES)


def _mixer_body(x_ref, g_ref, win_ref, wgrp_ref, ps_ref, wpo_ref, cw_ref, wco_ref, wmo_ref, o_ref,
                pool_buf, conv_buf, *, tm, seq):
    n_groups, _, gw = wgrp_ref.shape
    pw = n_groups * gw
    cwid = cw_ref.shape[1]
    d = x_ref.shape[1]
    o_c, o_b, o_u, o_g = pw, pw + cwid, pw + 2 * cwid, pw + 3 * cwid

    pos0 = lax.rem(pl.program_id(0) * tm, seq)

    @pl.when(pos0 == 0)
    def _():
        pool_buf[:, 0:POOL_HALO, :] = jnp.zeros((n_groups, POOL_HALO, gw), F32)
        conv_buf[:, 0:CONV_HALO, :] = jnp.zeros((conv_buf.shape[0], CONV_HALO, LANES), F32)

    @pl.when(pos0 != 0)
    def _():
        pool_buf[:, 0:POOL_HALO, :] = pool_buf[:, tm:tm + POOL_HALO, :]
        conv_buf[:, 0:CONV_HALO, :] = conv_buf[:, tm:tm + CONV_HALO, :]

    cm = tm // MIXER_ROW_CHUNKS
    for r0 in range(0, tm, cm):
        x = x_ref[r0:r0 + cm, :]
        h = _rms(x, g_ref[...]).astype(BF16)

        u_pool = _dot(h, win_ref[:, 0:o_c])
        c_gate = _dot(h, win_ref[:, o_c:o_b])
        u_conv = _dot(h, win_ref[:, o_u:o_g])
        b_gate = _dot(h, win_ref[:, o_b:o_u])
        gate_pool = _dot(h, win_ref[:, o_g:o_g + d])
        gate_conv = _dot(h, win_ref[:, o_g + d:o_g + 2 * d])
        ys = []
        for g, w in enumerate(POOL_WINDOWS):
            ug = u_pool[:, g * gw:(g + 1) * gw]
            lo = POOL_HALO + r0
            pool_buf[g, lo:lo + cm, :] = ug
            acc = ug
            for k in range(1, w):
                acc = acc + pool_buf[g, lo - k:lo - k + cm, :]
            if r0 == 0:
                pos1 = lax.broadcasted_iota(jnp.int32, (POOL_HALO, gw), 0) + (pos0 + 1)
                head = acc[0:POOL_HALO] / jnp.minimum(pos1, w).astype(F32)
                mean = jnp.concatenate([head, acc[POOL_HALO:] * (1.0 / w)], axis=0)
            else:
                mean = acc * (1.0 / w)
            ys.append(_dot((mean - ug).astype(BF16), wgrp_ref[g]))
        y = jnp.concatenate(ys, axis=1) * ps_ref[...]
        y_pool = _dot(y.astype(BF16), wpo_ref[...])

        zc = c_gate * u_conv
        ycs = []
        for j in range(cwid // LANES):
            sl = slice(j * LANES, (j + 1) * LANES)
            zj = zc[:, sl]
            lo = CONV_HALO + r0
            conv_buf[j, lo:lo + cm, :] = zj
            acc = cw_ref[0:1, sl] * conv_buf[j, lo - (CONV_K - 1):lo - (CONV_K - 1) + cm, :]
            for k in range(1, CONV_K - 1):
                off = lo - (CONV_K - 1) + k
                acc = acc + cw_ref[k:k + 1, sl] * conv_buf[j, off:off + cm, :]
            ycs.append(acc + cw_ref[CONV_K - 1:CONV_K, sl] * zj)
        y_conv = _dot((b_gate * jnp.concatenate(ycs, axis=1)).astype(BF16), wco_ref[...])

        merged = _sigmoid(gate_pool) * y_pool + _sigmoid(gate_conv) * y_conv
        o_ref[r0:r0 + cm, :] = x + _dot(merged.astype(BF16), wmo_ref[...])


def _mixer(x, g, win, wgrp, ps, wpo, cw, wco, wmo, *, seq, tm):
    n, d = x.shape
    n_groups, _, gw = wgrp.shape
    assert seq % tm == 0 and n % tm == 0 and gw == LANES and cw.shape[1] % LANES == 0
    assert (tm // MIXER_ROW_CHUNKS) % SUBLANES == 0 and tm // MIXER_ROW_CHUNKS >= POOL_HALO
    assert len(POOL_WINDOWS) == n_groups and max(POOL_WINDOWS) <= POOL_HALO and cw.shape[0] == CONV_K
    assert all(w & (w - 1) == 0 for w in POOL_WINDOWS)
    tile = pl.BlockSpec((tm, d), lambda i: (i, 0))
    return pl.pallas_call(
        functools.partial(_mixer_body, tm=tm, seq=seq),
        grid=(n // tm,),
        in_specs=[tile, _resident(g.shape), _resident(win.shape), _resident(wgrp.shape), _resident(ps.shape),
                  _resident(wpo.shape), _resident(cw.shape), _resident(wco.shape), _resident(wmo.shape)],
        out_specs=tile,
        out_shape=jax.ShapeDtypeStruct((n, d), F32),
        scratch_shapes=[pltpu.VMEM((n_groups, tm + POOL_HALO, gw), F32),
                        pltpu.VMEM((cw.shape[1] // LANES, tm + CONV_HALO, LANES), F32)],
        compiler_params=_params("arbitrary"),
        name="mixer",
    )(x, g, win, wgrp, ps, wpo, cw, wco, wmo)


def _memkv_body(mem_ref, g_ref, wk_ref, wv_ref, kt_ref, v_ref):
    mn = _rms(mem_ref[...], g_ref[...]).astype(BF16)
    kt_ref[...] = _dot(mn, wk_ref[...]).T.astype(BF16)
    v_ref[...] = _dot(mn, wv_ref[...]).astype(BF16)


def _memkv(mem, g_mem, wk, wv):
    b, m, d = mem.shape
    depth = wk.shape[0]
    per_layer = lambda l, i: (l, 0, 0)
    return pl.pallas_call(
        _memkv_body,
        grid=(depth, b),
        in_specs=[pl.BlockSpec((None, m, d), lambda l, i: (i, 0, 0)),
                  pl.BlockSpec((None, 1, d), per_layer),
                  pl.BlockSpec((None, d, d), per_layer),
                  pl.BlockSpec((None, d, d), per_layer)],
        out_specs=[pl.BlockSpec((None, None, d, m), lambda l, i: (l, i, 0, 0)),
                   pl.BlockSpec((None, None, m, d), lambda l, i: (l, i, 0, 0))],
        out_shape=[jax.ShapeDtypeStruct((depth, b, d, m), BF16), jax.ShapeDtypeStruct((depth, b, m, d), BF16)],
        compiler_params=_params("arbitrary", "arbitrary"),
        name="memkv",
    )(mem, g_mem, wk, wv)


def _xattn_body(x_ref, g_ref, wq_ref, kt_ref, v_ref, wo_ref, o_ref):
    tm = x_ref.shape[0]
    for r0 in range(0, tm, tm // ROW_CHUNKS):
        rows = slice(r0, r0 + tm // ROW_CHUNKS)
        x = x_ref[rows, :]
        h = _rms(x, g_ref[...]).astype(BF16)
        q = _dot(h, wq_ref[...])
        dh = q.shape[1] // N_XHEADS
        sls = [slice(hd * dh, (hd + 1) * dh) for hd in range(N_XHEADS)]
        scores = [_dot(q[:, sl].astype(BF16), kt_ref[sl, :]) * (dh ** -0.5) for sl in sls]
        probs = []
        for s in scores:
            e = jnp.exp(s - jnp.max(s, axis=-1, keepdims=True))
            probs.append((e / jnp.sum(e, axis=-1, keepdims=True)).astype(BF16))
        heads = [_dot(p, v_ref[:, sl]) for p, sl in zip(probs, sls)]
        o = jnp.concatenate(heads, axis=1).astype(BF16)
        o_ref[rows, :] = x + _dot(o, wo_ref[...])


def _xattn(x, g, wq, kt, v, wo, *, layer, seq, tm):
    n, d = x.shape
    m = v.shape[2]
    assert seq % tm == 0 and d % N_XHEADS == 0
    tiles_per_seq = seq // tm
    tile = pl.BlockSpec((tm, d), lambda i: (i, 0))
    return pl.pallas_call(
        _xattn_body,
        grid=(n // tm,),
        in_specs=[tile, _resident(g.shape), _resident(wq.shape),
                  pl.BlockSpec((None, None, d, m), lambda i: (layer, i // tiles_per_seq, 0, 0)),
                  pl.BlockSpec((None, None, m, d), lambda i: (layer, i // tiles_per_seq, 0, 0)),
                  _resident(wo.shape)],
        out_specs=tile,
        out_shape=jax.ShapeDtypeStruct((n, d), F32),
        compiler_params=_params("arbitrary"),
        name="xattn",
    )(x, g, wq, kt, v, wo)


def _mixer_xattn_body(x_ref, gm_ref, win_ref, wgrp_ref, ps_ref, wpo_ref, cw_ref, wco_ref, wmo_ref,
                      gx_ref, wq_ref, kt_ref, v_ref, wo_ref, o_ref, pool_buf, conv_buf, mid, *, tm, seq):
    _mixer_body(x_ref, gm_ref, win_ref, wgrp_ref, ps_ref, wpo_ref, cw_ref, wco_ref, wmo_ref, mid,
                pool_buf, conv_buf, tm=tm, seq=seq)
    _xattn_body(mid, gx_ref, wq_ref, kt_ref, v_ref, wo_ref, o_ref)


def _mixer_xattn(x, gm, win, wgrp, ps, wpo, cw, wco, wmo, gx, wq, kt, v, wo, *, layer, seq, tm):
    n, d = x.shape
    n_groups, _, gw = wgrp.shape
    m = v.shape[2]
    assert seq % tm == 0 and n % tm == 0 and gw == LANES and cw.shape[1] % LANES == 0 and d % N_XHEADS == 0
    assert (tm // MIXER_ROW_CHUNKS) % SUBLANES == 0 and tm // MIXER_ROW_CHUNKS >= POOL_HALO
    assert len(POOL_WINDOWS) == n_groups and max(POOL_WINDOWS) <= POOL_HALO and cw.shape[0] == CONV_K
    assert all(w & (w - 1) == 0 for w in POOL_WINDOWS)
    tiles_per_seq = seq // tm
    tile = pl.BlockSpec((tm, d), lambda i: (i, 0))
    resident = [gm, win, wgrp, ps, wpo, cw, wco, wmo, gx, wq]
    return pl.pallas_call(
        functools.partial(_mixer_xattn_body, tm=tm, seq=seq),
        grid=(n // tm,),
        in_specs=[tile] + [_resident(a.shape) for a in resident]
        + [pl.BlockSpec((None, None, d, m), lambda i: (layer, i // tiles_per_seq, 0, 0)),
           pl.BlockSpec((None, None, m, d), lambda i: (layer, i // tiles_per_seq, 0, 0)),
           _resident(wo.shape)],
        out_specs=tile,
        out_shape=jax.ShapeDtypeStruct((n, d), F32),
        scratch_shapes=[pltpu.VMEM((n_groups, tm + POOL_HALO, gw), F32),
                        pltpu.VMEM((cw.shape[1] // LANES, tm + CONV_HALO, LANES), F32),
                        pltpu.VMEM((tm, d), F32)],
        compiler_params=_params("arbitrary"),
        name="mixer_xattn",
    )(x, *resident, kt, v, wo)


def _swiglu(h, wg, wu, wd):
    g = _dot(h, wg)
    return _dot((g * _sigmoid(g) * _dot(h, wu)).astype(BF16), wd)


def _ffn_body(x_ref, g_ref, wg_ref, wu_ref, wd_ref, o_ref):
    tm = x_ref.shape[0]
    for r0 in range(0, tm, tm // ROW_CHUNKS):
        rows = slice(r0, r0 + tm // ROW_CHUNKS)
        x = x_ref[rows, :]
        h = _rms(x, g_ref[...]).astype(BF16)
        o_ref[rows, :] = x + _swiglu(h, wg_ref[...], wu_ref[...], wd_ref[...])


def _ffn(x, g, wg, wu, wd, *, tm):
    n, d = x.shape
    tile = pl.BlockSpec((tm, d), lambda i: (i, 0))
    return pl.pallas_call(
        _ffn_body,
        grid=(n // tm,),
        in_specs=[tile, _resident(g.shape), _resident(wg.shape), _resident(wu.shape), _resident(wd.shape)],
        out_specs=tile,
        out_shape=jax.ShapeDtypeStruct((n, d), F32),
        compiler_params=_params("arbitrary"),
        name="ffn",
    )(x, g, wg, wu, wd)


def _split_bf16(v):
    hi = v.astype(BF16)
    return hi, (v - hi.astype(F32)).astype(BF16)


def _router_body(x_ref, g_ref, w2_ref, h_ref, idx_ref, wt_ref, cnt_ref, base_ref, *, n_exp):
    tm = x_ref.shape[0]

    @pl.when(pl.program_id(0) == 0)
    def _():
        base_ref[...] = jnp.zeros(base_ref.shape, F32)

    h = _rms(x_ref[...], g_ref[...])
    h_ref[...] = h
    h_hi, h_lo = _split_bf16(h)
    part = _dot(h_hi, w2_ref[...]) + _dot(h_lo, w2_ref[...])
    lane = lax.broadcasted_iota(jnp.int32, (tm, LANES), 1)
    logits = jnp.where(lane < n_exp, part + pltpu.roll(part, LANES - n_exp, axis=1), -jnp.inf)
    lane_f = lane.astype(F32)
    m1 = jnp.max(logits, axis=-1, keepdims=True)
    i1 = jnp.min(jnp.where(logits == m1, lane_f, float(LANES)), axis=-1, keepdims=True).astype(jnp.int32)
    rest = jnp.where(lane == i1, -jnp.inf, logits)
    m2 = jnp.max(rest, axis=-1, keepdims=True)
    i2 = jnp.min(jnp.where(rest == m2, lane_f, float(LANES)), axis=-1, keepdims=True).astype(jnp.int32)
    e2 = jnp.exp(m2 - m1)
    den = 1.0 + e2
    wt_ref[...] = jnp.where(lane == 0, 1.0 / den, jnp.where(lane == 1, e2 / den, 0.0))

    oh1, oh2 = lane == i1, lane == i2
    hits = jnp.logical_or(oh1, oh2)
    earlier = lax.broadcasted_iota(jnp.int32, (tm, tm), 1) < lax.broadcasted_iota(jnp.int32, (tm, tm), 0)
    prefix = _dot(earlier.astype(BF16), hits.astype(BF16)) + base_ref[...]
    r1 = jnp.sum(jnp.where(oh1, prefix, 0.0), axis=-1, keepdims=True).astype(jnp.int32)
    r2 = jnp.sum(jnp.where(oh2, prefix, 0.0), axis=-1, keepdims=True).astype(jnp.int32)
    idx = jnp.where(lane == 0, i1, jnp.where(lane == 1, i2, jnp.where(lane == 2, r1, jnp.where(lane == 3, r2, 0))))
    idx_ref[...] = idx.T[0:SUBLANES, :]
    base_ref[...] += jnp.sum(hits.astype(F32), axis=0, keepdims=True)
    cnt_ref[...] = jnp.broadcast_to(base_ref[...], cnt_ref.shape).astype(jnp.int32)


def _router(x, g, w_router, *, tm):
    n, d = x.shape
    n_exp = w_router.shape[1]
    assert 2 * n_exp <= LANES
    w2 = jnp.pad(jnp.concatenate(_split_bf16(w_router), axis=1), ((0, 0), (0, LANES - 2 * n_exp)))
    tile = lambda w: pl.BlockSpec((tm, w), lambda i: (i, 0))
    return pl.pallas_call(
        functools.partial(_router_body, n_exp=n_exp),
        grid=(n // tm,),
        in_specs=[tile(d), _resident(g.shape), _resident(w2.shape)],
        out_specs=[tile(d), pl.BlockSpec((SUBLANES, tm), lambda i: (i, 0)), tile(LANES),
                   pl.BlockSpec((SUBLANES, LANES), lambda i: (0, 0))],
        out_shape=[jax.ShapeDtypeStruct((n, d), F32), jax.ShapeDtypeStruct((n // tm * SUBLANES, tm), jnp.int32),
                   jax.ShapeDtypeStruct((n, LANES), F32), jax.ShapeDtypeStruct((SUBLANES, LANES), jnp.int32)],
        scratch_shapes=[pltpu.VMEM((1, LANES), F32)],
        compiler_params=_params("arbitrary"),
        name="router",
    )(x, g, w2)


def _to_row_tiles(v, dst_ref):
    c = v.shape[1] // LANES
    for j in range(c):
        dst_ref[pl.ds(j, v.shape[0], stride=c), :] = v[:, j * LANES:(j + 1) * LANES]


def _from_row_tiles(src_ref, rows, c):
    return [src_ref[pl.ds(j, rows, stride=c), :] for j in range(c)]


def _tile_copy(src, src_row8, dst, dst_row8, sem):
    return pltpu.make_async_copy(src.at[pl.ds(pl.multiple_of(src_row8, SUBLANES), SUBLANES)],
                                 dst.at[pl.ds(pl.multiple_of(dst_row8, SUBLANES), SUBLANES)], sem)


def _tiles_wait(hbm, vmem, sem):
    pltpu.make_async_copy(hbm.at[pl.ds(0, vmem.shape[0])], vmem, sem).wait()


def _dispatch_body(ends_ref, cnt_ref, pos_ref, h_ref, xs_hbm, zbuf, stage, zsem, sems, *, tmb):
    i, n = pl.program_id(0), pl.num_programs(0)
    tm = h_ref.shape[0]
    slot = lax.rem(i, 2)

    @pl.when(i == 0)
    def _():
        n_exp = ends_ref.shape[0]
        zbuf[...] = jnp.zeros(zbuf.shape, zbuf.dtype)
        for e in range(2 * n_exp):
            first = ends_ref[e] - tmb if e < n_exp else ends_ref[n_exp - 1] + (e - n_exp) * tmb
            live = cnt_ref[e] > 0 if e < n_exp else first * SUBLANES < xs_hbm.shape[0]

            @pl.when(live)
            def _():
                start = pl.multiple_of(first * SUBLANES, tmb * SUBLANES)
                fill = pltpu.make_async_copy(zbuf, xs_hbm.at[pl.ds(start, tmb * SUBLANES)], zsem)
                fill.start()
                fill.wait()

    _to_row_tiles(h_ref[...], stage.at[slot])

    def scatter(g, c):
        for j in range(SUBLANES):
            for k in range(TOP_K):
                _tile_copy(stage.at[slot], (g * SUBLANES + j) * SUBLANES,
                           xs_hbm, pos_ref[0, 0, k * tm + g * SUBLANES + j],
                           sems.at[slot]).start(priority=(j * TOP_K + k) % N_DMA_QUEUES)
        return c
    lax.fori_loop(0, tm // SUBLANES, scatter, 0)

    def drain(s):
        for _ in range(TOP_K):
            _tiles_wait(xs_hbm, stage.at[s], sems.at[s])

    @pl.when(i > 0)
    def _():
        drain(1 - slot)

    @pl.when(i == n - 1)
    def _():
        drain(slot)


def _dispatch(ends, counts, pos8, h, *, n_rows, tm, tmb):
    n, d = h.shape
    assert d == SUBLANES * LANES and tm % SUBLANES == 0
    return pl.pallas_call(
        functools.partial(_dispatch_body, tmb=tmb),
        grid_spec=pltpu.PrefetchScalarGridSpec(
            num_scalar_prefetch=2,
            grid=(n // tm,),
            in_specs=[pl.BlockSpec((1, 1, TOP_K * tm), lambda i, *_: (i, 0, 0), memory_space=pltpu.SMEM),
                      pl.BlockSpec((tm, d), lambda i, *_: (i, 0))],
            out_specs=pl.BlockSpec(memory_space=pl.ANY),
            scratch_shapes=[pltpu.VMEM((tmb * SUBLANES, LANES), F32), pltpu.VMEM((2, tm * SUBLANES, LANES), F32),
                            pltpu.SemaphoreType.DMA(()), pltpu.SemaphoreType.DMA((2,))],
        ),
        out_shape=jax.ShapeDtypeStruct((n_rows * SUBLANES, LANES), F32),
        compiler_params=_params("arbitrary"),
        name="dispatch",
    )(ends, counts, pos8, h)


def _expert_body(te_ref, nv_ref, tr_ref, xs_ref, wg_hbm, wu_hbm, wd_hbm, y_ref, hb, acc, wgb, wub, wdb, sems, *, fc):
    i = pl.program_id(0)
    tmb, d = hb.shape
    nf = wg_hbm.shape[2] // fc
    n_valid = nv_ref[0]

    def weight_copies(tile, f, slot):
        e = te_ref[tile]
        cols = pl.ds(pl.multiple_of(f * fc, fc), fc)
        return (pltpu.make_async_copy(wg_hbm.at[e, :, cols], wgb.at[slot], sems.at[0, slot]),
                pltpu.make_async_copy(wu_hbm.at[e, :, cols], wub.at[slot], sems.at[1, slot]),
                pltpu.make_async_copy(wd_hbm.at[e, cols, :], wdb.at[slot], sems.at[2, slot]))

    @pl.when(i >= n_valid)
    def _():
        y_ref[...] = jnp.zeros(y_ref.shape, F32)

    ahead = WEIGHT_BUFFERS - 1

    def run_chunk(f, first, last):
        q = i * nf + f
        wraps = f + ahead >= nf
        ahead_tile = jnp.where(wraps, i + 1, i)

        @pl.when(ahead_tile < n_valid)
        def _():
            for cp in weight_copies(ahead_tile, jnp.where(wraps, f + ahead - nf, f + ahead),
                                    lax.rem(q + ahead, WEIGHT_BUFFERS)):
                cp.start()

        slot = lax.rem(q, WEIGHT_BUFFERS)
        for cp in weight_copies(i, f, slot):
            cp.wait()

        def accumulate(m):
            if first:
                h = jnp.concatenate([col.astype(BF16) for col in _from_row_tiles(xs_ref, m, d // LANES)], axis=1)
                hb[0:m, :] = h
            else:
                h = hb[0:m, :]
            contrib = _swiglu(h, wgb[slot].astype(BF16), wub[slot].astype(BF16), wdb[slot].astype(BF16))
            total = contrib if first else acc[0:m, :] + contrib
            if last:
                _to_row_tiles(total, y_ref)
                if m < tmb:
                    y_ref[m * (d // LANES):, :] = jnp.zeros(((tmb - m) * (d // LANES), LANES), F32)
            else:
                acc[0:m, :] = total

        lo = 0
        for m in sorted(tmb // k for k in EXPERT_ROW_FRACTIONS):
            @pl.when(jnp.logical_and(tr_ref[i] > lo, tr_ref[i] <= m))
            def _():
                accumulate(m)
            lo = m

    @pl.when(i < n_valid)
    def _():
        @pl.when(i == 0)
        def _():
            for f0 in range(ahead):
                for cp in weight_copies(0, f0, f0):
                    cp.start()

        run_chunk(0, True, False)

        def middle(f, carry):
            run_chunk(f, False, False)
            return carry
        lax.fori_loop(1, nf - 1, middle, 0)
        run_chunk(nf - 1, False, True)


def _experts(tile_expert, n_valid, tile_rows, xs, wg, wu, wd, *, tmb, fc):
    n_tiles = tile_expert.shape[0]
    _, d, ff = wg.shape
    c = d // LANES
    assert ff % fc == 0 and ff // fc >= 2 and xs.shape == (n_tiles * tmb * c, LANES)
    return pl.pallas_call(
        functools.partial(_expert_body, fc=fc),
        grid_spec=pltpu.PrefetchScalarGridSpec(
            num_scalar_prefetch=3,
            grid=(n_tiles,),
            in_specs=[pl.BlockSpec((tmb * c, LANES), lambda i, te, nv, tr: (jnp.minimum(i, nv[0] - 1), 0)),
                      pl.BlockSpec(memory_space=pl.ANY), pl.BlockSpec(memory_space=pl.ANY),
                      pl.BlockSpec(memory_space=pl.ANY)],
            out_specs=pl.BlockSpec((tmb * c, LANES), lambda i, te, nv, tr: (i, 0)),
            scratch_shapes=[pltpu.VMEM((tmb, d), BF16), pltpu.VMEM((tmb, d), F32),
                            pltpu.VMEM((WEIGHT_BUFFERS, d, fc), wg.dtype), pltpu.VMEM((WEIGHT_BUFFERS, d, fc), wu.dtype),
                            pltpu.VMEM((WEIGHT_BUFFERS, fc, d), wd.dtype),
                            pltpu.SemaphoreType.DMA((3, WEIGHT_BUFFERS))],
        ),
        out_shape=jax.ShapeDtypeStruct((n_tiles * tmb * c, LANES), F32),
        compiler_params=_params("arbitrary"),
        name="experts",
    )(tile_expert, n_valid, tile_rows, xs, wg, wu, wd)


def _combine_body(pos_ref, pos_next_ref, x_ref, wt_ref, g_ref, y_hbm, o_ref, yrows, sems, *, final_norm):
    i, n = pl.program_id(0), pl.num_programs(0)
    tm, d = x_ref.shape
    slot = lax.rem(i, 2)

    def gather_group(p_ref, s, g):
        for j in range(SUBLANES):
            for k in range(TOP_K):
                _tile_copy(y_hbm, p_ref[0, 0, k * tm + g * SUBLANES + j],
                           yrows.at[s, k], (g * SUBLANES + j) * SUBLANES,
                           sems.at[s]).start(priority=(j * TOP_K + k) % N_DMA_QUEUES)

    def combine():
        ys = [_from_row_tiles(yrows.at[slot, k], tm, d // LANES) for k in range(TOP_K)]
        cols = []
        for j in range(d // LANES):
            moe = wt_ref[:, 0:1] * ys[0][j]
            for k in range(1, TOP_K):
                moe = moe + wt_ref[:, k:k + 1] * ys[k][j]
            cols.append(moe)
        out = x_ref[...] + jnp.concatenate(cols, axis=1)
        o_ref[...] = _rms(out, g_ref[...]) if final_norm else out

    @pl.when(i == 0)
    def _():
        def first(g, c):
            gather_group(pos_ref, 0, g)
            return c
        lax.fori_loop(0, tm // SUBLANES, first, 0)

    for k in range(TOP_K):
        _tiles_wait(y_hbm, yrows.at[slot, k], sems.at[slot])

    @pl.when(i + 1 < n)
    def _():
        for g in range(tm // SUBLANES):
            gather_group(pos_next_ref, 1 - slot, g)
        combine()

    @pl.when(i + 1 == n)
    def _():
        combine()


def _combine(pos8, x, wt, g, y, *, tm, final_norm):
    n, d = x.shape
    assert d == SUBLANES * LANES and tm % SUBLANES == 0
    last = n // tm - 1
    tile = pl.BlockSpec((tm, d), lambda i: (i, 0))
    return pl.pallas_call(
        functools.partial(_combine_body, final_norm=final_norm),
        grid=(n // tm,),
        in_specs=[pl.BlockSpec((1, 1, TOP_K * tm), lambda i: (i, 0, 0), memory_space=pltpu.SMEM),
                  pl.BlockSpec((1, 1, TOP_K * tm), lambda i: (jnp.minimum(i + 1, last), 0, 0),
                               memory_space=pltpu.SMEM),
                  tile, pl.BlockSpec((tm, LANES), lambda i: (i, 0)), _resident(g.shape),
                  pl.BlockSpec(memory_space=pl.ANY)],
        out_specs=tile,
        out_shape=jax.ShapeDtypeStruct((n, d), F32),
        scratch_shapes=[pltpu.VMEM((2, TOP_K, tm * SUBLANES, LANES), F32), pltpu.SemaphoreType.DMA((2,))],
        compiler_params=_params("arbitrary"),
        name="combine",
    )(pos8, pos8, x, wt, g, y)


def _routing_tables(idx, cnt, n, n_exp, tmb, tc):
    tm = idx.shape[1]
    idx = idx.reshape(n // tm, SUBLANES, tm)
    experts = jnp.arange(n_exp, dtype=jnp.int32)
    counts = cnt[0, :n_exp]
    padded = ((counts + (tmb - 1)) // tmb) * tmb
    ends = jnp.sum(jnp.where(experts[None, :] <= experts[:, None], padded[None, :], 0), axis=1)
    starts = ends - padded
    chosen = idx[:, 0:TOP_K, :]
    first = sum(jnp.where(chosen == e, starts[e], 0) for e in range(n_exp))
    pos8 = (first + idx[:, TOP_K:2 * TOP_K, :]) * SUBLANES
    pos8 = pos8.reshape(n // tm, TOP_K, tm // tc, tc).transpose(0, 2, 1, 3).reshape(n // tc, 1, TOP_K * tc)
    n_tiles = (n * TOP_K) // tmb + n_exp
    tile_start = jnp.arange(n_tiles, dtype=jnp.int32) * tmb
    tile_expert = jnp.minimum(jnp.sum((tile_start[:, None] >= ends[None, :]).astype(jnp.int32), axis=1), n_exp - 1)
    tile_rows = jnp.clip(sum(jnp.where(tile_expert == e, starts[e] + counts[e], 0) for e in range(n_exp)) - tile_start,
                         0, tmb)
    return pos8, ends, counts, tile_expert, ends[n_exp - 1:] // tmb, tile_rows, n_tiles


def _moe(x, g, w_router, wg, wu, wd, g_out, *, final_norm):
    h, idx, wt, cnt = _router(x, g, w_router, tm=ROUTER_ROW_TILE)
    pos8, ends, counts, tile_expert, n_valid, tile_rows, n_tiles = _routing_tables(
        idx, cnt, x.shape[0], wg.shape[0], EXPERT_ROW_TILE, ROW_TILE)
    xs = _dispatch(ends, counts, pos8, h, n_rows=n_tiles * EXPERT_ROW_TILE, tm=ROW_TILE, tmb=EXPERT_ROW_TILE)
    y = _experts(tile_expert, n_valid, tile_rows, xs, wg, wu, wd, tmb=EXPERT_ROW_TILE, fc=EXPERT_FF_TILE)
    return _combine(pos8, x, wt, g_out, y, tm=ROW_TILE, final_norm=final_norm)


def _final_norm_body(x_ref, g_ref, o_ref):
    o_ref[...] = _rms(x_ref[...], g_ref[...])


def _final_norm(x, g, *, tm):
    n, d = x.shape
    tile = pl.BlockSpec((tm, d), lambda i: (i, 0))
    return pl.pallas_call(
        _final_norm_body, grid=(n // tm,), in_specs=[tile, _resident(g.shape)], out_specs=tile,
        out_shape=jax.ShapeDtypeStruct((n, d), F32), compiler_params=_params("arbitrary"), name="final_norm",
    )(x, g)


def kernel(x, mem, g_mix, w_in, w_pool_group, pool_scale, w_pool_out, conv_w, w_conv_out, w_mix_out, g_xattn, g_mem, w_xq, w_xk, w_xv, w_xo, g_ffn, w_ff_gate, w_ff_up, w_ff_down, w_router, w_e_gate, w_e_up, w_e_down, g_final):
    b, s, d = x.shape
    depth = g_mix.shape[0]
    bf = lambda w: w.astype(BF16)
    row = lambda v: v.reshape(1, -1)

    kt, v = _memkv(mem, g_mem.reshape(depth, 1, d), bf(w_xk), bf(w_xv))
    xf = x.reshape(b * s, d)
    for l in range(depth):
        xf = _mixer_xattn(xf, row(g_mix[l]), bf(w_in[l]), bf(w_pool_group[l]), row(pool_scale[l]),
                          bf(w_pool_out[l]), conv_w[l], bf(w_conv_out[l]), bf(w_mix_out[l]),
                          row(g_xattn[l]), bf(w_xq[l]), kt, v, bf(w_xo[l]), layer=l, seq=s, tm=MIXER_ROW_TILE)
        last = l == depth - 1
        i = l // 2
        if l % 2 == 0:
            xf = _ffn(xf, row(g_ffn[l]), bf(w_ff_gate[i]), bf(w_ff_up[i]), bf(w_ff_down[i]), tm=ROW_TILE)
            if last:
                xf = _final_norm(xf, row(g_final), tm=ROW_TILE)
        else:
            xf = _moe(xf, row(g_ffn[l]), w_router[i], w_e_gate[i], w_e_up[i], w_e_down[i],
                      row(g_final), final_norm=last)
    return xf.reshape(b, s, d)
```
